```python
import jax, jax.numpy as jnp
from jax import lax
import numpy as np

D_MODEL = 2048
BATCH = 8
SEQ = 4096
DEPTH = 4

HEAD_DIM = 128
N_Q_HEADS = D_MODEL // HEAD_DIM
N_KV_HEADS = max(N_Q_HEADS // 4, 1)
GQA_GROUP = N_Q_HEADS // N_KV_HEADS
WINDOW = 128
BLOCK = 128
ROPE_DIM = HEAD_DIM // 4
ROPE_THETA = 500000.0
D_CONV = D_MODEL
CONV_WIDTH = 31
D_FF = ((8 * D_MODEL // 3 + 255) // 256) * 256
D_Q = N_Q_HEADS * HEAD_DIM
D_KV = N_KV_HEADS * HEAD_DIM
D_IN = D_Q + 2 * D_KV + 2 * D_CONV + 2 * D_MODEL
SPLITS = (D_Q, D_Q + D_KV, D_Q + 2 * D_KV, D_Q + 2 * D_KV + D_CONV,
          D_Q + 2 * D_KV + 2 * D_CONV, D_Q + 2 * D_KV + 2 * D_CONV + D_MODEL)
N_MOD = 6
DEEPNORM_ALPHA = (2.0 * DEPTH) ** 0.25
DEEPNORM_BETA = (8.0 * DEPTH) ** -0.25
LN_EPS = 1e-5
NEG_INF = -1e30

kernel_name = 'hybrid_swa_conformer_deepnorm_adaln'


def layer_norm(x, g, b):
    xf = x.astype(jnp.float32)
    mu = jnp.mean(xf, axis=-1, keepdims=True)
    xc = xf - mu
    var = jnp.mean(xc * xc, axis=-1, keepdims=True)
    y = xc * lax.rsqrt(var + LN_EPS) * g.astype(jnp.float32) + b.astype(jnp.float32)
    return y.astype(x.dtype)


def partial_rotary(t, cos, sin):
    half = ROPE_DIM // 2
    tf = t[..., :ROPE_DIM].astype(jnp.float32)
    t1, t2 = tf[..., :half], tf[..., half:]
    cs, sn = cos[None, :, None, :], sin[None, :, None, :]
    rot = jnp.concatenate([t1 * cs - t2 * sn, t2 * cs + t1 * sn], axis=-1).astype(t.dtype)
    return jnp.concatenate([rot, t[..., ROPE_DIM:]], axis=-1)


def _band(t, nb):
    B, _, H, D = t.shape
    tp = jnp.pad(t, ((0, 0), (BLOCK, BLOCK), (0, 0), (0, 0))).reshape(B, nb + 2, BLOCK, H, D)
    return jnp.concatenate([tp[:, :-2], tp[:, 1:-1], tp[:, 2:]], axis=2)


def window_gqa_with_sink(q, k, v, sink):
    B, S = q.shape[0], q.shape[1]
    nb = S // BLOCK
    qb = q.reshape(B, nb, BLOCK, N_KV_HEADS, GQA_GROUP, HEAD_DIM)
    kb, vb = _band(k, nb), _band(v, nb)
    s = jnp.einsum('bnqhgd,bnkhd->bnhgqk', qb, kb).astype(jnp.float32) * (HEAD_DIM ** -0.5)
    q_pos = jnp.arange(nb)[:, None] * BLOCK + jnp.arange(BLOCK)[None, :]
    k_pos = jnp.arange(nb)[:, None] * BLOCK - BLOCK + jnp.arange(3 * BLOCK)[None, :]
    valid = ((k_pos >= 0) & (k_pos < S))[:, None, :] & \
        (jnp.abs(k_pos[:, None, :] - q_pos[:, :, None]) <= WINDOW)
    s = jnp.where(valid[None, :, None, None], s, NEG_INF)
    sink_l = sink.astype(jnp.float32).reshape(N_KV_HEADS, GQA_GROUP)[None, None, :, :, None, None]
    m = jnp.maximum(jnp.max(s, axis=-1, keepdims=True), sink_l)
    p = jnp.exp(s - m)
    denom = jnp.sum(p, axis=-1, keepdims=True) + jnp.exp(sink_l - m)
    o = jnp.einsum('bnhgqk,bnkhd->bnqhgd', (p / denom).astype(v.dtype), vb)
    return o.reshape(B, S, N_Q_HEADS * HEAD_DIM)


def conformer_conv(glu_a, glu_b, w_dw, ln_g, ln_b):
    u = glu_a * jax.nn.sigmoid(glu_b)
    u = lax.conv_general_dilated(
        u, w_dw[:, None, :].astype(u.dtype), window_strides=(1,),
        padding=[(CONV_WIDTH // 2, CONV_WIDTH // 2)],
        dimension_numbers=('NWC', 'WIO', 'NWC'), feature_group_count=u.shape[-1])
    return jax.nn.silu(layer_norm(u, ln_g, ln_b))


def _fwd_setup_inputs(seed: int = 0) -> dict:
    key = jax.random.key(seed)
    ks = jax.random.split(key, 18)
    L = DEPTH

    def nrm(k, shape, scale):
        return jax.random.normal(k, shape, jnp.float32) * scale

    return {
        'x': nrm(ks[0], (BATCH, SEQ, D_MODEL), 1.0),
        'c': nrm(ks[1], (BATCH, D_MODEL), 1.0),
        'w_ada': nrm(ks[2], (L, D_MODEL, N_MOD * D_MODEL), 0.3 * D_MODEL ** -0.5),
        'b_ada': nrm(ks[3], (L, N_MOD * D_MODEL), 0.02),
        'w_in': nrm(ks[4], (L, D_MODEL, D_IN), D_MODEL ** -0.5),
        'sink': nrm(ks[5], (L, N_Q_HEADS), 1.0),
        'w_dw': nrm(ks[6], (L, CONV_WIDTH, D_CONV), CONV_WIDTH ** -0.5),
        'conv_ln_g': 1.0 + nrm(ks[7], (L, D_CONV), 0.02),
        'conv_ln_b': nrm(ks[8], (L, D_CONV), 0.02),
        'w_oa': nrm(ks[9], (L, D_Q, D_MODEL), D_Q ** -0.5),
        'w_ob': nrm(ks[10], (L, D_CONV, D_MODEL), D_CONV ** -0.5),
        'w_out': nrm(ks[11], (L, D_MODEL, D_MODEL), DEEPNORM_BETA * D_MODEL ** -0.5),
        'ln1_g': 1.0 + nrm(ks[12], (L, D_MODEL), 0.02),
        'ln1_b': nrm(ks[13], (L, D_MODEL), 0.02),
        'w_gu': nrm(ks[14], (L, D_MODEL, 2 * D_FF), D_MODEL ** -0.5),
        'w_down': nrm(ks[15], (L, D_FF, D_MODEL), DEEPNORM_BETA * D_FF ** -0.5),
        'ln2_g': 1.0 + nrm(ks[16], (L, D_MODEL), 0.02),
        'ln2_b': nrm(ks[17], (L, D_MODEL), 0.02),
    }


def _fwd_reference(x, c, w_ada, b_ada, w_in, sink, w_dw, conv_ln_g, conv_ln_b, w_oa, w_ob, w_out,
              ln1_g, ln1_b, w_gu, w_down, ln2_g, ln2_b):
    B, S, _ = x.shape
    pos = jnp.arange(S, dtype=jnp.float32)
    inv_freq = ROPE_THETA ** (-jnp.arange(0, ROPE_DIM, 2, dtype=jnp.float32) / ROPE_DIM)
    ang = pos[:, None] * inv_freq[None, :]
    cos, sin = jnp.cos(ang), jnp.sin(ang)
    c_act = jax.nn.silu(c)
    for l in range(DEPTH):
        mod = (c_act @ w_ada[l] + b_ada[l])[:, None, :]
        sh_a, sc_a, gt_a, sh_f, sc_f, gt_f = jnp.split(mod, N_MOD, axis=-1)
        h = x * (1 + sc_a) + sh_a
        q, k, v, glu_a, glu_b, g_a, g_b = jnp.split(h @ w_in[l], SPLITS, axis=-1)
        q = partial_rotary(q.reshape(B, S, N_Q_HEADS, HEAD_DIM), cos, sin)
        k = partial_rotary(k.reshape(B, S, N_KV_HEADS, HEAD_DIM), cos, sin)
        v = v.reshape(B, S, N_KV_HEADS, HEAD_DIM)
        y_a = window_gqa_with_sink(q, k, v, sink[l]) @ w_oa[l]
        y_b = conformer_conv(glu_a, glu_b, w_dw[l], conv_ln_g[l], conv_ln_b[l]) @ w_ob[l]
        merged = jax.nn.sigmoid(g_a) * y_a + jax.nn.sigmoid(g_b) * y_b
        x = layer_norm(DEEPNORM_ALPHA * x + (1 + gt_a) * (merged @ w_out[l]), ln1_g[l], ln1_b[l])
        h = x * (1 + sc_f) + sh_f
        gate, up = jnp.split(h @ w_gu[l], 2, axis=-1)
        ffn = (jax.nn.silu(gate) * up) @ w_down[l]
        x = layer_norm(DEEPNORM_ALPHA * x + (1 + gt_f) * ffn, ln2_g[l], ln2_b[l])
    return x


import jax as _jax
import jax.numpy as _jnp

TWIN_FORMAT = 'train_step'
FWD_PARAMS = ['x', 'c', 'w_ada', 'b_ada', 'w_in', 'sink', 'w_dw', 'conv_ln_g', 'conv_ln_b', 'w_oa', 'w_ob', 'w_out', 'ln1_g', 'ln1_b', 'w_gu', 'w_down', 'ln2_g', 'ln2_b']
TWIN_WEIGHTS = ['w_ada', 'b_ada', 'w_in', 'sink', 'w_dw', 'conv_ln_g', 'conv_ln_b', 'w_oa', 'w_ob', 'w_out', 'ln1_g', 'ln1_b', 'w_gu', 'w_down', 'ln2_g', 'ln2_b']
TWIN_DIFF_INPUT = 'x'
TWIN_INPUTS = ['x', 'c', 'w_ada', 'b_ada', 'w_in', 'sink', 'w_dw', 'conv_ln_g', 'conv_ln_b', 'w_oa', 'w_ob', 'w_out', 'ln1_g', 'ln1_b', 'w_gu', 'w_down', 'ln2_g', 'ln2_b', 'loss_target', 'm_w_ada', 'm_b_ada', 'm_w_in', 'm_sink', 'm_w_dw', 'm_conv_ln_g', 'm_conv_ln_b', 'm_w_oa', 'm_w_ob', 'm_w_out', 'm_ln1_g', 'm_ln1_b', 'm_w_gu', 'm_w_down', 'm_ln2_g', 'm_ln2_b', 'v_w_ada', 'v_b_ada', 'v_w_in', 'v_sink', 'v_w_dw', 'v_conv_ln_g', 'v_conv_ln_b', 'v_w_oa', 'v_w_ob', 'v_w_out', 'v_ln1_g', 'v_ln1_b', 'v_w_gu', 'v_w_down', 'v_ln2_g', 'v_ln2_b']
TWIN_OUTPUTS = ['loss', 'grad_x', 'grad_w_ada', 'grad_b_ada', 'grad_w_in', 'grad_sink', 'grad_w_dw', 'grad_conv_ln_g', 'grad_conv_ln_b', 'grad_w_oa', 'grad_w_ob', 'grad_w_out', 'grad_ln1_g', 'grad_ln1_b', 'grad_w_gu', 'grad_w_down', 'grad_ln2_g', 'grad_ln2_b', 'delta_w_ada', 'delta_b_ada', 'delta_w_in', 'delta_sink', 'delta_w_dw', 'delta_conv_ln_g', 'delta_conv_ln_b', 'delta_w_oa', 'delta_w_ob', 'delta_w_out', 'delta_ln1_g', 'delta_ln1_b', 'delta_w_gu', 'delta_w_down', 'delta_ln2_g', 'delta_ln2_b', 'new_m_w_ada', 'new_m_b_ada', 'new_m_w_in', 'new_m_sink', 'new_m_w_dw', 'new_m_conv_ln_g', 'new_m_conv_ln_b', 'new_m_w_oa', 'new_m_w_ob', 'new_m_w_out', 'new_m_ln1_g', 'new_m_ln1_b', 'new_m_w_gu', 'new_m_w_down', 'new_m_ln2_g', 'new_m_ln2_b', 'new_v_w_ada', 'new_v_b_ada', 'new_v_w_in', 'new_v_sink', 'new_v_w_dw', 'new_v_conv_ln_g', 'new_v_conv_ln_b', 'new_v_w_oa', 'new_v_w_ob', 'new_v_w_out', 'new_v_ln1_g', 'new_v_ln1_b', 'new_v_w_gu', 'new_v_w_down', 'new_v_ln2_g', 'new_v_ln2_b']
TWIN_LEAF_KINDS = {'loss': 'loss', 'grad_x': 'grad_x', 'grad_w_ada': 'grad_w', 'grad_b_ada': 'grad_w', 'grad_w_in': 'grad_w', 'grad_sink': 'grad_w', 'grad_w_dw': 'grad_w', 'grad_conv_ln_g': 'grad_w', 'grad_conv_ln_b': 'grad_w', 'grad_w_oa': 'grad_w', 'grad_w_ob': 'grad_w', 'grad_w_out': 'grad_w', 'grad_ln1_g': 'grad_w', 'grad_ln1_b': 'grad_w', 'grad_w_gu': 'grad_w', 'grad_w_down': 'grad_w', 'grad_ln2_g': 'grad_w', 'grad_ln2_b': 'grad_w', 'delta_w_ada': 'delta_w', 'delta_b_ada': 'delta_w', 'delta_w_in': 'delta_w', 'delta_sink': 'delta_w', 'delta_w_dw': 'delta_w', 'delta_conv_ln_g': 'delta_w', 'delta_conv_ln_b': 'delta_w', 'delta_w_oa': 'delta_w', 'delta_w_ob': 'delta_w', 'delta_w_out': 'delta_w', 'delta_ln1_g': 'delta_w', 'delta_ln1_b': 'delta_w', 'delta_w_gu': 'delta_w', 'delta_w_down': 'delta_w', 'delta_ln2_g': 'delta_w', 'delta_ln2_b': 'delta_w', 'new_m_w_ada': 'new_m', 'new_m_b_ada': 'new_m', 'new_m_w_in': 'new_m', 'new_m_sink': 'new_m', 'new_m_w_dw': 'new_m', 'new_m_conv_ln_g': 'new_m', 'new_m_conv_ln_b': 'new_m', 'new_m_w_oa': 'new_m', 'new_m_w_ob': 'new_m', 'new_m_w_out': 'new_m', 'new_m_ln1_g': 'new_m', 'new_m_ln1_b': 'new_m', 'new_m_w_gu': 'new_m', 'new_m_w_down': 'new_m', 'new_m_ln2_g': 'new_m', 'new_m_ln2_b': 'new_m', 'new_v_w_ada': 'new_v', 'new_v_b_ada': 'new_v', 'new_v_w_in': 'new_v', 'new_v_sink': 'new_v', 'new_v_w_dw': 'new_v', 'new_v_conv_ln_g': 'new_v', 'new_v_conv_ln_b': 'new_v', 'new_v_w_oa': 'new_v', 'new_v_w_ob': 'new_v', 'new_v_w_out': 'new_v', 'new_v_ln1_g': 'new_v', 'new_v_ln1_b': 'new_v', 'new_v_w_gu': 'new_v', 'new_v_w_down': 'new_v', 'new_v_ln2_g': 'new_v', 'new_v_ln2_b': 'new_v'}


def _forward(args):
    return _fwd_reference(*[args[k] for k in FWD_PARAMS])


def _output_shape():
    def fwd():
        inp = _fwd_setup_inputs(0)
        return _fwd_reference(*[inp[k] for k in FWD_PARAMS])
    out = _jax.eval_shape(fwd)
    return out.shape, out.dtype

N_MICROBATCH = 1
ADAM_LR = 0.001
ADAM_B1 = 0.9
ADAM_B2 = 0.999
ADAM_EPS = 1e-08
ADAM_WD = 0.01
ADAM_STEP = 10
PER_EXAMPLE_BATCH_AXIS = {'x': 0, 'c': 0, 'loss_target': 0}
SHARED_INPUTS = []
_WEIGHT_DTYPES = {'w_ada': _jnp.float32, 'b_ada': _jnp.float32, 'w_in': _jnp.float32, 'sink': _jnp.float32, 'w_dw': _jnp.float32, 'conv_ln_g': _jnp.float32, 'conv_ln_b': _jnp.float32, 'w_oa': _jnp.float32, 'w_ob': _jnp.float32, 'w_out': _jnp.float32, 'ln1_g': _jnp.float32, 'ln1_b': _jnp.float32, 'w_gu': _jnp.float32, 'w_down': _jnp.float32, 'ln2_g': _jnp.float32, 'ln2_b': _jnp.float32}
MOMENT_SCALE = {'w_ada': 9.981545e-03, 'b_ada': 2.112568e-02, 'w_in': 4.276881e-03, 'sink': 3.124296e-04, 'w_dw': 8.056356e-03, 'conv_ln_g': 1.324066e-02, 'conv_ln_b': 1.672120e-02, 'w_oa': 3.812631e-03, 'w_ob': 9.480389e-03, 'w_out': 2.446826e-02, 'ln1_g': 5.524781e-01, 'ln1_b': 2.749227e-01, 'w_gu': 8.912190e-03, 'w_down': 3.468613e-02, 'ln2_g': 8.049022e+00, 'ln2_b': 6.310430e-01}


def _to_microbatches(a, axis):
    t = _jnp.moveaxis(a, axis, 0)
    t = t.reshape((N_MICROBATCH, t.shape[0] // N_MICROBATCH) + t.shape[1:])
    return _jnp.moveaxis(t, 1, axis + 1)


def setup_inputs(seed: int = 0) -> dict:
    inp = _fwd_setup_inputs(seed)
    key = _jax.random.fold_in(_jax.random.key(seed), 7919)
    shape, _ = _output_shape()
    out = dict(inp)
    out["loss_target"] = _jax.random.normal(_jax.random.fold_in(key, 0), shape, _jnp.float32)
    for i, name in enumerate(TWIN_WEIGHTS):
        w = inp[name].astype(_jnp.float32)
        if MOMENT_SCALE is None:
            s = _jnp.sqrt(_jnp.mean(_jnp.square(w)) + 1e-30)
        else:
            s = MOMENT_SCALE[name]
        km, kv = _jax.random.split(_jax.random.fold_in(key, i + 1))
        out[name] = w
        out["m_" + name] = s * _jax.random.normal(km, w.shape, _jnp.float32)
        out["v_" + name] = (s * s) * _jax.random.uniform(kv, w.shape, _jnp.float32, 0.5, 1.5)
    if N_MICROBATCH > 1:
        for name, axis in PER_EXAMPLE_BATCH_AXIS.items():
            out[name] = _to_microbatches(out[name], axis)
    return {'x': out['x'], 'c': out['c'], 'w_ada': out['w_ada'], 'b_ada': out['b_ada'], 'w_in': out['w_in'], 'sink': out['sink'], 'w_dw': out['w_dw'], 'conv_ln_g': out['conv_ln_g'], 'conv_ln_b': out['conv_ln_b'], 'w_oa': out['w_oa'], 'w_ob': out['w_ob'], 'w_out': out['w_out'], 'ln1_g': out['ln1_g'], 'ln1_b': out['ln1_b'], 'w_gu': out['w_gu'], 'w_down': out['w_down'], 'ln2_g': out['ln2_g'], 'ln2_b': out['ln2_b'], 'loss_target': out['loss_target'], 'm_w_ada': out['m_w_ada'], 'm_b_ada': out['m_b_ada'], 'm_w_in': out['m_w_in'], 'm_sink': out['m_sink'], 'm_w_dw': out['m_w_dw'], 'm_conv_ln_g': out['m_conv_ln_g'], 'm_conv_ln_b': out['m_conv_ln_b'], 'm_w_oa': out['m_w_oa'], 'm_w_ob': out['m_w_ob'], 'm_w_out': out['m_w_out'], 'm_ln1_g': out['m_ln1_g'], 'm_ln1_b': out['m_ln1_b'], 'm_w_gu': out['m_w_gu'], 'm_w_down': out['m_w_down'], 'm_ln2_g': out['m_ln2_g'], 'm_ln2_b': out['m_ln2_b'], 'v_w_ada': out['v_w_ada'], 'v_b_ada': out['v_b_ada'], 'v_w_in': out['v_w_in'], 'v_sink': out['v_sink'], 'v_w_dw': out['v_w_dw'], 'v_conv_ln_g': out['v_conv_ln_g'], 'v_conv_ln_b': out['v_conv_ln_b'], 'v_w_oa': out['v_w_oa'], 'v_w_ob': out['v_w_ob'], 'v_w_out': out['v_w_out'], 'v_ln1_g': out['v_ln1_g'], 'v_ln1_b': out['v_ln1_b'], 'v_w_gu': out['v_w_gu'], 'v_w_down': out['v_w_down'], 'v_ln2_g': out['v_ln2_g'], 'v_ln2_b': out['v_ln2_b']}


def _loss(weights, diff, rest, loss_target):
    with _jax.named_scope("forward"):
        args = {**rest, TWIN_DIFF_INPUT: diff, **{k: w.astype(_WEIGHT_DTYPES[k]) for k, w in weights.items()}}
        y = _forward(args)
    with _jax.named_scope("loss_head"):
        err = _jnp.square(y.astype(_jnp.float32) - loss_target)
        return 0.5 * _jnp.sum(_jnp.mean(err, axis=-1)) if err.ndim else 0.5 * err


def _adamw(w, g, m, v):
    m = ADAM_B1 * m + (1.0 - ADAM_B1) * g
    v = ADAM_B2 * v + (1.0 - ADAM_B2) * _jnp.square(g)
    m_hat = m / (1.0 - ADAM_B1 ** ADAM_STEP)
    v_hat = v / (1.0 - ADAM_B2 ** ADAM_STEP)
    delta = -ADAM_LR * (m_hat / (_jnp.sqrt(v_hat) + ADAM_EPS) + ADAM_WD * w)
    return delta, m, v


def reference(x, c, w_ada, b_ada, w_in, sink, w_dw, conv_ln_g, conv_ln_b, w_oa, w_ob, w_out, ln1_g, ln1_b, w_gu, w_down, ln2_g, ln2_b, loss_target, m_w_ada, m_b_ada, m_w_in, m_sink, m_w_dw, m_conv_ln_g, m_conv_ln_b, m_w_oa, m_w_ob, m_w_out, m_ln1_g, m_ln1_b, m_w_gu, m_w_down, m_ln2_g, m_ln2_b, v_w_ada, v_b_ada, v_w_in, v_sink, v_w_dw, v_conv_ln_g, v_conv_ln_b, v_w_oa, v_w_ob, v_w_out, v_ln1_g, v_ln1_b, v_w_gu, v_w_down, v_ln2_g, v_ln2_b):
    given = dict(x=x, c=c, w_ada=w_ada, b_ada=b_ada, w_in=w_in, sink=sink, w_dw=w_dw, conv_ln_g=conv_ln_g, conv_ln_b=conv_ln_b, w_oa=w_oa, w_ob=w_ob, w_out=w_out, ln1_g=ln1_g, ln1_b=ln1_b, w_gu=w_gu, w_down=w_down, ln2_g=ln2_g, ln2_b=ln2_b, loss_target=loss_target, m_w_ada=m_w_ada, m_b_ada=m_b_ada, m_w_in=m_w_in, m_sink=m_sink, m_w_dw=m_w_dw, m_conv_ln_g=m_conv_ln_g, m_conv_ln_b=m_conv_ln_b, m_w_oa=m_w_oa, m_w_ob=m_w_ob, m_w_out=m_w_out, m_ln1_g=m_ln1_g, m_ln1_b=m_ln1_b, m_w_gu=m_w_gu, m_w_down=m_w_down, m_ln2_g=m_ln2_g, m_ln2_b=m_ln2_b, v_w_ada=v_w_ada, v_b_ada=v_b_ada, v_w_in=v_w_in, v_sink=v_sink, v_w_dw=v_w_dw, v_conv_ln_g=v_conv_ln_g, v_conv_ln_b=v_conv_ln_b, v_w_oa=v_w_oa, v_w_ob=v_w_ob, v_w_out=v_w_out, v_ln1_g=v_ln1_g, v_ln1_b=v_ln1_b, v_w_gu=v_w_gu, v_w_down=v_w_down, v_ln2_g=v_ln2_g, v_ln2_b=v_ln2_b)
    weights = {n: given[n] for n in TWIN_WEIGHTS}
    shared = {n: given[n] for n in SHARED_INPUTS}
    per_example = {n: given[n] for n in ['x', 'c']}
    grad_fn = _jax.value_and_grad(_loss, argnums=(0, 1))

    def one_microbatch(ex, loss_target):
        ex = dict(ex)
        diff = ex.pop(TWIN_DIFF_INPUT)
        return grad_fn(weights, diff, {**shared, **ex}, loss_target)

    if N_MICROBATCH == 1:
        loss, (grad_w, grad_x) = one_microbatch(per_example, given["loss_target"])
    else:
        def body(carry, xs):
            loss_sum, grad_sum = carry
            l_k, (gw_k, gx_k) = one_microbatch(xs[0], xs[1])
            with _jax.named_scope("update"):
                return (loss_sum + l_k, _jax.tree.map(_jnp.add, grad_sum, gw_k)), gx_k

        init = (_jnp.zeros((), _jnp.float32), _jax.tree.map(_jnp.zeros_like, weights))
        (loss, grad_w), grad_x = _jax.lax.scan(body, init, (per_example, given["loss_target"]))
    with _jax.named_scope("update"):
        delta_w, new_m, new_v = {}, {}, {}
        for n in TWIN_WEIGHTS:
            delta_w[n], new_m[n], new_v[n] = _adamw(weights[n], grad_w[n], given["m_" + n], given["v_" + n])
    return (loss, grad_x, *[grad_w[n] for n in TWIN_WEIGHTS], *[delta_w[n] for n in TWIN_WEIGHTS],
            *[new_m[n] for n in TWIN_WEIGHTS], *[new_v[n] for n in TWIN_WEIGHTS])
```

```python
import math

import jax
import jax.numpy as jnp
from jax import lax
from jax.experimental import pallas as pl
from jax.experimental.pallas import tpu as pltpu

F32 = jnp.float32
BF16 = jnp.bfloat16
N_DEV = 8
AXES = ("x", "y", "c")
HEAD_DIM = 128
GQA_GROUP = 4
BLOCK = 128
ROPE_DIM = HEAD_DIM // 4
ROPE_HALF = ROPE_DIM // 2
ROPE_THETA = 500000.0
CONV_WIDTH = 31
CONV_HALO = 16
N_MOD = 6
LN_EPS = 1e-5
NEG_INF = -1e30
ADAM_LR, ADAM_B1, ADAM_B2, ADAM_EPS, ADAM_WD, ADAM_STEP = 0.001, 0.9, 0.999, 1e-08, 0.01, 10
VMEM_LIMIT = 56 * 1024 * 1024
LANE = 128
ROW_CHUNK = 16
SMALL_W = 512


def _params(*sem):
    return pltpu.CompilerParams(dimension_semantics=sem, vmem_limit_bytes=VMEM_LIMIT)


def _pick(dim, pref, mult=LANE):
    if dim <= pref:
        return dim
    best = None
    for t in range(mult, pref + 1, mult):
        if dim % t == 0:
            best = t
    assert best is not None, (dim, pref)
    return best


def matmul(name, a, b, mode, out_dtype, *, b_off=0, n=None, tm=512, tn=1024, tk=1024):
    if mode == "nn":
        (M, K), N = a.shape, (n or b.shape[1])
    elif mode == "tn":
        (K, M), N = a.shape, b.shape[1]
    else:
        (M, K), N = a.shape, b.shape[0]
    tm, tn, tk = _pick(M, tm, 16), _pick(math.gcd(N, b_off) if b_off else N, tn), _pick(K, tk)
    assert b_off % tn == 0 and N % tn == 0
    boff = b_off // tn
    nk = K // tk
    if mode == "nn":
        a_spec = pl.BlockSpec((tm, tk), lambda i, j, k: (i, k))
        b_spec = pl.BlockSpec((tk, tn), lambda i, j, k: (k, j + boff))
        dims = (((1,), (0,)), ((), ()))
    elif mode == "tn":
        a_spec = pl.BlockSpec((tk, tm), lambda i, j, k: (k, i))
        b_spec = pl.BlockSpec((tk, tn), lambda i, j, k: (k, j))
        dims = (((0,), (0,)), ((), ()))
    else:
        a_spec = pl.BlockSpec((tm, tk), lambda i, j, k: (i, k))
        b_spec = pl.BlockSpec((tn, tk), lambda i, j, k: (j, k))
        dims = (((1,), (1,)), ((), ()))

    def body(a_ref, b_ref, o_ref, acc_ref):
        k = pl.program_id(2)
        part = lax.dot_general(a_ref[...].astype(BF16), b_ref[...].astype(BF16), dims,
                               preferred_element_type=F32)
        if nk == 1:
            o_ref[...] = part.astype(out_dtype)
        else:
            @pl.when(k == 0)
            def _():
                acc_ref[...] = part

            @pl.when(k > 0)
            def _():
                acc_ref[...] += part

            @pl.when(k == nk - 1)
            def _():
                o_ref[...] = acc_ref[...].astype(out_dtype)

    return pl.pallas_call(
        body, name=name, grid=(M // tm, N // tn, nk),
        in_specs=[a_spec, b_spec],
        out_specs=pl.BlockSpec((tm, tn), lambda i, j, k: (i, j)),
        out_shape=jax.ShapeDtypeStruct((M, N), out_dtype),
        scratch_shapes=[pltpu.VMEM((tm, tn), F32)],
        compiler_params=_params("parallel", "parallel", "arbitrary"),
    )(a, b)


def rowwise(name, fn, nrows, row_ins, vec_ins, row_outs, n_sums=0, *, W, tr=256, cw=None):
    cw = cw or W
    assert W % cw == 0 and nrows % ROW_CHUNK == 0
    tr = _pick(nrows, tr, ROW_CHUNK)
    nrt = nrows // tr
    n_ri, n_v, n_ro = len(row_ins), len(vec_ins), len(row_outs)

    def row_spec(lead, colblk, shift, whole):
        w = cw if whole else W

        def rmap(i):
            return jnp.clip(i + shift, 0, nrt - 1) if shift else i
        if lead is None:
            return pl.BlockSpec((tr, w), lambda i: (rmap(i), colblk))
        return pl.BlockSpec((None, tr, w), lambda i: (lead, rmap(i), colblk))

    in_specs = [row_spec(*ri[1:]) for ri in row_ins]
    in_specs += [pl.BlockSpec((1, W), lambda i: (0, 0)) for _ in vec_ins]
    out_specs = [pl.BlockSpec((tr, p * W), lambda i: (i, 0)) for _, p in row_outs]
    out_specs += [pl.BlockSpec((1, W), lambda i: (0, 0)) for _ in range(n_sums)]
    out_shape = [jax.ShapeDtypeStruct((nrows, p * W), dt) for dt, p in row_outs]
    out_shape += [jax.ShapeDtypeStruct((1, W), F32) for _ in range(n_sums)]

    def body(*refs):
        rin, vin = refs[:n_ri], refs[n_ri:n_ri + n_v]
        rout = refs[n_ri + n_v:n_ri + n_v + n_ro]
        sout = refs[n_ri + n_v + n_ro:n_ri + n_v + n_ro + n_sums]
        acc = refs[n_ri + n_v + n_ro + n_sums:]
        i = pl.program_id(0)
        if n_sums:
            @pl.when(i == 0)
            def _():
                for a in acc:
                    a[...] = jnp.zeros_like(a)
        for c in range(W // cw):
            c0 = c * cw
            vecs = [v[:, c0:c0 + cw] for v in vin]

            def step(r, carry, c=c, c0=c0, vecs=vecs):
                r0 = pl.multiple_of(r * ROW_CHUNK, ROW_CHUNK)
                rows = [ref[pl.ds(r0, ROW_CHUNK), :] if ri[4] else ref[pl.ds(r0, ROW_CHUNK), c0:c0 + cw]
                        for ref, ri in zip(rin, row_ins)]
                outs, sums = fn(rows, vecs, c, i)
                for oref, (dt, _), pieces in zip(rout, row_outs, outs):
                    for pi, piece in enumerate(pieces):
                        oref[pl.ds(r0, ROW_CHUNK), pi * W + c0:pi * W + c0 + cw] = piece.astype(dt)
                for a, s in zip(acc, sums):
                    a[:, c0:c0 + cw] += s
                return carry
            lax.fori_loop(0, tr // ROW_CHUNK, step, 0)
        if n_sums:
            @pl.when(i == nrt - 1)
            def _():
                for o, a in zip(sout, acc):
                    o[...] = jnp.sum(a[...], axis=0, keepdims=True)

    res = pl.pallas_call(
        body, name=name, grid=(nrt,), in_specs=in_specs, out_specs=out_specs, out_shape=out_shape,
        scratch_shapes=[pltpu.VMEM((ROW_CHUNK, W), F32) for _ in range(n_sums)],
        compiler_params=_params("arbitrary"),
    )(*[ri[0] for ri in row_ins], *vec_ins)
    return res


def _ri(arr, lead=None, col=0, shift=0, whole=False):
    return (arr, lead, col, shift, whole)


def _sig(x):
    return jax.nn.sigmoid(x)


def _ln_stats(t):
    mu = jnp.mean(t, axis=-1, keepdims=True)
    xc = t - mu
    var = jnp.mean(xc * xc, axis=-1, keepdims=True)
    rstd = lax.rsqrt(var + LN_EPS)
    return xc * rstd, rstd


def _ln_bwd(dy_g, xhat, rstd):
    m1 = jnp.mean(dy_g, axis=-1, keepdims=True)
    m2 = jnp.mean(dy_g * xhat, axis=-1, keepdims=True)
    return rstd * (dy_g - m1 - xhat * m2)


def _rope(x, cos, s_lo, s_hi, sign):
    up = pltpu.roll(x, HEAD_DIM - ROPE_HALF, 1)
    down = pltpu.roll(x, ROPE_HALF, 1)
    return x * cos + sign * (up * s_lo + down * s_hi)


def rope_tables(S):
    pos = jnp.arange(S, dtype=F32)
    inv_freq = ROPE_THETA ** (-jnp.arange(0, ROPE_DIM, 2, dtype=F32) / ROPE_DIM)
    ang = pos[:, None] * inv_freq[None, :]
    cos, sin = jnp.cos(ang), jnp.sin(ang)
    ones = jnp.ones((S, HEAD_DIM - ROPE_DIM), F32)
    zeros = jnp.zeros((S, HEAD_DIM - ROPE_DIM), F32)
    zh = jnp.zeros((S, ROPE_HALF), F32)
    t_cos = jnp.concatenate([cos, cos, ones], axis=1)
    t_lo = jnp.concatenate([-sin, zh, zeros], axis=1)
    t_hi = jnp.concatenate([zh, sin, zeros], axis=1)
    return t_cos, t_lo, t_hi


def _attn_specs(S, D, Dkv):
    nb, nkv, qb = S // BLOCK, Dkv // HEAD_DIM, D // HEAD_DIM
    gw = GQA_GROUP * HEAD_DIM
    q_spec = pl.BlockSpec((BLOCK, gw), lambda h, n: (n, h))

    def band(col0):
        return [pl.BlockSpec((BLOCK, HEAD_DIM), lambda h, n: (jnp.maximum(n - 1, 0), col0 + h)),
                pl.BlockSpec((BLOCK, HEAD_DIM), lambda h, n: (n, col0 + h)),
                pl.BlockSpec((BLOCK, HEAD_DIM), lambda h, n: (jnp.minimum(n + 1, nb - 1), col0 + h))]
    sink_spec = pl.BlockSpec((None, GQA_GROUP * BLOCK, 1), lambda h, n: (h, 0, 0))
    return nb, nkv, gw, q_spec, band(qb), band(qb + nkv), sink_spec


def _attn_probs(q_ref, k_refs, sink_ref, n, S):
    q = q_ref[...]
    qs = jnp.concatenate([q[:, g * HEAD_DIM:(g + 1) * HEAD_DIM] for g in range(GQA_GROUP)], axis=0)
    kb = jnp.concatenate([r[...] for r in k_refs], axis=0)
    s = lax.dot_general(qs, kb, (((1,), (1,)), ((), ())), preferred_element_type=F32) * (HEAD_DIM ** -0.5)
    shape = (GQA_GROUP * BLOCK, 3 * BLOCK)
    row = lax.broadcasted_iota(jnp.int32, shape, 0) & (BLOCK - 1)
    col = lax.broadcasted_iota(jnp.int32, shape, 1)
    rel = col - BLOCK - row
    kpos = (n - 1) * BLOCK + col
    valid = (jnp.abs(rel) <= BLOCK) & (kpos >= 0) & (kpos < S)
    s = jnp.where(valid, s, NEG_INF)
    sink = sink_ref[...]
    m = jnp.maximum(jnp.max(s, axis=-1, keepdims=True), sink)
    p = jnp.exp(s - m)
    e_sink = jnp.exp(sink - m)
    denom = jnp.sum(p, axis=-1, keepdims=True) + e_sink
    return qs, kb, p / denom, e_sink / denom


def attn_fwd(qkv, sinkcol, S, D, Dkv):
    nb, nkv, gw, q_spec, k_specs, v_specs, sink_spec = _attn_specs(S, D, Dkv)

    def body(q_ref, k0, k1, k2, v0, v1, v2, sink_ref, o_ref):
        n = pl.program_id(1)
        _, _, w, _ = _attn_probs(q_ref, (k0, k1, k2), sink_ref, n, S)
        vb = jnp.concatenate([v0[...], v1[...], v2[...]], axis=0)
        o = jnp.dot(w.astype(BF16), vb, preferred_element_type=F32)
        for g in range(GQA_GROUP):
            o_ref[:, g * HEAD_DIM:(g + 1) * HEAD_DIM] = o[g * BLOCK:(g + 1) * BLOCK].astype(BF16)

    return pl.pallas_call(
        body, name="attn_fwd", grid=(nkv, nb),
        in_specs=[q_spec, *k_specs, *v_specs, sink_spec],
        out_specs=pl.BlockSpec((BLOCK, gw), lambda h, n: (n, h)),
        out_shape=jax.ShapeDtypeStruct((S, D), BF16),
        compiler_params=_params("parallel", "arbitrary"),
    )(qkv, qkv, qkv, qkv, qkv, qkv, qkv, sinkcol)


def attn_bwd(qkv, sinkcol, datt, S, D, Dkv):
    nb, nkv, gw, q_spec, k_specs, v_specs, sink_spec = _attn_specs(S, D, Dkv)

    def body(q_ref, k0, k1, k2, v0, v1, v2, sink_ref, do_ref, dq_ref, dkp_ref, dvp_ref, dsink_ref):
        n = pl.program_id(1)
        qs, kb, w, w_sink = _attn_probs(q_ref, (k0, k1, k2), sink_ref, n, S)
        vb = jnp.concatenate([v0[...], v1[...], v2[...]], axis=0)
        do = do_ref[...]
        dos = jnp.concatenate([do[:, g * HEAD_DIM:(g + 1) * HEAD_DIM] for g in range(GQA_GROUP)], axis=0)
        dv = lax.dot_general(w.astype(BF16), dos, (((0,), (0,)), ((), ())), preferred_element_type=F32)
        dw = lax.dot_general(dos, vb, (((1,), (1,)), ((), ())), preferred_element_type=F32)
        delta = jnp.sum(w * dw, axis=-1, keepdims=True)
        ds = (w * (dw - delta) * (HEAD_DIM ** -0.5)).astype(BF16)
        dq = jnp.dot(ds, kb, preferred_element_type=F32)
        dk = lax.dot_general(ds, qs, (((0,), (0,)), ((), ())), preferred_element_type=F32)
        for g in range(GQA_GROUP):
            dq_ref[:, g * HEAD_DIM:(g + 1) * HEAD_DIM] = dq[g * BLOCK:(g + 1) * BLOCK]
        for j in range(3):
            dkp_ref[j] = dk[j * BLOCK:(j + 1) * BLOCK]
            dvp_ref[j] = dv[j * BLOCK:(j + 1) * BLOCK]

        @pl.when(n == 0)
        def _():
            dsink_ref[...] = jnp.zeros_like(dsink_ref)
        t = w_sink * delta
        for g in range(GQA_GROUP):
            dsink_ref[g:g + 1, :] -= jnp.sum(t[g * BLOCK:(g + 1) * BLOCK], axis=0, keepdims=True)

    part_spec = pl.BlockSpec((3, BLOCK, HEAD_DIM), lambda h, n: (0, n, h))
    return pl.pallas_call(
        body, name="attn_bwd", grid=(nkv, nb),
        in_specs=[q_spec, *k_specs, *v_specs, sink_spec, pl.BlockSpec((BLOCK, gw), lambda h, n: (n, h))],
        out_specs=[pl.BlockSpec((BLOCK, gw), lambda h, n: (n, h)), part_spec, part_spec,
                   pl.BlockSpec((None, 8, LANE), lambda h, n: (h, 0, 0))],
        out_shape=[jax.ShapeDtypeStruct((S, D), F32), jax.ShapeDtypeStruct((3, S, Dkv), F32),
                   jax.ShapeDtypeStruct((3, S, Dkv), F32), jax.ShapeDtypeStruct((nkv, 8, LANE), F32)],
        compiler_params=_params("parallel", "arbitrary"),
    )(qkv, qkv, qkv, qkv, qkv, qkv, qkv, sinkcol, datt)


CONV_TC = 128
CONV_ROWS = 32


def _conv_fill_u(ga_ref, gb_ref, upad_ref, S):
    tc = upad_ref.shape[1]
    zero = jnp.zeros((CONV_HALO, tc), F32)
    upad_ref[0:CONV_HALO, :] = zero
    upad_ref[S + CONV_HALO:S + 2 * CONV_HALO, :] = zero

    def fill(r, carry):
        r0 = pl.multiple_of(r * CONV_ROWS, CONV_ROWS)
        upad_ref[pl.ds(r0 + CONV_HALO, CONV_ROWS), :] = ga_ref[pl.ds(r0, CONV_ROWS), :] * _sig(gb_ref[pl.ds(r0, CONV_ROWS), :])
        return carry
    lax.fori_loop(0, S // CONV_ROWS, fill, 0)


def conv_fwd(pglu, wdw, S, D):
    tc = min(CONV_TC, D)
    nct = D // tc

    def body(ga_ref, gb_ref, w_ref, o_ref, upad_ref):
        _conv_fill_u(ga_ref, gb_ref, upad_ref, S)
        w = w_ref[...]

        def step(r, carry):
            r0 = pl.multiple_of(r * CONV_ROWS, CONV_ROWS)
            win = upad_ref[pl.ds(r0, CONV_ROWS + 2 * CONV_HALO), :]
            acc = jnp.zeros((CONV_ROWS, tc), F32)
            for k in range(CONV_WIDTH):
                o = k + CONV_HALO - CONV_WIDTH // 2
                acc = acc + w[k:k + 1, :] * win[o:o + CONV_ROWS, :]
            o_ref[pl.ds(r0, CONV_ROWS), :] = acc
            return carry
        lax.fori_loop(0, S // CONV_ROWS, step, 0)

    return pl.pallas_call(
        body, name="conv_fwd", grid=(nct,),
        in_specs=[pl.BlockSpec((S, tc), lambda j: (0, j)), pl.BlockSpec((S, tc), lambda j: (0, nct + j)),
                  pl.BlockSpec((CONV_WIDTH, tc), lambda j: (0, j))],
        out_specs=pl.BlockSpec((S, tc), lambda j: (0, j)),
        out_shape=jax.ShapeDtypeStruct((S, D), F32),
        scratch_shapes=[pltpu.VMEM((S + 2 * CONV_HALO, tc), F32)],
        compiler_params=_params("parallel"),
    )(pglu, pglu, wdw)


def conv_bwd(duc, pglu, wdw, S, D):
    tc = min(CONV_TC, D)
    nct = D // tc
    half = CONV_WIDTH // 2

    def body(d_ref, ga_ref, gb_ref, w_ref, dga_ref, dgb_ref, dw_ref, upad_ref, dpad_ref, dwacc_ref):
        _conv_fill_u(ga_ref, gb_ref, upad_ref, S)
        zero = jnp.zeros((CONV_HALO, tc), F32)
        dpad_ref[0:CONV_HALO, :] = zero
        dpad_ref[S + CONV_HALO:S + 2 * CONV_HALO, :] = zero

        def fill(r, carry):
            r0 = pl.multiple_of(r * CONV_ROWS, CONV_ROWS)
            dpad_ref[pl.ds(r0 + CONV_HALO, CONV_ROWS), :] = d_ref[pl.ds(r0, CONV_ROWS), :]
            return carry
        lax.fori_loop(0, S // CONV_ROWS, fill, 0)
        dwacc_ref[...] = jnp.zeros_like(dwacc_ref)
        w = w_ref[...]

        def step(r, carry):
            r0 = pl.multiple_of(r * CONV_ROWS, CONV_ROWS)
            uwin = upad_ref[pl.ds(r0, CONV_ROWS + 2 * CONV_HALO), :]
            dwin = dpad_ref[pl.ds(r0, CONV_ROWS + 2 * CONV_HALO), :]
            d = dwin[CONV_HALO:CONV_HALO + CONV_ROWS, :]
            du = jnp.zeros((CONV_ROWS, tc), F32)
            for k in range(CONV_WIDTH):
                o = CONV_HALO + half - k
                du = du + w[k:k + 1, :] * dwin[o:o + CONV_ROWS, :]
                o = CONV_HALO + k - half
                prod = d * uwin[o:o + CONV_ROWS, :]
                part = prod[0:8]
                for q in range(1, CONV_ROWS // 8):
                    part = part + prod[8 * q:8 * q + 8]
                dwacc_ref[k] += part
            ga = ga_ref[pl.ds(r0, CONV_ROWS), :]
            sg = _sig(gb_ref[pl.ds(r0, CONV_ROWS), :])
            dga_ref[pl.ds(r0, CONV_ROWS), :] = (du * sg).astype(BF16)
            dgb_ref[pl.ds(r0, CONV_ROWS), :] = (du * ga * sg * (1.0 - sg)).astype(BF16)
            return carry
        lax.fori_loop(0, S // CONV_ROWS, step, 0)
        dw_ref[...] = jnp.sum(dwacc_ref[...], axis=1)

    return pl.pallas_call(
        body, name="conv_bwd", grid=(nct,),
        in_specs=[pl.BlockSpec((S, tc), lambda j: (0, j)), pl.BlockSpec((S, tc), lambda j: (0, j)),
                  pl.BlockSpec((S, tc), lambda j: (0, nct + j)), pl.BlockSpec((CONV_WIDTH, tc), lambda j: (0, j))],
        out_specs=[pl.BlockSpec((S, tc), lambda j: (0, j)), pl.BlockSpec((S, tc), lambda j: (0, j)),
                   pl.BlockSpec((32, tc), lambda j: (0, j))],
        out_shape=[jax.ShapeDtypeStruct((S, D), BF16), jax.ShapeDtypeStruct((S, D), BF16),
                   jax.ShapeDtypeStruct((32, D), F32)],
        scratch_shapes=[pltpu.VMEM((S + 2 * CONV_HALO, tc), F32), pltpu.VMEM((S + 2 * CONV_HALO, tc), F32),
                        pltpu.VMEM((32, 8, tc), F32)],
        compiler_params=_params("parallel"),
    )(duc, pglu, pglu, wdw)


def exchange(name, ops):
    n = len(ops)

    def body(*refs):
        xs, outs = refs[:n], refs[n:2 * n]
        send, recv, lsem = refs[2 * n:]
        mx, my, mc = lax.axis_index("x"), lax.axis_index("y"), lax.axis_index("c")
        me = 4 * mx + 2 * my + mc
        local = [pltpu.make_async_copy(op[3](xs[i], me), op[4](outs[i], me), lsem.at[i]) for i, op in enumerate(ops)]
        for cp in local:
            cp.start()
        copies = []
        for k in range(1, N_DEV):
            px = 1 - mx if k & 4 else mx
            py = 1 - my if k & 2 else my
            pc = 1 - mc if k & 1 else mc
            peer = 4 * px + 2 * py + pc
            for i, op in enumerate(ops):
                cp = pltpu.make_async_remote_copy(
                    src_ref=op[3](xs[i], peer), dst_ref=op[4](outs[i], me),
                    send_sem=send.at[i, k - 1], recv_sem=recv.at[i, k - 1],
                    device_id=(px, py, pc), device_id_type=pl.DeviceIdType.MESH)
                cp.start()
                copies.append(cp)
        for cp in copies:
            cp.wait()
        for cp in local:
            cp.wait()

    any_spec = pl.BlockSpec(memory_space=pl.ANY)
    return pl.pallas_call(
        body, name=name,
        in_specs=[any_spec] * n, out_specs=[any_spec] * n,
        out_shape=[jax.ShapeDtypeStruct(op[1], op[2]) for op in ops],
        scratch_shapes=[pltpu.SemaphoreType.DMA((n, N_DEV - 1)), pltpu.SemaphoreType.DMA((n, N_DEV - 1)),
                        pltpu.SemaphoreType.DMA((n,))],
        compiler_params=pltpu.CompilerParams(has_side_effects=True),
    )(*[op[0] for op in ops])


def _whole(ref, q):
    return ref


def _slot(ref, q):
    return ref.at[q]


def op_gather_stack(x):
    return (x, (N_DEV,) + x.shape, x.dtype, _whole, _slot)


def op_gather_axis(x, axis):
    size = x.shape[axis]
    shape = x.shape[:axis] + (N_DEV * size,) + x.shape[axis + 1:]

    def dst(ref, q):
        idx = [slice(None)] * len(shape)
        idx[axis] = pl.ds(pl.multiple_of(q * size, size), size)
        return ref.at[tuple(idx)]
    return (x, shape, x.dtype, _whole, dst)


def op_scatter_axis(x, axis):
    size = x.shape[axis] // N_DEV
    shape = x.shape[:axis] + (size,) + x.shape[axis + 1:]

    def src(ref, q):
        idx = [slice(None)] * len(shape)
        idx[axis] = pl.ds(pl.multiple_of(q * size, size), size)
        return ref.at[tuple(idx)]
    return (x, (N_DEV,) + shape, x.dtype, src, _slot)


def _adamw(w, g, m, v):
    m = ADAM_B1 * m + (1.0 - ADAM_B1) * g
    v = ADAM_B2 * v + (1.0 - ADAM_B2) * (g * g)
    m_hat = m / (1.0 - ADAM_B1 ** ADAM_STEP)
    v_hat = v / (1.0 - ADAM_B2 ** ADAM_STEP)
    delta = -ADAM_LR * (m_hat / (jnp.sqrt(v_hat) + ADAM_EPS) + ADAM_WD * w)
    return delta, m, v


def adamw_update(name, w, m, v, *, landed=None, grad=None):
    R, W = w.shape
    n_g = N_DEV if landed is not None else 1

    def fn(rows, vecs, c, i):
        g = rows[0].astype(F32)
        for q in range(1, n_g):
            g = g + rows[q].astype(F32)
        wv, mv, vv = rows[n_g:]
        delta, m2, v2 = _adamw(wv, g, mv, vv)
        return [[g], [delta], [m2], [v2]], []

    g_ins = [_ri(landed, lead=q) for q in range(N_DEV)] if landed is not None else [_ri(grad)]
    cw = LANE if W % LANE == 0 else W
    return rowwise(name, fn, R, g_ins + [_ri(w), _ri(m), _ri(v)], [], [(F32, 1)] * 4, W=W, tr=128, cw=cw)


def kernel(x, c, w_ada, b_ada, w_in, sink, w_dw, conv_ln_g, conv_ln_b, w_oa, w_ob, w_out, ln1_g, ln1_b, w_gu, w_down, ln2_g, ln2_b, loss_target, m_w_ada, m_b_ada, m_w_in, m_sink, m_w_dw, m_conv_ln_g, m_conv_ln_b, m_w_oa, m_w_ob, m_w_out, m_ln1_g, m_ln1_b, m_w_gu, m_w_down, m_ln2_g, m_ln2_b, v_w_ada, v_b_ada, v_w_in, v_sink, v_w_dw, v_conv_ln_g, v_conv_ln_b, v_w_oa, v_w_ob, v_w_out, v_ln1_g, v_ln1_b, v_w_gu, v_w_down, v_ln2_g, v_ln2_b):
    L = w_ada.shape[0]
    S, D = x.shape[1], x.shape[2]
    Dkv = D // GQA_GROUP
    Dqkv = D + 2 * Dkv
    DFF = w_down.shape[1] * N_DEV
    nq, nkv, nb = D // HEAD_DIM, Dkv // HEAD_DIM, S // BLOCK
    alpha = (2.0 * L) ** 0.25
    me = 4 * lax.axis_index("x") + 2 * lax.axis_index("y") + lax.axis_index("c")
    x0 = x.reshape(S, D)
    target = loss_target.reshape(S, D)
    t_cos, t_lo, t_hi = rope_tables(S)

    c_act = jax.nn.silu(c)
    c_all = exchange("gather_c", [op_gather_stack(c_act)])[0].reshape(N_DEV, D)
    c_pad = jnp.concatenate([c_all, jnp.zeros_like(c_all)], axis=0).astype(BF16)
    ada_cols = w_ada.shape[2]
    mod_part = jnp.stack([matmul("mod_mm", c_pad, w_ada[l], "nn", F32)[:N_DEV] for l in range(L)], axis=1)
    mod_land = exchange("scatter_mod", [op_scatter_axis(mod_part, 0)])[0]
    mod = jnp.transpose(mod_land.reshape(N_DEV, L, ada_cols), (1, 0, 2)).reshape(L, N_MOD * D) + b_ada
    mods = [[mod[l:l + 1, j * D:(j + 1) * D] for j in range(N_MOD)] for l in range(L)]

    big = exchange("gather_weights", [
        op_gather_axis(w_in.astype(BF16), 2), op_gather_axis(w_gu.astype(BF16), 2),
        op_gather_axis(w_oa.astype(BF16), 1), op_gather_axis(w_ob.astype(BF16), 1),
        op_gather_axis(w_out.astype(BF16), 1), op_gather_axis(w_down.astype(BF16), 1),
        op_gather_axis(w_dw, 2)])
    W_in, W_gu, W_oa, W_ob, W_out, W_down, W_dw = big

    def vec(a, l):
        return a[l:l + 1]

    def f_mod(rows, vecs, c_, i_):
        (xv,), (sc, sh) = rows, vecs
        return [[xv * (1.0 + sc) + sh]], []

    def f_rope(rows, vecs, c_, i_):
        xv, tc_, tl, th = rows
        return [[_rope(xv, tc_, tl, th, 1.0) if c_ < nq + nkv else xv]], []

    def f_convln(rows, vecs, c_, i_):
        (uc,), (g, b) = rows, vecs
        xhat, _ = _ln_stats(uc)
        nrm = xhat * g + b
        return [[nrm * _sig(nrm)]], []

    def f_merge(rows, vecs, c_, i_):
        g_a, g_b, y_a, y_b = rows
        return [[_sig(g_a) * y_a + _sig(g_b) * y_b]], []

    def f_ln(rows, vecs, c_, i_):
        (xv, r), (gt, g, b, sc, sh) = rows, vecs
        xhat, _ = _ln_stats(alpha * xv + (1.0 + gt) * r)
        y = xhat * g + b
        return [[y], [y * (1.0 + sc) + sh]], []

    def f_swiglu(rows, vecs, c_, i_):
        gate, up = rows
        return [[gate * _sig(gate) * up]], []

    saved = []
    xl = x0
    h = rowwise("modulate", f_mod, S, [_ri(x0)], [mods[0][1], mods[0][0]], [(BF16, 1)], W=D)[0]
    for l in range(L):
        sh_a, sc_a, gt_a, sh_f, sc_f, gt_f = mods[l]
        p_qkv = matmul("in_qkv", h, W_in[l], "nn", F32, n=Dqkv)
        p_glu = matmul("in_glu", h, W_in[l], "nn", F32, b_off=Dqkv, n=2 * D)
        p_gate = matmul("in_gate", h, W_in[l], "nn", F32, b_off=Dqkv + 2 * D, n=2 * D)
        qkv = rowwise("rope", f_rope, S, [_ri(p_qkv), _ri(t_cos, whole=True), _ri(t_lo, whole=True), _ri(t_hi, whole=True)],
                      [], [(BF16, 1)], W=Dqkv, cw=HEAD_DIM)[0]
        sinkcol = jnp.repeat(sink[l].reshape(nkv, GQA_GROUP), BLOCK, axis=1).reshape(nkv, GQA_GROUP * BLOCK, 1)
        att = attn_fwd(qkv, sinkcol, S, D, Dkv)
        y_a = matmul("oa", att, W_oa[l], "nn", F32)
        uc = conv_fwd(p_glu, W_dw[l], S, D)
        z = rowwise("conv_ln", f_convln, S, [_ri(uc)], [vec(conv_ln_g, l), vec(conv_ln_b, l)], [(BF16, 1)], W=D)[0]
        y_b = matmul("ob", z, W_ob[l], "nn", F32)
        merged = rowwise("merge", f_merge, S, [_ri(p_gate, col=0), _ri(p_gate, col=1), _ri(y_a), _ri(y_b)], [],
                         [(BF16, 1)], W=D, cw=_pick(D, 512))[0]
        r1 = matmul("out", merged, W_out[l], "nn", F32)
        x1, h2 = rowwise("ln1", f_ln, S, [_ri(xl), _ri(r1)], [gt_a, vec(ln1_g, l), vec(ln1_b, l), sc_f, sh_f],
                         [(F32, 1), (BF16, 1)], W=D)
        gu = matmul("gu", h2, W_gu[l], "nn", F32)
        act = rowwise("swiglu", f_swiglu, S, [_ri(gu, col=0), _ri(gu, col=1)], [], [(BF16, 1)], W=DFF, tr=128,
                      cw=_pick(DFF, 512))[0]
        f = matmul("down", act, W_down[l], "nn", F32)
        nsc, nsh = (mods[l + 1][1], mods[l + 1][0]) if l + 1 < L else (sc_a, sh_a)
        x2, h_next = rowwise("ln2", f_ln, S, [_ri(x1), _ri(f)], [gt_f, vec(ln2_g, l), vec(ln2_b, l), nsc, nsh],
                             [(F32, 1), (BF16, 1)], W=D)
        saved.append(dict(x=xl, h=h, qkv=qkv, sinkcol=sinkcol, att=att, p_glu=p_glu, p_gate=p_gate, y_a=y_a, y_b=y_b,
                          uc=uc, z=z, merged=merged, r1=r1, x1=x1, h2=h2, gu=gu, act=act, f=f))
        xl, h = x2, h_next

    def f_loss(rows, vecs, c_, i_):
        y, t = rows
        e = y - t
        return [[e * (1.0 / D)]], [e * e]

    dy, err = rowwise("loss", f_loss, S, [_ri(xl), _ri(target)], [], [(F32, 1)], 1, W=D)
    loss = lax.psum(0.5 * jnp.sum(err) / D, AXES)

    def f_ln_bwd_last(rows, vecs, c_, i_):
        (dout, xin, r), (gt, g) = rows, vecs
        xhat, rstd = _ln_stats(alpha * xin + (1.0 + gt) * r)
        dt = _ln_bwd(dout * g, xhat, rstd)
        return [[(1.0 + gt) * dt], [alpha * dt]], [dout * xhat, dout, dt * r]

    def f_ln_bwd(rows, vecs, c_, i_):
        (dres, dh, xout, xin, r), (sc, gt, g) = rows, vecs
        dout = dres + dh * (1.0 + sc)
        xhat, rstd = _ln_stats(alpha * xin + (1.0 + gt) * r)
        dt = _ln_bwd(dout * g, xhat, rstd)
        return [[(1.0 + gt) * dt], [alpha * dt]], [dout * xhat, dout, dt * r, dh * xout, dh]

    def f_swiglu_bwd(rows, vecs, c_, i_):
        da, gate, up = rows
        sg = _sig(gate)
        return [[da * up * sg * (1.0 + gate * (1.0 - sg)), da * (gate * sg)]], []

    def f_gate_bwd(rows, vecs, c_, i_):
        dm, g_a, g_b, y_a, y_b = rows
        sa, sb = _sig(g_a), _sig(g_b)
        return [[dm * sa], [dm * sb], [dm * y_a * sa * (1.0 - sa), dm * y_b * sb * (1.0 - sb)]], []

    def f_convln_bwd(rows, vecs, c_, i_):
        (dz, uc), (g, b) = rows, vecs
        xhat, rstd = _ln_stats(uc)
        nrm = xhat * g + b
        sg = _sig(nrm)
        dn = dz * sg * (1.0 + nrm * (1.0 - sg))
        return [[_ln_bwd(dn * g, xhat, rstd)]], [dn * xhat, dn]

    def f_unrope_q(rows, vecs, c_, i_):
        dq, tc_, tl, th = rows
        return [[_rope(dq, tc_, tl, th, -1.0)]], []

    def f_kv_combine(rows, vecs, c_, i_):
        k_lo, k_mid, k_hi, v_lo, v_mid, v_hi, tc_, tl, th = rows
        lo, hi = i_ > 0, i_ < nb - 1
        dk = jnp.where(lo, k_lo, 0.0) + k_mid + jnp.where(hi, k_hi, 0.0)
        dv = jnp.where(lo, v_lo, 0.0) + v_mid + jnp.where(hi, v_hi, 0.0)
        return [[_rope(dk, tc_, tl, th, -1.0)], [dv]], []

    def f_mod_bwd(rows, vecs, c_, i_):
        (dres, dh, xv), (sc,) = rows, vecs
        return [[dres + dh * (1.0 + sc)]], [dh * xv, dh]

    def flat(a):
        return a.reshape(-1, a.shape[-1])

    def per_layer(a, l):
        rows = a.shape[1]
        return flat(a)[l * rows:(l + 1) * rows]

    big_names = ("w_in", "w_gu", "w_oa", "w_ob", "w_out", "w_down")
    big_w = dict(w_in=w_in, w_gu=w_gu, w_oa=w_oa, w_ob=w_ob, w_out=w_out, w_down=w_down)
    big_m = dict(w_in=m_w_in, w_gu=m_w_gu, w_oa=m_w_oa, w_ob=m_w_ob, w_out=m_w_out, w_down=m_w_down)
    big_v = dict(w_in=v_w_in, w_gu=v_w_gu, w_oa=v_w_oa, w_ob=v_w_ob, w_out=v_w_out, w_down=v_w_down)
    big_axis = dict(w_in=1, w_gu=1, w_oa=0, w_ob=0, w_out=0, w_down=0)
    big_res = {nm: [None] * L for nm in big_names}
    dmod = [None] * L
    small = dict(sink=[None] * L, conv_ln_g=[None] * L, conv_ln_b=[None] * L, ln1_g=[None] * L, ln1_b=[None] * L,
                 ln2_g=[None] * L, ln2_b=[None] * L)
    dwdw = [None] * L

    dres, dh_next = dy, None
    for l in reversed(range(L)):
        sv = saved[l]
        sh_a, sc_a, gt_a, sh_f, sc_f, gt_f = mods[l]
        if dh_next is None:
            df, dres, d_g2, d_b2, d_gtf = rowwise(
                "ln2_bwd_last", f_ln_bwd_last, S, [_ri(dres), _ri(sv["x1"]), _ri(sv["f"])], [gt_f, vec(ln2_g, l)],
                [(BF16, 1), (F32, 1)], 3, W=D)
            d_sca_next = d_sha_next = None
        else:
            df, dres, d_g2, d_b2, d_gtf, d_sca_next, d_sha_next = rowwise(
                "ln2_bwd", f_ln_bwd, S, [_ri(dres), _ri(dh_next), _ri(saved[l + 1]["x"]), _ri(sv["x1"]), _ri(sv["f"])],
                [mods[l + 1][1], gt_f, vec(ln2_g, l)], [(BF16, 1), (F32, 1)], 5, W=D)
            dmod[l + 1][1], dmod[l + 1][0] = d_sca_next, d_sha_next
        dmod[l] = [None] * N_MOD
        dmod[l][5] = d_gtf
        small["ln2_g"][l], small["ln2_b"][l] = d_g2, d_b2
        g_down = matmul("d_w_down", sv["act"], df, "tn", BF16)
        dact = matmul("d_act", df, W_down[l], "nt", F32)
        dgu = rowwise("swiglu_bwd", f_swiglu_bwd, S, [_ri(dact), _ri(sv["gu"], col=0), _ri(sv["gu"], col=1)], [],
                      [(BF16, 2)], W=DFF, tr=128, cw=_pick(DFF, 512))[0]
        g_gu = matmul("d_w_gu", sv["h2"], dgu, "tn", BF16)
        dh2 = matmul("d_h2", dgu, W_gu[l], "nt", F32)
        dr1, dres, d_g1, d_b1, d_gta, d_scf, d_shf = rowwise(
            "ln1_bwd", f_ln_bwd, S, [_ri(dres), _ri(dh2), _ri(sv["x1"]), _ri(sv["x"]), _ri(sv["r1"])],
            [sc_f, gt_a, vec(ln1_g, l)], [(BF16, 1), (F32, 1)], 5, W=D)
        dmod[l][2], dmod[l][4], dmod[l][3] = d_gta, d_scf, d_shf
        small["ln1_g"][l], small["ln1_b"][l] = d_g1, d_b1
        g_out = matmul("d_w_out", sv["merged"], dr1, "tn", BF16)
        dmerged = matmul("d_merged", dr1, W_out[l], "nt", F32)
        dy_a, dy_b, dp_gate = rowwise(
            "gate_bwd", f_gate_bwd, S,
            [_ri(dmerged), _ri(sv["p_gate"], col=0), _ri(sv["p_gate"], col=1), _ri(sv["y_a"]), _ri(sv["y_b"])], [],
            [(BF16, 1), (BF16, 1), (BF16, 2)], W=D, cw=_pick(D, 512))
        g_oa = matmul("d_w_oa", sv["att"], dy_a, "tn", BF16)
        datt = matmul("d_att", dy_a, W_oa[l], "nt", BF16)
        g_ob = matmul("d_w_ob", sv["z"], dy_b, "tn", BF16)
        dz = matmul("d_z", dy_b, W_ob[l], "nt", F32)
        duc, d_cg, d_cb = rowwise("conv_ln_bwd", f_convln_bwd, S, [_ri(dz), _ri(sv["uc"])],
                                  [vec(conv_ln_g, l), vec(conv_ln_b, l)], [(F32, 1)], 2, W=D)
        small["conv_ln_g"][l], small["conv_ln_b"][l] = d_cg, d_cb
        dga, dgb, dwdw[l] = conv_bwd(duc, sv["p_glu"], W_dw[l], S, D)
        dq, dkp, dvp, dsink = attn_bwd(sv["qkv"], sv["sinkcol"], datt, S, D, Dkv)
        small["sink"][l] = dsink[:, :GQA_GROUP, 0].reshape(1, nq)
        dq_r = rowwise("unrope_q", f_unrope_q, S, [_ri(dq), _ri(t_cos, whole=True), _ri(t_lo, whole=True), _ri(t_hi, whole=True)],
                       [], [(BF16, 1)], W=D, cw=HEAD_DIM)[0]
        dk_r, dv_r = rowwise(
            "kv_combine", f_kv_combine, S,
            [_ri(dkp, lead=2, shift=-1), _ri(dkp, lead=1), _ri(dkp, lead=0, shift=1),
             _ri(dvp, lead=2, shift=-1), _ri(dvp, lead=1), _ri(dvp, lead=0, shift=1),
             _ri(t_cos, whole=True), _ri(t_lo, whole=True), _ri(t_hi, whole=True)],
            [], [(BF16, 1), (BF16, 1)], W=Dkv, tr=BLOCK, cw=HEAD_DIM)
        dp = jnp.concatenate([dq_r, dk_r, dv_r, dga, dgb, dp_gate], axis=1)
        g_in = matmul("d_w_in", sv["h"], dp, "tn", BF16)
        dh_next = matmul("d_h", dp, W_in[l], "nt", F32)

        grads = dict(w_in=g_in, w_gu=g_gu, w_oa=g_oa, w_ob=g_ob, w_out=g_out, w_down=g_down)
        landed = exchange("scatter_grads", [op_scatter_axis(grads[nm], big_axis[nm]) for nm in big_names])
        for nm, land in zip(big_names, landed):
            big_res[nm][l] = adamw_update("adamw_" + nm, per_layer(big_w[nm], l), per_layer(big_m[nm], l),
                                          per_layer(big_v[nm], l), landed=land)

    grad_x, d_sca0, d_sha0 = rowwise("mod_bwd", f_mod_bwd, S, [_ri(dres), _ri(dh_next), _ri(x0)], [mods[0][1]],
                                     [(F32, 1)], 2, W=D)
    dmod[0][1], dmod[0][0] = d_sca0, d_sha0

    small_names = ("b_ada", "sink", "conv_ln_g", "conv_ln_b", "ln1_g", "ln1_b", "ln2_g", "ln2_b")
    small_w = dict(b_ada=b_ada, sink=sink, conv_ln_g=conv_ln_g, conv_ln_b=conv_ln_b, ln1_g=ln1_g, ln1_b=ln1_b,
                   ln2_g=ln2_g, ln2_b=ln2_b)
    small_m = dict(b_ada=m_b_ada, sink=m_sink, conv_ln_g=m_conv_ln_g, conv_ln_b=m_conv_ln_b, ln1_g=m_ln1_g,
                   ln1_b=m_ln1_b, ln2_g=m_ln2_g, ln2_b=m_ln2_b)
    small_v = dict(b_ada=v_b_ada, sink=v_sink, conv_ln_g=v_conv_ln_g, conv_ln_b=v_conv_ln_b, ln1_g=v_ln1_g,
                   ln1_b=v_ln1_b, ln2_g=v_ln2_g, ln2_b=v_ln2_b)
    small_g = dict(small)
    small_g["b_ada"] = [jnp.concatenate(dmod[l], axis=1) for l in range(L)]
    sizes = [small_w[nm].size for nm in small_names]
    total = sum(sizes)
    padded = -(-total // (SMALL_W * ROW_CHUNK)) * (SMALL_W * ROW_CHUNK)

    def pack(parts):
        flat_ = jnp.concatenate([p.reshape(-1) for p in parts] + [jnp.zeros((padded - total,), F32)])
        return flat_.reshape(padded // SMALL_W, SMALL_W)

    g_pack = pack([jnp.concatenate(small_g[nm], axis=0) for nm in small_names])
    small_land, dwdw_land = exchange("gather_small", [op_gather_stack(g_pack),
                                                      op_scatter_axis(jnp.stack(dwdw, axis=0), 2)])
    small_out = adamw_update("adamw_small", pack([small_w[nm] for nm in small_names]),
                             pack([small_m[nm] for nm in small_names]), pack([small_v[nm] for nm in small_names]),
                             landed=small_land)

    def unpack(buf):
        flat_, out, o = buf.reshape(-1), {}, 0
        for nm, sz in zip(small_names, sizes):
            out[nm] = flat_[o:o + sz].reshape(small_w[nm].shape)
            o += sz
        return out
    small_res = [unpack(b) for b in small_out]

    dw_cols = w_dw.shape[2]

    def pad_dw(a):
        return jnp.pad(a, ((0, 0), (0, 32 - CONV_WIDTH), (0, 0))).reshape(L * 32, dw_cols)
    dw_out = adamw_update("adamw_w_dw", pad_dw(w_dw), pad_dw(m_w_dw), pad_dw(v_w_dw),
                          landed=dwdw_land.reshape(N_DEV, L * 32, dw_cols))
    dw_res = [a.reshape(L, 32, dw_cols)[:, :CONV_WIDTH] for a in dw_out]

    dmod_all = small_land.reshape(N_DEV, -1)[:, :L * N_MOD * D].reshape(N_DEV, L, N_MOD * D)
    dmod_mine = lax.dynamic_slice_in_dim(dmod_all, me * ada_cols, ada_cols, axis=2)
    dmod_pad = jnp.concatenate([dmod_mine, jnp.zeros_like(dmod_mine)], axis=0).astype(BF16)
    g_ada = jnp.stack([matmul("d_w_ada", c_pad, dmod_pad[:, l], "tn", F32) for l in range(L)], axis=0)
    ada_out = adamw_update("adamw_w_ada", flat(w_ada), flat(m_w_ada), flat(v_w_ada), grad=flat(g_ada))
    ada_res = [a.reshape(w_ada.shape) for a in ada_out]

    order = ("w_ada", "b_ada", "w_in", "sink", "w_dw", "conv_ln_g", "conv_ln_b", "w_oa", "w_ob", "w_out",
             "ln1_g", "ln1_b", "w_gu", "w_down", "ln2_g", "ln2_b")

    def result(nm, j):
        if nm == "w_ada":
            return ada_res[j]
        if nm == "w_dw":
            return dw_res[j]
        if nm in big_res:
            return jnp.stack([big_res[nm][l][j] for l in range(L)], axis=0)
        return small_res[j][nm]

    outs = [loss, grad_x.reshape(x.shape)]
    for j in range(4):
        outs += [result(nm, j) for nm in order]
    return tuple(outs)
```

```python
import math

import jax
import jax.numpy as jnp
from jax import lax
from jax.experimental import pallas as pl
from jax.experimental.pallas import tpu as pltpu

F32 = jnp.float32
BF16 = jnp.bfloat16
N_DEV = 8
AXES = ("x", "y", "c")
HEAD_DIM = 128
GQA_GROUP = 4
BLOCK = 128
ROPE_DIM = HEAD_DIM // 4
ROPE_HALF = ROPE_DIM // 2
ROPE_THETA = 500000.0
CONV_WIDTH = 31
CONV_HALO = 16
N_MOD = 6
LN_EPS = 1e-5
NEG_INF = -1e30
ADAM_LR, ADAM_B1, ADAM_B2, ADAM_EPS, ADAM_WD, ADAM_STEP = 0.001, 0.9, 0.999, 1e-08, 0.01, 10
VMEM_LIMIT = 56 * 1024 * 1024
LANE = 128
ROW_CHUNK = 16
SMALL_W = 512


def _params(*sem):
    return pltpu.CompilerParams(dimension_semantics=sem, vmem_limit_bytes=VMEM_LIMIT)


def _pick(dim, pref, mult=LANE):
    if dim <= pref:
        return dim
    best = None
    for t in range(mult, pref + 1, mult):
        if dim % t == 0:
            best = t
    assert best is not None, (dim, pref)
    return best


def matmul(name, a, b, mode, out_dtype, *, b_off=0, n=None, tm=1024, tn=1408, tk=2816, deps=()):
    if mode == "nn":
        (M, K), N = a.shape, (n or b.shape[1])
    elif mode == "tn":
        (K, M), N = a.shape, b.shape[1]
    else:
        (M, K), N = a.shape, b.shape[0]
    tm = _pick(M, tn, LANE) if mode == "tn" else _pick(M, tm, 16)
    tn, tk = _pick(math.gcd(N, b_off) if b_off else N, tn), _pick(K, tk)
    assert b_off % tn == 0 and N % tn == 0
    boff = b_off // tn
    nk = K // tk
    if mode == "nn":
        a_spec = pl.BlockSpec((tm, tk), lambda i, j, k: (i, k))
        b_spec = pl.BlockSpec((tk, tn), lambda i, j, k: (k, j + boff))
        dims = (((1,), (0,)), ((), ()))
    elif mode == "tn":
        a_spec = pl.BlockSpec((tk, tm), lambda i, j, k: (k, i))
        b_spec = pl.BlockSpec((tk, tn), lambda i, j, k: (k, j))
        dims = (((0,), (0,)), ((), ()))
    else:
        a_spec = pl.BlockSpec((tm, tk), lambda i, j, k: (i, k))
        b_spec = pl.BlockSpec((tn, tk), lambda i, j, k: (j, k))
        dims = (((1,), (1,)), ((), ()))
    n_dep = len(deps)

    def body(a_ref, b_ref, *rest):
        o_ref = rest[n_dep]

        def dot():
            return lax.dot_general(a_ref[...].astype(BF16), b_ref[...].astype(BF16), dims, preferred_element_type=F32)
        if nk == 1:
            o_ref[...] = dot().astype(out_dtype)
        else:
            acc_ref = rest[n_dep + 1]
            k = pl.program_id(2)

            @pl.when(k == 0)
            def _():
                acc_ref[...] = jnp.zeros_like(acc_ref)
            acc_ref[...] += dot()

            @pl.when(k == nk - 1)
            def _():
                o_ref[...] = acc_ref[...].astype(out_dtype)

    return pl.pallas_call(
        body, name=name, grid=(M // tm, N // tn, nk),
        in_specs=[a_spec, b_spec] + [pl.BlockSpec(memory_space=pl.ANY)] * n_dep,
        out_specs=pl.BlockSpec((tm, tn), lambda i, j, k: (i, j)),
        out_shape=jax.ShapeDtypeStruct((M, N), out_dtype),
        scratch_shapes=[pltpu.VMEM((tm, tn), F32)] if nk > 1 else [],
        compiler_params=_params("parallel", "parallel", "arbitrary"),
    )(a, b, *deps)


def rowwise(name, fn, nrows, row_ins, vec_ins, row_outs, n_sums=0, *, W, tr=256, cw=None, rc=ROW_CHUNK):
    cw = cw or W
    tr = _pick(nrows, tr, ROW_CHUNK)
    rc = min(rc, tr)
    assert W % cw == 0 and nrows % ROW_CHUNK == 0 and tr % rc == 0 and rc % 8 == 0
    nrt = nrows // tr
    n_ri, n_v, n_ro = len(row_ins), len(vec_ins), len(row_outs)

    def row_spec(lead, colblk, shift, whole):
        w = cw if whole else W

        def rmap(i):
            return jnp.clip(i + shift, 0, nrt - 1) if shift else i
        if lead is None:
            return pl.BlockSpec((tr, w), lambda i: (rmap(i), colblk))
        return pl.BlockSpec((None, tr, w), lambda i: (lead, rmap(i), colblk))

    in_specs = [row_spec(*ri[1:]) for ri in row_ins]
    in_specs += [pl.BlockSpec((1, W), lambda i: (0, 0)) for _ in vec_ins]
    out_specs = [pl.BlockSpec((tr, p * W), lambda i: (i, 0)) for _, p in row_outs]
    out_specs += [pl.BlockSpec((1, W), lambda i: (0, 0)) for _ in range(n_sums)]
    out_shape = [jax.ShapeDtypeStruct((nrows, p * W), dt) for dt, p in row_outs]
    out_shape += [jax.ShapeDtypeStruct((1, W), F32) for _ in range(n_sums)]

    def body(*refs):
        rin, vin = refs[:n_ri], refs[n_ri:n_ri + n_v]
        rout = refs[n_ri + n_v:n_ri + n_v + n_ro]
        sout = refs[n_ri + n_v + n_ro:n_ri + n_v + n_ro + n_sums]
        acc = refs[n_ri + n_v + n_ro + n_sums:]
        i = pl.program_id(0)
        if n_sums:
            @pl.when(i == 0)
            def _():
                for a in acc:
                    a[...] = jnp.zeros_like(a)
        for c in range(W // cw):
            c0 = c * cw
            vecs = [v[:, c0:c0 + cw] for v in vin]

            def step(r, carry, c=c, c0=c0, vecs=vecs):
                r0 = pl.multiple_of(r * rc, rc)
                rows = [ref[pl.ds(r0, rc), :] if ri[4] else ref[pl.ds(r0, rc), c0:c0 + cw]
                        for ref, ri in zip(rin, row_ins)]
                outs, sums = fn(rows, vecs, c, i)
                for oref, (dt, _), pieces in zip(rout, row_outs, outs):
                    for pi, piece in enumerate(pieces):
                        oref[pl.ds(r0, rc), pi * W + c0:pi * W + c0 + cw] = piece.astype(dt)
                for a, s in zip(acc, sums):
                    part = s[0:8]
                    for q in range(1, rc // 8):
                        part = part + s[8 * q:8 * q + 8]
                    a[:, c0:c0 + cw] += part
                return carry
            lax.fori_loop(0, tr // rc, step, 0)
        if n_sums:
            @pl.when(i == nrt - 1)
            def _():
                for o, a in zip(sout, acc):
                    o[...] = jnp.sum(a[...], axis=0, keepdims=True)

    res = pl.pallas_call(
        body, name=name, grid=(nrt,), in_specs=in_specs, out_specs=out_specs, out_shape=out_shape,
        scratch_shapes=[pltpu.VMEM((8, W), F32) for _ in range(n_sums)],
        compiler_params=_params("arbitrary"),
    )(*[ri[0] for ri in row_ins], *vec_ins)
    return res


def _ri(arr, lead=None, col=0, shift=0, whole=False):
    return (arr, lead, col, shift, whole)


def _sig(x):
    return jax.nn.sigmoid(x)


def _ln_stats(t):
    mu = jnp.mean(t, axis=-1, keepdims=True)
    xc = t - mu
    var = jnp.mean(xc * xc, axis=-1, keepdims=True)
    rstd = lax.rsqrt(var + LN_EPS)
    return xc * rstd, rstd


def _ln_bwd(dy_g, xhat, rstd):
    m1 = jnp.mean(dy_g, axis=-1, keepdims=True)
    m2 = jnp.mean(dy_g * xhat, axis=-1, keepdims=True)
    return rstd * (dy_g - m1 - xhat * m2)


def _rope(x, cos, s_lo, s_hi, sign):
    up = pltpu.roll(x, HEAD_DIM - ROPE_HALF, 1)
    down = pltpu.roll(x, ROPE_HALF, 1)
    return x * cos + sign * (up * s_lo + down * s_hi)


def rope_tables(S):
    pos = jnp.arange(S, dtype=F32)
    inv_freq = ROPE_THETA ** (-jnp.arange(0, ROPE_DIM, 2, dtype=F32) / ROPE_DIM)
    ang = pos[:, None] * inv_freq[None, :]
    cos, sin = jnp.cos(ang), jnp.sin(ang)
    ones = jnp.ones((S, HEAD_DIM - ROPE_DIM), F32)
    zeros = jnp.zeros((S, HEAD_DIM - ROPE_DIM), F32)
    zh = jnp.zeros((S, ROPE_HALF), F32)
    t_cos = jnp.concatenate([cos, cos, ones], axis=1)
    t_lo = jnp.concatenate([-sin, zh, zeros], axis=1)
    t_hi = jnp.concatenate([zh, sin, zeros], axis=1)
    return t_cos, t_lo, t_hi


def _attn_specs(S, D, Dkv):
    nb, nkv, qb = S // BLOCK, Dkv // HEAD_DIM, D // HEAD_DIM
    gw = GQA_GROUP * HEAD_DIM
    q_spec = pl.BlockSpec((BLOCK, gw), lambda h, n: (n, h))

    def band(col0):
        return [pl.BlockSpec((BLOCK, HEAD_DIM), lambda h, n: (jnp.maximum(n - 1, 0), col0 + h)),
                pl.BlockSpec((BLOCK, HEAD_DIM), lambda h, n: (n, col0 + h)),
                pl.BlockSpec((BLOCK, HEAD_DIM), lambda h, n: (jnp.minimum(n + 1, nb - 1), col0 + h))]
    sink_spec = pl.BlockSpec((None, GQA_GROUP * BLOCK, 1), lambda h, n: (h, 0, 0))
    return nb, nkv, gw, q_spec, band(qb), band(qb + nkv), sink_spec


def _attn_probs(q_ref, k_refs, sink_ref, n, S):
    q = q_ref[...]
    qs = jnp.concatenate([q[:, g * HEAD_DIM:(g + 1) * HEAD_DIM] for g in range(GQA_GROUP)], axis=0)
    kb = jnp.concatenate([r[...] for r in k_refs], axis=0)
    s = lax.dot_general(qs, kb, (((1,), (1,)), ((), ())), preferred_element_type=F32) * (HEAD_DIM ** -0.5)
    shape = (GQA_GROUP * BLOCK, 3 * BLOCK)
    row = lax.broadcasted_iota(jnp.int32, shape, 0) & (BLOCK - 1)
    col = lax.broadcasted_iota(jnp.int32, shape, 1)
    rel = col - BLOCK - row
    kpos = (n - 1) * BLOCK + col
    valid = (jnp.abs(rel) <= BLOCK) & (kpos >= 0) & (kpos < S)
    s = jnp.where(valid, s, NEG_INF)
    sink = sink_ref[...]
    m = jnp.maximum(jnp.max(s, axis=-1, keepdims=True), sink)
    p = jnp.exp(s - m)
    e_sink = jnp.exp(sink - m)
    denom = jnp.sum(p, axis=-1, keepdims=True) + e_sink
    return qs, kb, p / denom, e_sink / denom


def attn_fwd(qkv, sinkcol, S, D, Dkv):
    nb, nkv, gw, q_spec, k_specs, v_specs, sink_spec = _attn_specs(S, D, Dkv)

    def body(q_ref, k0, k1, k2, v0, v1, v2, sink_ref, o_ref):
        n = pl.program_id(1)
        _, _, w, _ = _attn_probs(q_ref, (k0, k1, k2), sink_ref, n, S)
        vb = jnp.concatenate([v0[...], v1[...], v2[...]], axis=0)
        o = jnp.dot(w.astype(BF16), vb, preferred_element_type=F32)
        for g in range(GQA_GROUP):
            o_ref[:, g * HEAD_DIM:(g + 1) * HEAD_DIM] = o[g * BLOCK:(g + 1) * BLOCK].astype(BF16)

    return pl.pallas_call(
        body, name="attn_fwd", grid=(nkv, nb),
        in_specs=[q_spec, *k_specs, *v_specs, sink_spec],
        out_specs=pl.BlockSpec((BLOCK, gw), lambda h, n: (n, h)),
        out_shape=jax.ShapeDtypeStruct((S, D), BF16),
        compiler_params=_params("parallel", "arbitrary"),
    )(qkv, qkv, qkv, qkv, qkv, qkv, qkv, sinkcol)


def attn_bwd(qkv, sinkcol, datt, S, D, Dkv):
    nb, nkv, gw, q_spec, k_specs, v_specs, sink_spec = _attn_specs(S, D, Dkv)

    def body(q_ref, k0, k1, k2, v0, v1, v2, sink_ref, do_ref, dq_ref, dkp_ref, dvp_ref, dsink_ref):
        n = pl.program_id(1)
        qs, kb, w, w_sink = _attn_probs(q_ref, (k0, k1, k2), sink_ref, n, S)
        vb = jnp.concatenate([v0[...], v1[...], v2[...]], axis=0)
        do = do_ref[...]
        dos = jnp.concatenate([do[:, g * HEAD_DIM:(g + 1) * HEAD_DIM] for g in range(GQA_GROUP)], axis=0)
        dv = lax.dot_general(w.astype(BF16), dos, (((0,), (0,)), ((), ())), preferred_element_type=F32)
        dw = lax.dot_general(dos, vb, (((1,), (1,)), ((), ())), preferred_element_type=F32)
        delta = jnp.sum(w * dw, axis=-1, keepdims=True)
        ds = (w * (dw - delta) * (HEAD_DIM ** -0.5)).astype(BF16)
        dq = jnp.dot(ds, kb, preferred_element_type=F32)
        dk = lax.dot_general(ds, qs, (((0,), (0,)), ((), ())), preferred_element_type=F32)
        for g in range(GQA_GROUP):
            dq_ref[:, g * HEAD_DIM:(g + 1) * HEAD_DIM] = dq[g * BLOCK:(g + 1) * BLOCK]
        for j in range(3):
            dkp_ref[j] = dk[j * BLOCK:(j + 1) * BLOCK]
            dvp_ref[j] = dv[j * BLOCK:(j + 1) * BLOCK]

        @pl.when(n == 0)
        def _():
            dsink_ref[...] = jnp.zeros_like(dsink_ref)
        t = w_sink * delta
        for g in range(GQA_GROUP):
            dsink_ref[g:g + 1, :] -= jnp.sum(t[g * BLOCK:(g + 1) * BLOCK], axis=0, keepdims=True)

    part_spec = pl.BlockSpec((3, BLOCK, HEAD_DIM), lambda h, n: (0, n, h))
    return pl.pallas_call(
        body, name="attn_bwd", grid=(nkv, nb),
        in_specs=[q_spec, *k_specs, *v_specs, sink_spec, pl.BlockSpec((BLOCK, gw), lambda h, n: (n, h))],
        out_specs=[pl.BlockSpec((BLOCK, gw), lambda h, n: (n, h)), part_spec, part_spec,
                   pl.BlockSpec((None, 8, LANE), lambda h, n: (h, 0, 0))],
        out_shape=[jax.ShapeDtypeStruct((S, D), F32), jax.ShapeDtypeStruct((3, S, Dkv), F32),
                   jax.ShapeDtypeStruct((3, S, Dkv), F32), jax.ShapeDtypeStruct((nkv, 8, LANE), F32)],
        compiler_params=_params("parallel", "arbitrary"),
    )(qkv, qkv, qkv, qkv, qkv, qkv, qkv, sinkcol, datt)


CONV_TC = 128
CONV_ROWS = 32


def _conv_fill_u(ga_ref, gb_ref, upad_ref, S):
    tc = upad_ref.shape[1]
    zero = jnp.zeros((CONV_HALO, tc), F32)
    upad_ref[0:CONV_HALO, :] = zero
    upad_ref[S + CONV_HALO:S + 2 * CONV_HALO, :] = zero

    def fill(r, carry):
        r0 = pl.multiple_of(r * CONV_ROWS, CONV_ROWS)
        upad_ref[pl.ds(r0 + CONV_HALO, CONV_ROWS), :] = ga_ref[pl.ds(r0, CONV_ROWS), :] * _sig(gb_ref[pl.ds(r0, CONV_ROWS), :])
        return carry
    lax.fori_loop(0, S // CONV_ROWS, fill, 0)


def conv_fwd(pglu, wdw, S, D):
    tc = min(CONV_TC, D)
    nct = D // tc

    def body(ga_ref, gb_ref, w_ref, o_ref, upad_ref):
        _conv_fill_u(ga_ref, gb_ref, upad_ref, S)
        w = w_ref[...]

        def step(r, carry):
            r0 = pl.multiple_of(r * CONV_ROWS, CONV_ROWS)
            win = upad_ref[pl.ds(r0, CONV_ROWS + 2 * CONV_HALO), :]
            acc = jnp.zeros((CONV_ROWS, tc), F32)
            for k in range(CONV_WIDTH):
                o = k + CONV_HALO - CONV_WIDTH // 2
                acc = acc + w[k:k + 1, :] * win[o:o + CONV_ROWS, :]
            o_ref[pl.ds(r0, CONV_ROWS), :] = acc
            return carry
        lax.fori_loop(0, S // CONV_ROWS, step, 0)

    return pl.pallas_call(
        body, name="conv_fwd", grid=(nct,),
        in_specs=[pl.BlockSpec((S, tc), lambda j: (0, j)), pl.BlockSpec((S, tc), lambda j: (0, nct + j)),
                  pl.BlockSpec((CONV_WIDTH, tc), lambda j: (0, j))],
        out_specs=pl.BlockSpec((S, tc), lambda j: (0, j)),
        out_shape=jax.ShapeDtypeStruct((S, D), F32),
        scratch_shapes=[pltpu.VMEM((S + 2 * CONV_HALO, tc), F32)],
        compiler_params=_params("parallel"),
    )(pglu, pglu, wdw)


def conv_bwd(duc, pglu, wdw, S, D):
    tc = min(CONV_TC, D)
    nct = D // tc
    half = CONV_WIDTH // 2

    def body(d_ref, ga_ref, gb_ref, w_ref, dga_ref, dgb_ref, dw_ref, upad_ref, dpad_ref, dwacc_ref):
        _conv_fill_u(ga_ref, gb_ref, upad_ref, S)
        zero = jnp.zeros((CONV_HALO, tc), F32)
        dpad_ref[0:CONV_HALO, :] = zero
        dpad_ref[S + CONV_HALO:S + 2 * CONV_HALO, :] = zero

        def fill(r, carry):
            r0 = pl.multiple_of(r * CONV_ROWS, CONV_ROWS)
            dpad_ref[pl.ds(r0 + CONV_HALO, CONV_ROWS), :] = d_ref[pl.ds(r0, CONV_ROWS), :]
            return carry
        lax.fori_loop(0, S // CONV_ROWS, fill, 0)
        dwacc_ref[...] = jnp.zeros_like(dwacc_ref)
        w = w_ref[...]

        def step(r, carry):
            r0 = pl.multiple_of(r * CONV_ROWS, CONV_ROWS)
            uwin = upad_ref[pl.ds(r0, CONV_ROWS + 2 * CONV_HALO), :]
            dwin = dpad_ref[pl.ds(r0, CONV_ROWS + 2 * CONV_HALO), :]
            d = dwin[CONV_HALO:CONV_HALO + CONV_ROWS, :]
            du = jnp.zeros((CONV_ROWS, tc), F32)
            for k in range(CONV_WIDTH):
                o = CONV_HALO + half - k
                du = du + w[k:k + 1, :] * dwin[o:o + CONV_ROWS, :]
                o = CONV_HALO + k - half
                prod = d * uwin[o:o + CONV_ROWS, :]
                part = prod[0:8]
                for q in range(1, CONV_ROWS // 8):
                    part = part + prod[8 * q:8 * q + 8]
                dwacc_ref[k] += part
            ga = ga_ref[pl.ds(r0, CONV_ROWS), :]
            sg = _sig(gb_ref[pl.ds(r0, CONV_ROWS), :])
            dga_ref[pl.ds(r0, CONV_ROWS), :] = (du * sg).astype(BF16)
            dgb_ref[pl.ds(r0, CONV_ROWS), :] = (du * ga * sg * (1.0 - sg)).astype(BF16)
            return carry
        lax.fori_loop(0, S // CONV_ROWS, step, 0)
        dw_ref[...] = jnp.sum(dwacc_ref[...], axis=1)

    return pl.pallas_call(
        body, name="conv_bwd", grid=(nct,),
        in_specs=[pl.BlockSpec((S, tc), lambda j: (0, j)), pl.BlockSpec((S, tc), lambda j: (0, j)),
                  pl.BlockSpec((S, tc), lambda j: (0, nct + j)), pl.BlockSpec((CONV_WIDTH, tc), lambda j: (0, j))],
        out_specs=[pl.BlockSpec((S, tc), lambda j: (0, j)), pl.BlockSpec((S, tc), lambda j: (0, j)),
                   pl.BlockSpec((32, tc), lambda j: (0, j))],
        out_shape=[jax.ShapeDtypeStruct((S, D), BF16), jax.ShapeDtypeStruct((S, D), BF16),
                   jax.ShapeDtypeStruct((32, D), F32)],
        scratch_shapes=[pltpu.VMEM((S + 2 * CONV_HALO, tc), F32), pltpu.VMEM((S + 2 * CONV_HALO, tc), F32),
                        pltpu.VMEM((32, 8, tc), F32)],
        compiler_params=_params("parallel"),
    )(duc, pglu, pglu, wdw)


def exchange(name, ops):
    n = len(ops)

    def body(*refs):
        xs, outs = refs[:n], refs[n:2 * n]
        send, recv, lsem = refs[2 * n:]
        mx, my, mc = lax.axis_index("x"), lax.axis_index("y"), lax.axis_index("c")
        me = 4 * mx + 2 * my + mc
        local = [pltpu.make_async_copy(op[3](xs[i], me), op[4](outs[i], me), lsem.at[i]) for i, op in enumerate(ops)]
        for cp in local:
            cp.start()
        copies = []
        for k in range(1, N_DEV):
            px = 1 - mx if k & 4 else mx
            py = 1 - my if k & 2 else my
            pc = 1 - mc if k & 1 else mc
            peer = 4 * px + 2 * py + pc
            for i, op in enumerate(ops):
                cp = pltpu.make_async_remote_copy(
                    src_ref=op[3](xs[i], peer), dst_ref=op[4](outs[i], me),
                    send_sem=send.at[i, k - 1], recv_sem=recv.at[i, k - 1],
                    device_id=(px, py, pc), device_id_type=pl.DeviceIdType.MESH)
                cp.start()
                copies.append(cp)
        for cp in copies:
            cp.wait()
        for cp in local:
            cp.wait()

    any_spec = pl.BlockSpec(memory_space=pl.ANY)
    return pl.pallas_call(
        body, name=name,
        in_specs=[any_spec] * n, out_specs=[any_spec] * n,
        out_shape=[jax.ShapeDtypeStruct(op[1], op[2]) for op in ops],
        scratch_shapes=[pltpu.SemaphoreType.DMA((n, N_DEV - 1)), pltpu.SemaphoreType.DMA((n, N_DEV - 1)),
                        pltpu.SemaphoreType.DMA((n,))],
        compiler_params=pltpu.CompilerParams(has_side_effects=True),
    )(*[op[0] for op in ops])


_HBM = pl.BlockSpec(memory_space=pltpu.HBM)
_SEM = pl.BlockSpec(memory_space=pltpu.SEMAPHORE)
_EFFECT = pltpu.SideEffectType.DATAFLOW_SIDE_EFFECTING


def _me_and_peers():
    mx, my, mc = lax.axis_index("x"), lax.axis_index("y"), lax.axis_index("c")
    peers = []
    for k in range(1, N_DEV):
        px = 1 - mx if k & 4 else mx
        py = 1 - my if k & 2 else my
        pc = 1 - mc if k & 1 else mc
        peers.append((k, (px, py, pc), 4 * px + 2 * py + pc))
    return 4 * mx + 2 * my + mc, peers


def exchange_local(name, ops):
    n = len(ops)

    def body(*refs):
        xs, outs, lsem = refs[:n], refs[n:2 * n], refs[2 * n]
        me = 4 * lax.axis_index("x") + 2 * lax.axis_index("y") + lax.axis_index("c")
        local = [pltpu.make_async_copy(op[3](xs[i], me), op[4](outs[i], me), lsem.at[i]) for i, op in enumerate(ops)]
        for cp in local:
            cp.start()
        for cp in local:
            cp.wait()

    any_spec = pl.BlockSpec(memory_space=pl.ANY)
    return pl.pallas_call(
        body, name=name, in_specs=[any_spec] * n, out_specs=[any_spec] * n,
        out_shape=[jax.ShapeDtypeStruct(op[1], op[2]) for op in ops],
        scratch_shapes=[pltpu.SemaphoreType.DMA((n,))],
    )(*[op[0] for op in ops])


def _remote_copies(ops, xs, lands, send, recv):
    me, peers = _me_and_peers()
    return [pltpu.make_async_remote_copy(
        src_ref=op[3](xs[i], peer), dst_ref=op[4](lands[i], me),
        send_sem=send.at[i * (N_DEV - 1) + k - 1], recv_sem=recv.at[i * (N_DEV - 1) + k - 1],
        device_id=peer_id, device_id_type=pl.DeviceIdType.MESH)
        for k, peer_id, peer in peers for i, op in enumerate(ops)]


def exchange_start(name, ops, lands):
    n = len(ops)

    def body(*refs):
        xs, ls, send, recv, token = refs[:n], refs[n:2 * n], refs[2 * n], refs[2 * n + 1], refs[-1]
        for cp in _remote_copies(ops, xs, ls, send, recv):
            cp.start()
        token[...] = jnp.zeros_like(token)

    args = [pltpu.with_memory_space_constraint(a, pltpu.HBM) for a in [op[0] for op in ops] + list(lands)]
    sems = pltpu.SemaphoreType.DMA((n * (N_DEV - 1),))
    outs = pl.pallas_call(
        body, name=name,
        out_shape=(sems, sems, *[pltpu.HBM(a.shape, a.dtype) for a in args], jax.ShapeDtypeStruct((8, LANE), F32)),
        in_specs=[_HBM] * (2 * n),
        out_specs=(_SEM, _SEM, *[_HBM] * (2 * n), pl.BlockSpec(memory_space=pltpu.VMEM)),
        input_output_aliases={i: 2 + i for i in range(2 * n)},
        compiler_params=pltpu.CompilerParams(has_side_effects=_EFFECT),
    )(*args)
    return (ops, outs[0], outs[1], outs[2:2 + 2 * n]), outs[-1]


def exchange_wait(name, handle, after):
    ops, send, recv, thru = handle
    n = len(ops)

    def body(*refs):
        xs, ls, send_ref, recv_ref = refs[:n], refs[n:2 * n], refs[2 * n], refs[2 * n + 1]
        for cp in _remote_copies(ops, xs, ls, send_ref, recv_ref):
            cp.wait_send()
            cp.wait_recv()

    outs = pl.pallas_call(
        body, name=name,
        out_shape=[pltpu.HBM(a.shape, a.dtype) for a in thru],
        in_specs=[_HBM] * (2 * n) + [_SEM, _SEM, pl.BlockSpec(memory_space=pl.ANY)],
        out_specs=[_HBM] * (2 * n),
        input_output_aliases={i: i for i in range(2 * n)},
        compiler_params=pltpu.CompilerParams(has_side_effects=_EFFECT),
    )(*thru, send, recv, after)
    return outs[n:]


def _whole(ref, q):
    return ref


def _slot(ref, q):
    return ref.at[q]


def op_gather_stack(x):
    return (x, (N_DEV,) + x.shape, x.dtype, _whole, _slot)


def op_gather_axis(x, axis):
    size = x.shape[axis]
    shape = x.shape[:axis] + (N_DEV * size,) + x.shape[axis + 1:]

    def dst(ref, q):
        idx = [slice(None)] * len(shape)
        idx[axis] = pl.ds(pl.multiple_of(q * size, size), size)
        return ref.at[tuple(idx)]
    return (x, shape, x.dtype, _whole, dst)


def op_scatter_axis(x, axis):
    size = x.shape[axis] // N_DEV
    shape = x.shape[:axis] + (size,) + x.shape[axis + 1:]

    def src(ref, q):
        idx = [slice(None)] * len(shape)
        idx[axis] = pl.ds(pl.multiple_of(q * size, size), size)
        return ref.at[tuple(idx)]
    return (x, (N_DEV,) + shape, x.dtype, src, _slot)


def _adamw(w, g, m, v):
    m = ADAM_B1 * m + (1.0 - ADAM_B1) * g
    v = ADAM_B2 * v + (1.0 - ADAM_B2) * (g * g)
    m_hat = m / (1.0 - ADAM_B1 ** ADAM_STEP)
    v_hat = v / (1.0 - ADAM_B2 ** ADAM_STEP)
    delta = -ADAM_LR * (m_hat / (jnp.sqrt(v_hat) + ADAM_EPS) + ADAM_WD * w)
    return delta, m, v


def adamw_update(name, w, m, v, *, layer=None, landed=None, grad=None):
    R, W = w.shape[-2:]
    n_g = N_DEV if landed is not None else 1

    def fn(rows, vecs, c, i):
        g = rows[0].astype(F32)
        for q in range(1, n_g):
            g = g + rows[q].astype(F32)
        wv, mv, vv = rows[n_g:]
        delta, m2, v2 = _adamw(wv, g, mv, vv)
        return [[g], [delta], [m2], [v2]], []

    g_ins = [_ri(landed, lead=q) for q in range(N_DEV)] if landed is not None else [_ri(grad)]
    cw = LANE if W % LANE == 0 else W
    return rowwise(name, fn, R, g_ins + [_ri(w, lead=layer), _ri(m, lead=layer), _ri(v, lead=layer)], [],
                   [(F32, 1)] * 4, W=W, tr=128, cw=cw, rc=32)


def kernel(x, c, w_ada, b_ada, w_in, sink, w_dw, conv_ln_g, conv_ln_b, w_oa, w_ob, w_out, ln1_g, ln1_b, w_gu, w_down, ln2_g, ln2_b, loss_target, m_w_ada, m_b_ada, m_w_in, m_sink, m_w_dw, m_conv_ln_g, m_conv_ln_b, m_w_oa, m_w_ob, m_w_out, m_ln1_g, m_ln1_b, m_w_gu, m_w_down, m_ln2_g, m_ln2_b, v_w_ada, v_b_ada, v_w_in, v_sink, v_w_dw, v_conv_ln_g, v_conv_ln_b, v_w_oa, v_w_ob, v_w_out, v_ln1_g, v_ln1_b, v_w_gu, v_w_down, v_ln2_g, v_ln2_b):
    L = w_ada.shape[0]
    S, D = x.shape[1], x.shape[2]
    Dkv = D // GQA_GROUP
    Dqkv = D + 2 * Dkv
    DFF = w_down.shape[1] * N_DEV
    nq, nkv, nb = D // HEAD_DIM, Dkv // HEAD_DIM, S // BLOCK
    alpha = (2.0 * L) ** 0.25
    me = 4 * lax.axis_index("x") + 2 * lax.axis_index("y") + lax.axis_index("c")
    x0 = x.reshape(S, D)
    target = loss_target.reshape(S, D)
    t_cos, t_lo, t_hi = rope_tables(S)

    c_act = jax.nn.silu(c)
    c_all = exchange("gather_c", [op_gather_stack(c_act)])[0].reshape(N_DEV, D)
    c_pad = jnp.concatenate([c_all, jnp.zeros_like(c_all)], axis=0).astype(BF16)
    ada_cols = w_ada.shape[2]
    mod_part = jnp.stack([matmul("mod_mm", c_pad, w_ada[l], "nn", F32)[:N_DEV] for l in range(L)], axis=1)
    mod_land = exchange("scatter_mod", [op_scatter_axis(mod_part, 0)])[0]
    mod = jnp.transpose(mod_land.reshape(N_DEV, L, ada_cols), (1, 0, 2)).reshape(L, N_MOD * D) + b_ada
    mods = [[mod[l:l + 1, j * D:(j + 1) * D] for j in range(N_MOD)] for l in range(L)]

    def begin(tag, ops):
        return exchange_start(tag + "_start", ops, exchange_local(tag + "_local", ops))

    def gather_first(l):
        return [op_gather_axis(w_in[l].astype(BF16), 1), op_gather_axis(w_dw[l], 1)]

    def gather_rest(l):
        return [op_gather_axis(w_gu[l].astype(BF16), 1), op_gather_axis(w_oa[l].astype(BF16), 0),
                op_gather_axis(w_ob[l].astype(BF16), 0), op_gather_axis(w_out[l].astype(BF16), 0),
                op_gather_axis(w_down[l].astype(BF16), 0)]

    W_in, W_gu, W_oa, W_ob, W_out, W_down, W_dw = ([None] * L for _ in range(7))
    ag_first, _ = begin("ag0a", gather_first(0))

    def vec(a, l):
        return a[l:l + 1]

    def f_mod(rows, vecs, c_, i_):
        (xv,), (sc, sh) = rows, vecs
        return [[xv * (1.0 + sc) + sh]], []

    def f_rope(rows, vecs, c_, i_):
        xv, tc_, tl, th = rows
        return [[_rope(xv, tc_, tl, th, 1.0) if c_ < nq + nkv else xv]], []

    def f_convln(rows, vecs, c_, i_):
        (uc,), (g, b) = rows, vecs
        xhat, _ = _ln_stats(uc)
        nrm = xhat * g + b
        return [[nrm * _sig(nrm)]], []

    def f_merge(rows, vecs, c_, i_):
        g_a, g_b, y_a, y_b = rows
        return [[_sig(g_a) * y_a + _sig(g_b) * y_b]], []

    def f_ln(rows, vecs, c_, i_):
        (xv, r), (gt, g, b, sc, sh) = rows, vecs
        xhat, _ = _ln_stats(alpha * xv + (1.0 + gt) * r)
        y = xhat * g + b
        return [[y], [y * (1.0 + sc) + sh]], []

    def f_swiglu(rows, vecs, c_, i_):
        gate, up = rows
        return [[gate * _sig(gate) * up]], []

    saved = []
    xl = x0
    h = rowwise("modulate", f_mod, S, [_ri(x0)], [mods[0][1], mods[0][0]], [(BF16, 1)], W=D)[0]
    W_in[0], W_dw[0] = exchange_wait("ag0a_wait", ag_first, h)
    ag_next = None
    for l in range(L):
        sh_a, sc_a, gt_a, sh_f, sc_f, gt_f = mods[l]
        if l == 0:
            ag_rest, token = begin("ag0b", gather_rest(0))
        elif l + 1 < L:
            ag_next, token = begin("ag%d" % (l + 1), gather_first(l + 1) + gather_rest(l + 1))
        else:
            token = None
        p_qkv = matmul("in_qkv", h, W_in[l], "nn", F32, n=Dqkv, deps=[token] if token is not None else [])
        p_glu = matmul("in_glu", h, W_in[l], "nn", F32, b_off=Dqkv, n=2 * D)
        p_gate = matmul("in_gate", h, W_in[l], "nn", F32, b_off=Dqkv + 2 * D, n=2 * D)
        qkv = rowwise("rope", f_rope, S, [_ri(p_qkv), _ri(t_cos, whole=True), _ri(t_lo, whole=True), _ri(t_hi, whole=True)],
                      [], [(BF16, 1)], W=Dqkv, cw=HEAD_DIM, rc=64)[0]
        sinkcol = jnp.repeat(sink[l].reshape(nkv, GQA_GROUP), BLOCK, axis=1).reshape(nkv, GQA_GROUP * BLOCK, 1)
        att = attn_fwd(qkv, sinkcol, S, D, Dkv)
        uc = conv_fwd(p_glu, W_dw[l], S, D)
        token = None
        if l == 0:
            W_gu[0], W_oa[0], W_ob[0], W_out[0], W_down[0] = exchange_wait("ag0b_wait", ag_rest, uc)
            if L > 1:
                ag_next, token = begin("ag1", gather_first(1) + gather_rest(1))
        y_a = matmul("oa", att, W_oa[l], "nn", F32, deps=[token] if token is not None else [])
        z = rowwise("conv_ln", f_convln, S, [_ri(uc)], [vec(conv_ln_g, l), vec(conv_ln_b, l)], [(BF16, 1)], W=D)[0]
        y_b = matmul("ob", z, W_ob[l], "nn", F32)
        merged = rowwise("merge", f_merge, S, [_ri(p_gate, col=0), _ri(p_gate, col=1), _ri(y_a), _ri(y_b)], [],
                         [(BF16, 1)], W=D, cw=_pick(D, 512))[0]
        r1 = matmul("out", merged, W_out[l], "nn", F32)
        x1, h2 = rowwise("ln1", f_ln, S, [_ri(xl), _ri(r1)], [gt_a, vec(ln1_g, l), vec(ln1_b, l), sc_f, sh_f],
                         [(F32, 1), (BF16, 1)], W=D)
        gu = matmul("gu", h2, W_gu[l], "nn", F32)
        act = rowwise("swiglu", f_swiglu, S, [_ri(gu, col=0), _ri(gu, col=1)], [], [(BF16, 1)], W=DFF, tr=128,
                      cw=_pick(DFF, 512))[0]
        f = matmul("down", act, W_down[l], "nn", F32)
        nsc, nsh = (mods[l + 1][1], mods[l + 1][0]) if l + 1 < L else (sc_a, sh_a)
        x2, h_next = rowwise("ln2", f_ln, S, [_ri(x1), _ri(f)], [gt_f, vec(ln2_g, l), vec(ln2_b, l), nsc, nsh],
                             [(F32, 1), (BF16, 1)], W=D)
        saved.append(dict(x=xl, h=h, qkv=qkv, sinkcol=sinkcol, att=att, p_glu=p_glu, p_gate=p_gate, y_a=y_a, y_b=y_b,
                          uc=uc, z=z, merged=merged, r1=r1, x1=x1, h2=h2, gu=gu, act=act, f=f))
        xl, h = x2, h_next
        if l + 1 < L:
            (W_in[l + 1], W_dw[l + 1], W_gu[l + 1], W_oa[l + 1], W_ob[l + 1], W_out[l + 1],
             W_down[l + 1]) = exchange_wait("ag%d_wait" % (l + 1), ag_next, x2)

    def f_loss(rows, vecs, c_, i_):
        y, t = rows
        e = y - t
        return [[e * (1.0 / D)]], [e * e]

    dy, err = rowwise("loss", f_loss, S, [_ri(xl), _ri(target)], [], [(F32, 1)], 1, W=D)
    loss = lax.psum(0.5 * jnp.sum(err) / D, AXES)

    def f_ln_bwd_last(rows, vecs, c_, i_):
        (dout, xin, r), (gt, g) = rows, vecs
        xhat, rstd = _ln_stats(alpha * xin + (1.0 + gt) * r)
        dt = _ln_bwd(dout * g, xhat, rstd)
        return [[(1.0 + gt) * dt], [alpha * dt]], [dout * xhat, dout, dt * r]

    def f_ln_bwd(rows, vecs, c_, i_):
        (dres, dh, xout, xin, r), (sc, gt, g) = rows, vecs
        dout = dres + dh * (1.0 + sc)
        xhat, rstd = _ln_stats(alpha * xin + (1.0 + gt) * r)
        dt = _ln_bwd(dout * g, xhat, rstd)
        return [[(1.0 + gt) * dt], [alpha * dt]], [dout * xhat, dout, dt * r, dh * xout, dh]

    def f_swiglu_bwd(rows, vecs, c_, i_):
        da, gate, up = rows
        sg = _sig(gate)
        return [[da * up * sg * (1.0 + gate * (1.0 - sg)), da * (gate * sg)]], []

    def f_gate_bwd(rows, vecs, c_, i_):
        dm, g_a, g_b, y_a, y_b = rows
        sa, sb = _sig(g_a), _sig(g_b)
        return [[dm * sa], [dm * sb], [dm * y_a * sa * (1.0 - sa), dm * y_b * sb * (1.0 - sb)]], []

    def f_convln_bwd(rows, vecs, c_, i_):
        (dz, uc), (g, b) = rows, vecs
        xhat, rstd = _ln_stats(uc)
        nrm = xhat * g + b
        sg = _sig(nrm)
        dn = dz * sg * (1.0 + nrm * (1.0 - sg))
        return [[_ln_bwd(dn * g, xhat, rstd)]], [dn * xhat, dn]

    def f_unrope_q(rows, vecs, c_, i_):
        dq, tc_, tl, th = rows
        return [[_rope(dq, tc_, tl, th, -1.0)]], []

    def f_kv_combine(rows, vecs, c_, i_):
        k_lo, k_mid, k_hi, v_lo, v_mid, v_hi, tc_, tl, th = rows
        lo, hi = i_ > 0, i_ < nb - 1
        dk = jnp.where(lo, k_lo, 0.0) + k_mid + jnp.where(hi, k_hi, 0.0)
        dv = jnp.where(lo, v_lo, 0.0) + v_mid + jnp.where(hi, v_hi, 0.0)
        return [[_rope(dk, tc_, tl, th, -1.0)], [dv]], []

    def f_mod_bwd(rows, vecs, c_, i_):
        (dres, dh, xv), (sc,) = rows, vecs
        return [[dres + dh * (1.0 + sc)]], [dh * xv, dh]

    def flat(a):
        return a.reshape(-1, a.shape[-1])

    big_names = ("w_in", "w_gu", "w_oa", "w_ob", "w_out", "w_down")
    big_w = dict(w_in=w_in, w_gu=w_gu, w_oa=w_oa, w_ob=w_ob, w_out=w_out, w_down=w_down)
    big_m = dict(w_in=m_w_in, w_gu=m_w_gu, w_oa=m_w_oa, w_ob=m_w_ob, w_out=m_w_out, w_down=m_w_down)
    big_v = dict(w_in=v_w_in, w_gu=v_w_gu, w_oa=v_w_oa, w_ob=v_w_ob, w_out=v_w_out, w_down=v_w_down)
    big_axis = dict(w_in=1, w_gu=1, w_oa=0, w_ob=0, w_out=0, w_down=0)
    big_res = {nm: [None] * L for nm in big_names}
    dmod = [None] * L
    small = dict(sink=[None] * L, conv_ln_g=[None] * L, conv_ln_b=[None] * L, ln1_g=[None] * L, ln1_b=[None] * L,
                 ln2_g=[None] * L, ln2_b=[None] * L)
    dwdw = [None] * L

    early, late = ("w_down", "w_gu"), ("w_out", "w_oa", "w_ob", "w_in")

    def finish(names, tag, handle, after, l):
        for nm, land in zip(names, exchange_wait(tag + "_wait", handle, after)):
            big_res[nm][l] = adamw_update("adamw_" + nm, big_w[nm], big_m[nm], big_v[nm], layer=l, landed=land)

    def after_token(token):
        return [token] if token is not None else []

    dres, dh_next = dy, None
    rs_late, token = None, None
    for l in reversed(range(L)):
        sv = saved[l]
        sh_a, sc_a, gt_a, sh_f, sc_f, gt_f = mods[l]
        if dh_next is None:
            df, dres, d_g2, d_b2, d_gtf = rowwise(
                "ln2_bwd_last", f_ln_bwd_last, S, [_ri(dres), _ri(sv["x1"]), _ri(sv["f"])], [gt_f, vec(ln2_g, l)],
                [(BF16, 1), (F32, 1)], 3, W=D)
            d_sca_next = d_sha_next = None
        else:
            df, dres, d_g2, d_b2, d_gtf, d_sca_next, d_sha_next = rowwise(
                "ln2_bwd", f_ln_bwd, S, [_ri(dres), _ri(dh_next), _ri(saved[l + 1]["x"]), _ri(sv["x1"]), _ri(sv["f"])],
                [mods[l + 1][1], gt_f, vec(ln2_g, l)], [(BF16, 1), (F32, 1)], 5, W=D)
            dmod[l + 1][1], dmod[l + 1][0] = d_sca_next, d_sha_next
        dmod[l] = [None] * N_MOD
        dmod[l][5] = d_gtf
        small["ln2_g"][l], small["ln2_b"][l] = d_g2, d_b2
        g_down = matmul("d_w_down", sv["act"], df, "tn", BF16, deps=after_token(token))
        dact = matmul("d_act", df, W_down[l], "nt", F32)
        dgu = rowwise("swiglu_bwd", f_swiglu_bwd, S, [_ri(dact), _ri(sv["gu"], col=0), _ri(sv["gu"], col=1)], [],
                      [(BF16, 2)], W=DFF, tr=128, cw=_pick(DFF, 512))[0]
        g_gu = matmul("d_w_gu", sv["h2"], dgu, "tn", BF16)
        if rs_late is not None:
            finish(late, "rsb%d" % (l + 1), rs_late, g_gu, l + 1)
        rs_early, token = begin("rsa%d" % l, [op_scatter_axis(g_down, 0), op_scatter_axis(g_gu, 1)])
        dh2 = matmul("d_h2", dgu, W_gu[l], "nt", F32, deps=[token])
        dr1, dres, d_g1, d_b1, d_gta, d_scf, d_shf = rowwise(
            "ln1_bwd", f_ln_bwd, S, [_ri(dres), _ri(dh2), _ri(sv["x1"]), _ri(sv["x"]), _ri(sv["r1"])],
            [sc_f, gt_a, vec(ln1_g, l)], [(BF16, 1), (F32, 1)], 5, W=D)
        dmod[l][2], dmod[l][4], dmod[l][3] = d_gta, d_scf, d_shf
        small["ln1_g"][l], small["ln1_b"][l] = d_g1, d_b1
        g_out = matmul("d_w_out", sv["merged"], dr1, "tn", BF16)
        dmerged = matmul("d_merged", dr1, W_out[l], "nt", F32)
        dy_a, dy_b, dp_gate = rowwise(
            "gate_bwd", f_gate_bwd, S,
            [_ri(dmerged), _ri(sv["p_gate"], col=0), _ri(sv["p_gate"], col=1), _ri(sv["y_a"]), _ri(sv["y_b"])], [],
            [(BF16, 1), (BF16, 1), (BF16, 2)], W=D, cw=_pick(D, 512))
        g_oa = matmul("d_w_oa", sv["att"], dy_a, "tn", BF16)
        datt = matmul("d_att", dy_a, W_oa[l], "nt", BF16)
        g_ob = matmul("d_w_ob", sv["z"], dy_b, "tn", BF16)
        dz = matmul("d_z", dy_b, W_ob[l], "nt", F32)
        duc, d_cg, d_cb = rowwise("conv_ln_bwd", f_convln_bwd, S, [_ri(dz), _ri(sv["uc"])],
                                  [vec(conv_ln_g, l), vec(conv_ln_b, l)], [(F32, 1)], 2, W=D)
        small["conv_ln_g"][l], small["conv_ln_b"][l] = d_cg, d_cb
        dga, dgb, dwdw[l] = conv_bwd(duc, sv["p_glu"], W_dw[l], S, D)
        dq, dkp, dvp, dsink = attn_bwd(sv["qkv"], sv["sinkcol"], datt, S, D, Dkv)
        small["sink"][l] = dsink[:, :GQA_GROUP, 0].reshape(1, nq)
        dq_r = rowwise("unrope_q", f_unrope_q, S, [_ri(dq), _ri(t_cos, whole=True), _ri(t_lo, whole=True), _ri(t_hi, whole=True)],
                       [], [(BF16, 1)], W=D, cw=HEAD_DIM, rc=64)[0]
        dk_r, dv_r = rowwise(
            "kv_combine", f_kv_combine, S,
            [_ri(dkp, lead=2, shift=-1), _ri(dkp, lead=1), _ri(dkp, lead=0, shift=1),
             _ri(dvp, lead=2, shift=-1), _ri(dvp, lead=1), _ri(dvp, lead=0, shift=1),
             _ri(t_cos, whole=True), _ri(t_lo, whole=True), _ri(t_hi, whole=True)],
            [], [(BF16, 1), (BF16, 1)], W=Dkv, tr=BLOCK, cw=HEAD_DIM, rc=32)
        dp = jnp.concatenate([dq_r, dk_r, dv_r, dga, dgb, dp_gate], axis=1)
        g_in = matmul("d_w_in", sv["h"], dp, "tn", BF16)
        dh_next = matmul("d_h", dp, W_in[l], "nt", F32)

        finish(early, "rsa%d" % l, rs_early, dh_next, l)
        rs_late, token = begin("rsb%d" % l, [op_scatter_axis(g_out, 0), op_scatter_axis(g_oa, 0),
                                             op_scatter_axis(g_ob, 0), op_scatter_axis(g_in, 1)])

    grad_x, d_sca0, d_sha0 = rowwise("mod_bwd", f_mod_bwd, S, [_ri(dres), _ri(dh_next), _ri(x0)], [mods[0][1]],
                                     [(F32, 1)], 2, W=D)
    dmod[0][1], dmod[0][0] = d_sca0, d_sha0
    finish(late, "rsb0", rs_late, grad_x, 0)

    small_names = ("b_ada", "sink", "conv_ln_g", "conv_ln_b", "ln1_g", "ln1_b", "ln2_g", "ln2_b")
    small_w = dict(b_ada=b_ada, sink=sink, conv_ln_g=conv_ln_g, conv_ln_b=conv_ln_b, ln1_g=ln1_g, ln1_b=ln1_b,
                   ln2_g=ln2_g, ln2_b=ln2_b)
    small_m = dict(b_ada=m_b_ada, sink=m_sink, conv_ln_g=m_conv_ln_g, conv_ln_b=m_conv_ln_b, ln1_g=m_ln1_g,
                   ln1_b=m_ln1_b, ln2_g=m_ln2_g, ln2_b=m_ln2_b)
    small_v = dict(b_ada=v_b_ada, sink=v_sink, conv_ln_g=v_conv_ln_g, conv_ln_b=v_conv_ln_b, ln1_g=v_ln1_g,
                   ln1_b=v_ln1_b, ln2_g=v_ln2_g, ln2_b=v_ln2_b)
    small_g = dict(small)
    small_g["b_ada"] = [jnp.concatenate(dmod[l], axis=1) for l in range(L)]
    sizes = [small_w[nm].size for nm in small_names]
    total = sum(sizes)
    padded = -(-total // (SMALL_W * ROW_CHUNK)) * (SMALL_W * ROW_CHUNK)

    def pack(parts):
        flat_ = jnp.concatenate([p.reshape(-1) for p in parts] + [jnp.zeros((padded - total,), F32)])
        return flat_.reshape(padded // SMALL_W, SMALL_W)

    g_pack = pack([jnp.concatenate(small_g[nm], axis=0) for nm in small_names])
    small_land, dwdw_land = exchange("gather_small", [op_gather_stack(g_pack),
                                                      op_scatter_axis(jnp.stack(dwdw, axis=0), 2)])
    small_out = adamw_update("adamw_small", pack([small_w[nm] for nm in small_names]),
                             pack([small_m[nm] for nm in small_names]), pack([small_v[nm] for nm in small_names]),
                             landed=small_land)

    def unpack(buf):
        flat_, out, o = buf.reshape(-1), {}, 0
        for nm, sz in zip(small_names, sizes):
            out[nm] = flat_[o:o + sz].reshape(small_w[nm].shape)
            o += sz
        return out
    small_res = [unpack(b) for b in small_out]

    dw_cols = w_dw.shape[2]

    def pad_dw(a):
        return jnp.pad(a, ((0, 0), (0, 32 - CONV_WIDTH), (0, 0))).reshape(L * 32, dw_cols)
    dw_out = adamw_update("adamw_w_dw", pad_dw(w_dw), pad_dw(m_w_dw), pad_dw(v_w_dw),
                          landed=dwdw_land.reshape(N_DEV, L * 32, dw_cols))
    dw_res = [a.reshape(L, 32, dw_cols)[:, :CONV_WIDTH] for a in dw_out]

    dmod_all = small_land.reshape(N_DEV, -1)[:, :L * N_MOD * D].reshape(N_DEV, L, N_MOD * D)
    dmod_mine = lax.dynamic_slice_in_dim(dmod_all, me * ada_cols, ada_cols, axis=2)
    dmod_pad = jnp.concatenate([dmod_mine, jnp.zeros_like(dmod_mine)], axis=0).astype(BF16)
    g_ada = jnp.stack([matmul("d_w_ada", c_pad, dmod_pad[:, l], "tn", F32) for l in range(L)], axis=0)
    ada_out = adamw_update("adamw_w_ada", flat(w_ada), flat(m_w_ada), flat(v_w_ada), grad=flat(g_ada))
    ada_res = [a.reshape(w_ada.shape) for a in ada_out]

    order = ("w_ada", "b_ada", "w_in", "sink", "w_dw", "conv_ln_g", "conv_ln_b", "w_oa", "w_ob", "w_out",
             "ln1_g", "ln1_b", "w_gu", "w_down", "ln2_g", "ln2_b")

    def result(nm, j):
        if nm == "w_ada":
            return ada_res[j]
        if nm == "w_dw":
            return dw_res[j]
        if nm in big_res:
            return jnp.stack([big_res[nm][l][j] for l in range(L)], axis=0)
        return small_res[j][nm]

    outs = [loss, grad_x.reshape(x.shape)]
    for j in range(4):
        outs += [result(nm, j) for nm in order]
    return tuple(outs)
```

```python
import math

import jax
import jax.numpy as jnp
from jax import lax
from jax.experimental import pallas as pl
from jax.experimental.pallas import tpu as pltpu

F32 = jnp.float32
BF16 = jnp.bfloat16
N_DEV = 8
AXES = ("x", "y", "c")
HEAD_DIM = 128
GQA_GROUP = 4
BLOCK = 128
ROPE_DIM = HEAD_DIM // 4
ROPE_HALF = ROPE_DIM // 2
ROPE_THETA = 500000.0
CONV_WIDTH = 31
CONV_HALO = 16
N_MOD = 6
LN_EPS = 1e-5
NEG_INF = -1e30
ADAM_LR, ADAM_B1, ADAM_B2, ADAM_EPS, ADAM_WD, ADAM_STEP = 0.001, 0.9, 0.999, 1e-08, 0.01, 10
VMEM_LIMIT = 56 * 1024 * 1024
LANE = 128
ROW_CHUNK = 16
SMALL_W = 512


def _params(*sem):
    return pltpu.CompilerParams(dimension_semantics=sem, vmem_limit_bytes=VMEM_LIMIT)


def _pick(dim, pref, mult=LANE):
    if dim <= pref:
        return dim
    best = None
    for t in range(mult, pref + 1, mult):
        if dim % t == 0:
            best = t
    assert best is not None, (dim, pref)
    return best


def matmul(name, a, b, mode, out_dtype, *, b_off=0, n=None, tm=1024, tn=1408, tk=2816, deps=()):
    if mode == "nn":
        (M, K), N = a.shape, (n or b.shape[1])
    elif mode == "tn":
        (K, M), N = a.shape, b.shape[1]
    else:
        (M, K), N = a.shape, b.shape[0]
    tm = _pick(M, tn, LANE) if mode == "tn" else _pick(M, tm, 16)
    tn, tk = _pick(math.gcd(N, b_off) if b_off else N, tn), _pick(K, tk)
    assert b_off % tn == 0 and N % tn == 0
    boff = b_off // tn
    nk = K // tk
    if mode == "nn":
        a_spec = pl.BlockSpec((tm, tk), lambda i, j, k: (i, k))
        b_spec = pl.BlockSpec((tk, tn), lambda i, j, k: (k, j + boff))
        dims = (((1,), (0,)), ((), ()))
    elif mode == "tn":
        a_spec = pl.BlockSpec((tk, tm), lambda i, j, k: (k, i))
        b_spec = pl.BlockSpec((tk, tn), lambda i, j, k: (k, j))
        dims = (((0,), (0,)), ((), ()))
    else:
        a_spec = pl.BlockSpec((tm, tk), lambda i, j, k: (i, k))
        b_spec = pl.BlockSpec((tn, tk), lambda i, j, k: (j, k))
        dims = (((1,), (1,)), ((), ()))
    n_dep = len(deps)

    def body(a_ref, b_ref, *rest):
        o_ref = rest[n_dep]

        def dot():
            return lax.dot_general(a_ref[...].astype(BF16), b_ref[...].astype(BF16), dims, preferred_element_type=F32)
        if nk == 1:
            o_ref[...] = dot().astype(out_dtype)
        else:
            acc_ref = rest[n_dep + 1]
            k = pl.program_id(2)

            @pl.when(k == 0)
            def _():
                acc_ref[...] = jnp.zeros_like(acc_ref)
            acc_ref[...] += dot()

            @pl.when(k == nk - 1)
            def _():
                o_ref[...] = acc_ref[...].astype(out_dtype)

    return pl.pallas_call(
        body, name=name, grid=(M // tm, N // tn, nk),
        in_specs=[a_spec, b_spec] + [pl.BlockSpec(memory_space=pl.ANY)] * n_dep,
        out_specs=pl.BlockSpec((tm, tn), lambda i, j, k: (i, j)),
        out_shape=jax.ShapeDtypeStruct((M, N), out_dtype),
        scratch_shapes=[pltpu.VMEM((tm, tn), F32)] if nk > 1 else [],
        compiler_params=_params("parallel", "parallel", "arbitrary"),
    )(a, b, *deps)


def rowwise(name, fn, nrows, row_ins, vec_ins, row_outs, n_sums=0, *, W, tr=256, cw=None, rc=ROW_CHUNK):
    cw = cw or W
    tr = _pick(nrows, tr, ROW_CHUNK)
    rc = min(rc, tr)
    assert W % cw == 0 and nrows % ROW_CHUNK == 0 and tr % rc == 0 and rc % 8 == 0
    nrt = nrows // tr
    n_ri, n_v, n_ro = len(row_ins), len(vec_ins), len(row_outs)

    def row_spec(lead, colblk, shift, whole):
        w = cw if whole else W

        def rmap(i):
            return jnp.clip(i + shift, 0, nrt - 1) if shift else i
        if lead is None:
            return pl.BlockSpec((tr, w), lambda i: (rmap(i), colblk))
        return pl.BlockSpec((None, tr, w), lambda i: (lead, rmap(i), colblk))

    in_specs = [row_spec(*ri[1:]) for ri in row_ins]
    in_specs += [pl.BlockSpec((1, W), lambda i: (0, 0)) for _ in vec_ins]
    out_specs = [pl.BlockSpec((tr, p * W), lambda i: (i, 0)) for _, p in row_outs]
    out_specs += [pl.BlockSpec((1, W), lambda i: (0, 0)) for _ in range(n_sums)]
    out_shape = [jax.ShapeDtypeStruct((nrows, p * W), dt) for dt, p in row_outs]
    out_shape += [jax.ShapeDtypeStruct((1, W), F32) for _ in range(n_sums)]

    def body(*refs):
        rin, vin = refs[:n_ri], refs[n_ri:n_ri + n_v]
        rout = refs[n_ri + n_v:n_ri + n_v + n_ro]
        sout = refs[n_ri + n_v + n_ro:n_ri + n_v + n_ro + n_sums]
        acc = refs[n_ri + n_v + n_ro + n_sums:]
        i = pl.program_id(0)
        if n_sums:
            @pl.when(i == 0)
            def _():
                for a in acc:
                    a[...] = jnp.zeros_like(a)
        for c in range(W // cw):
            c0 = c * cw
            vecs = [v[:, c0:c0 + cw] for v in vin]

            def step(r, carry, c=c, c0=c0, vecs=vecs):
                r0 = pl.multiple_of(r * rc, rc)
                rows = [ref[pl.ds(r0, rc), :] if ri[4] else ref[pl.ds(r0, rc), c0:c0 + cw]
                        for ref, ri in zip(rin, row_ins)]
                outs, sums = fn(rows, vecs, c, i)
                for oref, (dt, _), pieces in zip(rout, row_outs, outs):
                    for pi, piece in enumerate(pieces):
                        oref[pl.ds(r0, rc), pi * W + c0:pi * W + c0 + cw] = piece.astype(dt)
                for a, s in zip(acc, sums):
                    part = s[0:8]
                    for q in range(1, rc // 8):
                        part = part + s[8 * q:8 * q + 8]
                    a[:, c0:c0 + cw] += part
                return carry
            lax.fori_loop(0, tr // rc, step, 0)
        if n_sums:
            @pl.when(i == nrt - 1)
            def _():
                for o, a in zip(sout, acc):
                    o[...] = jnp.sum(a[...], axis=0, keepdims=True)

    res = pl.pallas_call(
        body, name=name, grid=(nrt,), in_specs=in_specs, out_specs=out_specs, out_shape=out_shape,
        scratch_shapes=[pltpu.VMEM((8, W), F32) for _ in range(n_sums)],
        compiler_params=_params("arbitrary"),
    )(*[ri[0] for ri in row_ins], *vec_ins)
    return res


def _ri(arr, lead=None, col=0, shift=0, whole=False):
    return (arr, lead, col, shift, whole)


def _sig(x):
    return jax.nn.sigmoid(x)


def _ln_stats(t):
    mu = jnp.mean(t, axis=-1, keepdims=True)
    xc = t - mu
    var = jnp.mean(xc * xc, axis=-1, keepdims=True)
    rstd = lax.rsqrt(var + LN_EPS)
    return xc * rstd, rstd


def _ln_bwd(dy_g, xhat, rstd):
    m1 = jnp.mean(dy_g, axis=-1, keepdims=True)
    m2 = jnp.mean(dy_g * xhat, axis=-1, keepdims=True)
    return rstd * (dy_g - m1 - xhat * m2)


def _rope(x, cos, s_lo, s_hi, sign):
    up = pltpu.roll(x, HEAD_DIM - ROPE_HALF, 1)
    down = pltpu.roll(x, ROPE_HALF, 1)
    return x * cos + sign * (up * s_lo + down * s_hi)


def rope_tables(S):
    pos = jnp.arange(S, dtype=F32)
    inv_freq = ROPE_THETA ** (-jnp.arange(0, ROPE_DIM, 2, dtype=F32) / ROPE_DIM)
    ang = pos[:, None] * inv_freq[None, :]
    cos, sin = jnp.cos(ang), jnp.sin(ang)
    ones = jnp.ones((S, HEAD_DIM - ROPE_DIM), F32)
    zeros = jnp.zeros((S, HEAD_DIM - ROPE_DIM), F32)
    zh = jnp.zeros((S, ROPE_HALF), F32)
    t_cos = jnp.concatenate([cos, cos, ones], axis=1)
    t_lo = jnp.concatenate([-sin, zh, zeros], axis=1)
    t_hi = jnp.concatenate([zh, sin, zeros], axis=1)
    return t_cos, t_lo, t_hi


def _attn_specs(S, D, Dkv):
    nb, nkv, qb = S // BLOCK, Dkv // HEAD_DIM, D // HEAD_DIM
    gw = GQA_GROUP * HEAD_DIM
    q_spec = pl.BlockSpec((BLOCK, gw), lambda h, n: (n, h))

    def band(col0):
        return [pl.BlockSpec((BLOCK, HEAD_DIM), lambda h, n: (jnp.maximum(n - 1, 0), col0 + h)),
                pl.BlockSpec((BLOCK, HEAD_DIM), lambda h, n: (n, col0 + h)),
                pl.BlockSpec((BLOCK, HEAD_DIM), lambda h, n: (jnp.minimum(n + 1, nb - 1), col0 + h))]
    sink_spec = pl.BlockSpec((None, GQA_GROUP * BLOCK, 1), lambda h, n: (h, 0, 0))
    return nb, nkv, gw, q_spec, band(qb), band(qb + nkv), sink_spec


def _attn_probs(q_ref, k_refs, sink_ref, n, S):
    q = q_ref[...]
    qs = jnp.concatenate([q[:, g * HEAD_DIM:(g + 1) * HEAD_DIM] for g in range(GQA_GROUP)], axis=0)
    kb = jnp.concatenate([r[...] for r in k_refs], axis=0)
    s = lax.dot_general(qs, kb, (((1,), (1,)), ((), ())), preferred_element_type=F32) * (HEAD_DIM ** -0.5)
    shape = (GQA_GROUP * BLOCK, 3 * BLOCK)
    row = lax.broadcasted_iota(jnp.int32, shape, 0) & (BLOCK - 1)
    col = lax.broadcasted_iota(jnp.int32, shape, 1)
    rel = col - BLOCK - row
    kpos = (n - 1) * BLOCK + col
    valid = (jnp.abs(rel) <= BLOCK) & (kpos >= 0) & (kpos < S)
    s = jnp.where(valid, s, NEG_INF)
    sink = sink_ref[...]
    m = jnp.maximum(jnp.max(s, axis=-1, keepdims=True), sink)
    p = jnp.exp(s - m)
    e_sink = jnp.exp(sink - m)
    denom = jnp.sum(p, axis=-1, keepdims=True) + e_sink
    return qs, kb, p / denom, e_sink / denom


def attn_fwd(qkv, sinkcol, S, D, Dkv):
    nb, nkv, gw, q_spec, k_specs, v_specs, sink_spec = _attn_specs(S, D, Dkv)

    def body(q_ref, k0, k1, k2, v0, v1, v2, sink_ref, o_ref):
        n = pl.program_id(1)
        _, _, w, _ = _attn_probs(q_ref, (k0, k1, k2), sink_ref, n, S)
        vb = jnp.concatenate([v0[...], v1[...], v2[...]], axis=0)
        o = jnp.dot(w.astype(BF16), vb, preferred_element_type=F32)
        for g in range(GQA_GROUP):
            o_ref[:, g * HEAD_DIM:(g + 1) * HEAD_DIM] = o[g * BLOCK:(g + 1) * BLOCK].astype(BF16)

    return pl.pallas_call(
        body, name="attn_fwd", grid=(nkv, nb),
        in_specs=[q_spec, *k_specs, *v_specs, sink_spec],
        out_specs=pl.BlockSpec((BLOCK, gw), lambda h, n: (n, h)),
        out_shape=jax.ShapeDtypeStruct((S, D), BF16),
        compiler_params=_params("parallel", "arbitrary"),
    )(qkv, qkv, qkv, qkv, qkv, qkv, qkv, sinkcol)


def attn_bwd(qkv, sinkcol, datt, S, D, Dkv):
    nb, nkv, gw, q_spec, k_specs, v_specs, sink_spec = _attn_specs(S, D, Dkv)

    def body(q_ref, k0, k1, k2, v0, v1, v2, sink_ref, do_ref, dq_ref, dkp_ref, dvp_ref, dsink_ref):
        n = pl.program_id(1)
        qs, kb, w, w_sink = _attn_probs(q_ref, (k0, k1, k2), sink_ref, n, S)
        vb = jnp.concatenate([v0[...], v1[...], v2[...]], axis=0)
        do = do_ref[...]
        dos = jnp.concatenate([do[:, g * HEAD_DIM:(g + 1) * HEAD_DIM] for g in range(GQA_GROUP)], axis=0)
        dv = lax.dot_general(w.astype(BF16), dos, (((0,), (0,)), ((), ())), preferred_element_type=F32)
        dw = lax.dot_general(dos, vb, (((1,), (1,)), ((), ())), preferred_element_type=F32)
        delta = jnp.sum(w * dw, axis=-1, keepdims=True)
        ds = (w * (dw - delta) * (HEAD_DIM ** -0.5)).astype(BF16)
        dq = jnp.dot(ds, kb, preferred_element_type=F32)
        dk = lax.dot_general(ds, qs, (((0,), (0,)), ((), ())), preferred_element_type=F32)
        for g in range(GQA_GROUP):
            dq_ref[:, g * HEAD_DIM:(g + 1) * HEAD_DIM] = dq[g * BLOCK:(g + 1) * BLOCK]
        for j in range(3):
            dkp_ref[j] = dk[j * BLOCK:(j + 1) * BLOCK]
            dvp_ref[j] = dv[j * BLOCK:(j + 1) * BLOCK]

        @pl.when(n == 0)
        def _():
            dsink_ref[...] = jnp.zeros_like(dsink_ref)
        t = w_sink * delta
        for g in range(GQA_GROUP):
            dsink_ref[g:g + 1, :] -= jnp.sum(t[g * BLOCK:(g + 1) * BLOCK], axis=0, keepdims=True)

    part_spec = pl.BlockSpec((3, BLOCK, HEAD_DIM), lambda h, n: (0, n, h))
    return pl.pallas_call(
        body, name="attn_bwd", grid=(nkv, nb),
        in_specs=[q_spec, *k_specs, *v_specs, sink_spec, pl.BlockSpec((BLOCK, gw), lambda h, n: (n, h))],
        out_specs=[pl.BlockSpec((BLOCK, gw), lambda h, n: (n, h)), part_spec, part_spec,
                   pl.BlockSpec((None, 8, LANE), lambda h, n: (h, 0, 0))],
        out_shape=[jax.ShapeDtypeStruct((S, D), F32), jax.ShapeDtypeStruct((3, S, Dkv), F32),
                   jax.ShapeDtypeStruct((3, S, Dkv), F32), jax.ShapeDtypeStruct((nkv, 8, LANE), F32)],
        compiler_params=_params("parallel", "arbitrary"),
    )(qkv, qkv, qkv, qkv, qkv, qkv, qkv, sinkcol, datt)


CONV_TC = 128
CONV_ROWS = 64


def _shift_rows(win, b):
    return win if b == 0 else pltpu.roll(win, win.shape[0] - b, 0)


def _conv_fill_u(ga_ref, gb_ref, upad_ref, S):
    tc = upad_ref.shape[1]
    zero = jnp.zeros((CONV_HALO, tc), F32)
    upad_ref[0:CONV_HALO, :] = zero
    upad_ref[S + CONV_HALO:S + 2 * CONV_HALO, :] = zero

    def fill(r, carry):
        r0 = pl.multiple_of(r * CONV_ROWS, CONV_ROWS)
        upad_ref[pl.ds(r0 + CONV_HALO, CONV_ROWS), :] = ga_ref[pl.ds(r0, CONV_ROWS), :] * _sig(gb_ref[pl.ds(r0, CONV_ROWS), :])
        return carry
    lax.fori_loop(0, S // CONV_ROWS, fill, 0)


def conv_fwd(pglu, wdw, S, D):
    tc = min(CONV_TC, D)
    nct = D // tc

    def body(ga_ref, gb_ref, w_ref, o_ref, upad_ref):
        _conv_fill_u(ga_ref, gb_ref, upad_ref, S)
        w = w_ref[...]

        def step(r, carry):
            r0 = pl.multiple_of(r * CONV_ROWS, CONV_ROWS)
            win = upad_ref[pl.ds(r0, CONV_ROWS + 2 * CONV_HALO), :]
            acc = jnp.zeros((CONV_ROWS, tc), F32)
            for b in range(8):
                shifted = _shift_rows(win, b)
                for k in range(CONV_WIDTH):
                    o = k + CONV_HALO - CONV_WIDTH // 2
                    if o % 8 == b:
                        acc = acc + w[k:k + 1, :] * shifted[o - b:o - b + CONV_ROWS, :]
            o_ref[pl.ds(r0, CONV_ROWS), :] = acc
            return carry
        lax.fori_loop(0, S // CONV_ROWS, step, 0)

    return pl.pallas_call(
        body, name="conv_fwd", grid=(nct,),
        in_specs=[pl.BlockSpec((S, tc), lambda j: (0, j)), pl.BlockSpec((S, tc), lambda j: (0, nct + j)),
                  pl.BlockSpec((CONV_WIDTH, tc), lambda j: (0, j))],
        out_specs=pl.BlockSpec((S, tc), lambda j: (0, j)),
        out_shape=jax.ShapeDtypeStruct((S, D), F32),
        scratch_shapes=[pltpu.VMEM((S + 2 * CONV_HALO, tc), F32)],
        compiler_params=_params("parallel"),
    )(pglu, pglu, wdw)


def conv_bwd(duc, pglu, wdw, S, D):
    tc = min(CONV_TC, D)
    nct = D // tc
    half = CONV_WIDTH // 2

    def body(d_ref, ga_ref, gb_ref, w_ref, dga_ref, dgb_ref, dw_ref, upad_ref, dpad_ref, dwacc_ref):
        _conv_fill_u(ga_ref, gb_ref, upad_ref, S)
        zero = jnp.zeros((CONV_HALO, tc), F32)
        dpad_ref[0:CONV_HALO, :] = zero
        dpad_ref[S + CONV_HALO:S + 2 * CONV_HALO, :] = zero

        def fill(r, carry):
            r0 = pl.multiple_of(r * CONV_ROWS, CONV_ROWS)
            dpad_ref[pl.ds(r0 + CONV_HALO, CONV_ROWS), :] = d_ref[pl.ds(r0, CONV_ROWS), :]
            return carry
        lax.fori_loop(0, S // CONV_ROWS, fill, 0)
        dwacc_ref[...] = jnp.zeros_like(dwacc_ref)
        w = w_ref[...]

        def step(r, carry):
            r0 = pl.multiple_of(r * CONV_ROWS, CONV_ROWS)
            uwin = upad_ref[pl.ds(r0, CONV_ROWS + 2 * CONV_HALO), :]
            dwin = dpad_ref[pl.ds(r0, CONV_ROWS + 2 * CONV_HALO), :]
            d = dwin[CONV_HALO:CONV_HALO + CONV_ROWS, :]
            du = jnp.zeros((CONV_ROWS, tc), F32)
            for b in range(8):
                d_shifted, u_shifted = _shift_rows(dwin, b), _shift_rows(uwin, b)
                for k in range(CONV_WIDTH):
                    o = CONV_HALO + half - k
                    if o % 8 == b:
                        du = du + w[k:k + 1, :] * d_shifted[o - b:o - b + CONV_ROWS, :]
                    o = CONV_HALO + k - half
                    if o % 8 == b:
                        prod = d * u_shifted[o - b:o - b + CONV_ROWS, :]
                        part = prod[0:8]
                        for q in range(1, CONV_ROWS // 8):
                            part = part + prod[8 * q:8 * q + 8]
                        dwacc_ref[k] += part
            ga = ga_ref[pl.ds(r0, CONV_ROWS), :]
            sg = _sig(gb_ref[pl.ds(r0, CONV_ROWS), :])
            dga_ref[pl.ds(r0, CONV_ROWS), :] = (du * sg).astype(BF16)
            dgb_ref[pl.ds(r0, CONV_ROWS), :] = (du * ga * sg * (1.0 - sg)).astype(BF16)
            return carry
        lax.fori_loop(0, S // CONV_ROWS, step, 0)
        dw_ref[...] = jnp.sum(dwacc_ref[...], axis=1)

    return pl.pallas_call(
        body, name="conv_bwd", grid=(nct,),
        in_specs=[pl.BlockSpec((S, tc), lambda j: (0, j)), pl.BlockSpec((S, tc), lambda j: (0, j)),
                  pl.BlockSpec((S, tc), lambda j: (0, nct + j)), pl.BlockSpec((CONV_WIDTH, tc), lambda j: (0, j))],
        out_specs=[pl.BlockSpec((S, tc), lambda j: (0, j)), pl.BlockSpec((S, tc), lambda j: (0, j)),
                   pl.BlockSpec((32, tc), lambda j: (0, j))],
        out_shape=[jax.ShapeDtypeStruct((S, D), BF16), jax.ShapeDtypeStruct((S, D), BF16),
                   jax.ShapeDtypeStruct((32, D), F32)],
        scratch_shapes=[pltpu.VMEM((S + 2 * CONV_HALO, tc), F32), pltpu.VMEM((S + 2 * CONV_HALO, tc), F32),
                        pltpu.VMEM((32, 8, tc), F32)],
        compiler_params=_params("parallel"),
    )(duc, pglu, pglu, wdw)


def exchange(name, ops):
    n = len(ops)

    def body(*refs):
        xs, outs = refs[:n], refs[n:2 * n]
        send, recv, lsem = refs[2 * n:]
        mx, my, mc = lax.axis_index("x"), lax.axis_index("y"), lax.axis_index("c")
        me = 4 * mx + 2 * my + mc
        local = [pltpu.make_async_copy(op[3](xs[i], me), op[4](outs[i], me), lsem.at[i]) for i, op in enumerate(ops)]
        for cp in local:
            cp.start()
        copies = []
        for k in range(1, N_DEV):
            px = 1 - mx if k & 4 else mx
            py = 1 - my if k & 2 else my
            pc = 1 - mc if k & 1 else mc
            peer = 4 * px + 2 * py + pc
            for i, op in enumerate(ops):
                cp = pltpu.make_async_remote_copy(
                    src_ref=op[3](xs[i], peer), dst_ref=op[4](outs[i], me),
                    send_sem=send.at[i, k - 1], recv_sem=recv.at[i, k - 1],
                    device_id=(px, py, pc), device_id_type=pl.DeviceIdType.MESH)
                cp.start()
                copies.append(cp)
        for cp in copies:
            cp.wait()
        for cp in local:
            cp.wait()

    any_spec = pl.BlockSpec(memory_space=pl.ANY)
    return pl.pallas_call(
        body, name=name,
        in_specs=[any_spec] * n, out_specs=[any_spec] * n,
        out_shape=[jax.ShapeDtypeStruct(op[1], op[2]) for op in ops],
        scratch_shapes=[pltpu.SemaphoreType.DMA((n, N_DEV - 1)), pltpu.SemaphoreType.DMA((n, N_DEV - 1)),
                        pltpu.SemaphoreType.DMA((n,))],
        compiler_params=pltpu.CompilerParams(has_side_effects=True),
    )(*[op[0] for op in ops])


_HBM = pl.BlockSpec(memory_space=pltpu.HBM)
_SEM = pl.BlockSpec(memory_space=pltpu.SEMAPHORE)
_EFFECT = pltpu.SideEffectType.DATAFLOW_SIDE_EFFECTING


def _me_and_peers():
    mx, my, mc = lax.axis_index("x"), lax.axis_index("y"), lax.axis_index("c")
    peers = []
    for k in range(1, N_DEV):
        px = 1 - mx if k & 4 else mx
        py = 1 - my if k & 2 else my
        pc = 1 - mc if k & 1 else mc
        peers.append((k, (px, py, pc), 4 * px + 2 * py + pc))
    return 4 * mx + 2 * my + mc, peers


PLACE_STEPS = 4


def exchange_local(name, ops, me_arr, deps=()):
    n = len(ops)
    in_specs, out_specs = [], []
    for op in ops:
        kind, axis = op[5]
        rows, cols = op[0].shape if kind == "gather" else op[1][1:]
        steps = PLACE_STEPS if rows % (PLACE_STEPS * ROW_CHUNK) == 0 else 1
        tr = rows // steps

        def row(i, steps=steps):
            return i if steps > 1 else 0

        def window(i, me, axis=axis, steps=steps):
            return (me[0] * steps + row(i, steps), 0) if axis == 0 else (row(i, steps), me[0])
        if kind == "gather":
            in_specs.append(pl.BlockSpec((tr, cols), lambda i, me, row=row: (row(i), 0)))
            out_specs.append(pl.BlockSpec((tr, cols), window))
        else:
            in_specs.append(pl.BlockSpec((tr, cols), window))
            out_specs.append(pl.BlockSpec((None, tr, cols), lambda i, me, row=row: (me[0], row(i), 0)))

    def body(me_ref, *refs):
        for i in range(n):
            refs[n + len(deps) + i][...] = refs[i][...]

    return pl.pallas_call(
        body, name=name,
        grid_spec=pltpu.PrefetchScalarGridSpec(
            num_scalar_prefetch=1, grid=(PLACE_STEPS,),
            in_specs=in_specs + [pl.BlockSpec(memory_space=pl.ANY)] * len(deps), out_specs=out_specs),
        out_shape=[jax.ShapeDtypeStruct(op[1], op[2]) for op in ops],
        compiler_params=_params("arbitrary"),
    )(me_arr, *[op[0] for op in ops], *deps)


def _remote_copies(ops, xs, lands, send, recv):
    me, peers = _me_and_peers()
    return [pltpu.make_async_remote_copy(
        src_ref=op[3](xs[i], peer), dst_ref=op[4](lands[i], me),
        send_sem=send.at[i * (N_DEV - 1) + k - 1], recv_sem=recv.at[i * (N_DEV - 1) + k - 1],
        device_id=peer_id, device_id_type=pl.DeviceIdType.MESH)
        for k, peer_id, peer in peers for i, op in enumerate(ops)]


def exchange_start(name, ops, lands):
    n = len(ops)

    def body(*refs):
        xs, ls, send, recv, token = refs[:n], refs[n:2 * n], refs[2 * n], refs[2 * n + 1], refs[-1]
        for cp in _remote_copies(ops, xs, ls, send, recv):
            cp.start()
        token[...] = jnp.zeros_like(token)

    args = [pltpu.with_memory_space_constraint(a, pltpu.HBM) for a in [op[0] for op in ops] + list(lands)]
    sems = pltpu.SemaphoreType.DMA((n * (N_DEV - 1),))
    outs = pl.pallas_call(
        body, name=name,
        out_shape=(sems, sems, *[pltpu.HBM(a.shape, a.dtype) for a in args], jax.ShapeDtypeStruct((8, LANE), F32)),
        in_specs=[_HBM] * (2 * n),
        out_specs=(_SEM, _SEM, *[_HBM] * (2 * n), pl.BlockSpec(memory_space=pltpu.VMEM)),
        input_output_aliases={i: 2 + i for i in range(2 * n)},
        compiler_params=pltpu.CompilerParams(has_side_effects=_EFFECT),
    )(*args)
    return (ops, outs[0], outs[1], outs[2:2 + 2 * n]), outs[-1]


def exchange_wait(name, handle, after):
    ops, send, recv, thru = handle
    n = len(ops)

    def body(*refs):
        xs, ls, send_ref, recv_ref = refs[:n], refs[n:2 * n], refs[2 * n], refs[2 * n + 1]
        for cp in _remote_copies(ops, xs, ls, send_ref, recv_ref):
            cp.wait_send()
            cp.wait_recv()

    outs = pl.pallas_call(
        body, name=name,
        out_shape=[pltpu.HBM(a.shape, a.dtype) for a in thru],
        in_specs=[_HBM] * (2 * n) + [_SEM, _SEM, pl.BlockSpec(memory_space=pl.ANY)],
        out_specs=[_HBM] * (2 * n),
        input_output_aliases={i: i for i in range(2 * n)},
        compiler_params=pltpu.CompilerParams(has_side_effects=_EFFECT),
    )(*thru, send, recv, after)
    return outs[n:]


def _whole(ref, q):
    return ref


def _slot(ref, q):
    return ref.at[q]


def op_gather_stack(x):
    return (x, (N_DEV,) + x.shape, x.dtype, _whole, _slot)


def op_gather_axis(x, axis):
    size = x.shape[axis]
    shape = x.shape[:axis] + (N_DEV * size,) + x.shape[axis + 1:]

    def dst(ref, q):
        idx = [slice(None)] * len(shape)
        idx[axis] = pl.ds(pl.multiple_of(q * size, size), size)
        return ref.at[tuple(idx)]
    return (x, shape, x.dtype, _whole, dst, ("gather", axis))


def op_scatter_axis(x, axis):
    size = x.shape[axis] // N_DEV
    shape = x.shape[:axis] + (size,) + x.shape[axis + 1:]

    def src(ref, q):
        idx = [slice(None)] * len(shape)
        idx[axis] = pl.ds(pl.multiple_of(q * size, size), size)
        return ref.at[tuple(idx)]
    return (x, (N_DEV,) + shape, x.dtype, src, _slot, ("scatter", axis))


def _adamw(w, g, m, v):
    m = ADAM_B1 * m + (1.0 - ADAM_B1) * g
    v = ADAM_B2 * v + (1.0 - ADAM_B2) * (g * g)
    m_hat = m / (1.0 - ADAM_B1 ** ADAM_STEP)
    v_hat = v / (1.0 - ADAM_B2 ** ADAM_STEP)
    delta = -ADAM_LR * (m_hat / (jnp.sqrt(v_hat) + ADAM_EPS) + ADAM_WD * w)
    return delta, m, v


def adamw_update(name, w, m, v, *, layer=None, landed=None, grad=None):
    R, W = w.shape[-2:]
    n_g = N_DEV if landed is not None else 1

    def fn(rows, vecs, c, i):
        g = rows[0].astype(F32)
        for q in range(1, n_g):
            g = g + rows[q].astype(F32)
        wv, mv, vv = rows[n_g:]
        delta, m2, v2 = _adamw(wv, g, mv, vv)
        return [[g], [delta], [m2], [v2]], []

    g_ins = [_ri(landed, lead=q) for q in range(N_DEV)] if landed is not None else [_ri(grad)]
    cw = LANE if W % LANE == 0 else W
    return rowwise(name, fn, R, g_ins + [_ri(w, lead=layer), _ri(m, lead=layer), _ri(v, lead=layer)], [],
                   [(F32, 1)] * 4, W=W, tr=128, cw=cw, rc=32)


def kernel(x, c, w_ada, b_ada, w_in, sink, w_dw, conv_ln_g, conv_ln_b, w_oa, w_ob, w_out, ln1_g, ln1_b, w_gu, w_down, ln2_g, ln2_b, loss_target, m_w_ada, m_b_ada, m_w_in, m_sink, m_w_dw, m_conv_ln_g, m_conv_ln_b, m_w_oa, m_w_ob, m_w_out, m_ln1_g, m_ln1_b, m_w_gu, m_w_down, m_ln2_g, m_ln2_b, v_w_ada, v_b_ada, v_w_in, v_sink, v_w_dw, v_conv_ln_g, v_conv_ln_b, v_w_oa, v_w_ob, v_w_out, v_ln1_g, v_ln1_b, v_w_gu, v_w_down, v_ln2_g, v_ln2_b):
    L = w_ada.shape[0]
    S, D = x.shape[1], x.shape[2]
    Dkv = D // GQA_GROUP
    Dqkv = D + 2 * Dkv
    DFF = w_down.shape[1] * N_DEV
    nq, nkv, nb = D // HEAD_DIM, Dkv // HEAD_DIM, S // BLOCK
    alpha = (2.0 * L) ** 0.25
    me = 4 * lax.axis_index("x") + 2 * lax.axis_index("y") + lax.axis_index("c")
    x0 = x.reshape(S, D)
    target = loss_target.reshape(S, D)
    t_cos, t_lo, t_hi = rope_tables(S)

    c_act = jax.nn.silu(c)
    c_all = exchange("gather_c", [op_gather_stack(c_act)])[0].reshape(N_DEV, D)
    c_pad = jnp.concatenate([c_all, jnp.zeros_like(c_all)], axis=0).astype(BF16)
    ada_cols = w_ada.shape[2]
    mod_part = jnp.stack([matmul("mod_mm", c_pad, w_ada[l], "nn", F32)[:N_DEV] for l in range(L)], axis=1)
    mod_land = exchange("scatter_mod", [op_scatter_axis(mod_part, 0)])[0]
    mod = jnp.transpose(mod_land.reshape(N_DEV, L, ada_cols), (1, 0, 2)).reshape(L, N_MOD * D) + b_ada
    mods = [[mod[l:l + 1, j * D:(j + 1) * D] for j in range(N_MOD)] for l in range(L)]

    me_arr = me.astype(jnp.int32).reshape(1)

    def begin(tag, ops, dep):
        return exchange_start(tag + "_start", ops, exchange_local(tag + "_local", ops, me_arr, [dep]))

    def gather_first(l):
        return [op_gather_axis(w_in[l].astype(BF16), 1), op_gather_axis(w_dw[l], 1)]

    def gather_rest(l):
        return [op_gather_axis(w_gu[l].astype(BF16), 1), op_gather_axis(w_oa[l].astype(BF16), 0),
                op_gather_axis(w_ob[l].astype(BF16), 0), op_gather_axis(w_out[l].astype(BF16), 0),
                op_gather_axis(w_down[l].astype(BF16), 0)]

    W_in, W_gu, W_oa, W_ob, W_out, W_down, W_dw = ([None] * L for _ in range(7))
    ag_first, _ = begin("ag0a", gather_first(0), mod_land)

    def vec(a, l):
        return a[l:l + 1]

    def f_mod(rows, vecs, c_, i_):
        (xv,), (sc, sh) = rows, vecs
        return [[xv * (1.0 + sc) + sh]], []

    def f_rope(rows, vecs, c_, i_):
        xv, tc_, tl, th = rows
        return [[_rope(xv, tc_, tl, th, 1.0) if c_ < nq + nkv else xv]], []

    def f_convln(rows, vecs, c_, i_):
        (uc,), (g, b) = rows, vecs
        xhat, _ = _ln_stats(uc)
        nrm = xhat * g + b
        return [[nrm * _sig(nrm)]], []

    def f_merge(rows, vecs, c_, i_):
        g_a, g_b, y_a, y_b = rows
        return [[_sig(g_a) * y_a + _sig(g_b) * y_b]], []

    def f_ln(rows, vecs, c_, i_):
        (xv, r), (gt, g, b, sc, sh) = rows, vecs
        xhat, _ = _ln_stats(alpha * xv + (1.0 + gt) * r)
        y = xhat * g + b
        return [[y], [y * (1.0 + sc) + sh]], []

    def f_swiglu(rows, vecs, c_, i_):
        gate, up = rows
        return [[gate * _sig(gate) * up]], []

    saved = []
    xl = x0
    h = rowwise("modulate", f_mod, S, [_ri(x0)], [mods[0][1], mods[0][0]], [(BF16, 1)], W=D)[0]
    W_in[0], W_dw[0] = exchange_wait("ag0a_wait", ag_first, h)
    ag_next = None
    for l in range(L):
        sh_a, sc_a, gt_a, sh_f, sc_f, gt_f = mods[l]
        if l == 0:
            ag_rest, token = begin("ag0b", gather_rest(0), W_in[0])
        elif l + 1 < L:
            ag_next, token = begin("ag%d" % (l + 1), gather_first(l + 1) + gather_rest(l + 1), W_in[l])
        else:
            token = None
        p_qkv = matmul("in_qkv", h, W_in[l], "nn", F32, n=Dqkv, deps=[token] if token is not None else [])
        p_glu = matmul("in_glu", h, W_in[l], "nn", F32, b_off=Dqkv, n=2 * D)
        p_gate = matmul("in_gate", h, W_in[l], "nn", F32, b_off=Dqkv + 2 * D, n=2 * D)
        qkv = rowwise("rope", f_rope, S, [_ri(p_qkv), _ri(t_cos, whole=True), _ri(t_lo, whole=True), _ri(t_hi, whole=True)],
                      [], [(BF16, 1)], W=Dqkv, cw=HEAD_DIM, rc=64)[0]
        sinkcol = jnp.repeat(sink[l].reshape(nkv, GQA_GROUP), BLOCK, axis=1).reshape(nkv, GQA_GROUP * BLOCK, 1)
        att = attn_fwd(qkv, sinkcol, S, D, Dkv)
        uc = conv_fwd(p_glu, W_dw[l], S, D)
        token = None
        if l == 0:
            W_gu[0], W_oa[0], W_ob[0], W_out[0], W_down[0] = exchange_wait("ag0b_wait", ag_rest, uc)
            if L > 1:
                ag_next, token = begin("ag1", gather_first(1) + gather_rest(1), W_gu[0])
        y_a = matmul("oa", att, W_oa[l], "nn", F32, deps=[token] if token is not None else [])
        z = rowwise("conv_ln", f_convln, S, [_ri(uc)], [vec(conv_ln_g, l), vec(conv_ln_b, l)], [(BF16, 1)], W=D)[0]
        y_b = matmul("ob", z, W_ob[l], "nn", F32)
        merged = rowwise("merge", f_merge, S, [_ri(p_gate, col=0), _ri(p_gate, col=1), _ri(y_a), _ri(y_b)], [],
                         [(BF16, 1)], W=D, cw=_pick(D, 512))[0]
        r1 = matmul("out", merged, W_out[l], "nn", F32)
        x1, h2 = rowwise("ln1", f_ln, S, [_ri(xl), _ri(r1)], [gt_a, vec(ln1_g, l), vec(ln1_b, l), sc_f, sh_f],
                         [(F32, 1), (BF16, 1)], W=D)
        gu = matmul("gu", h2, W_gu[l], "nn", F32)
        act = rowwise("swiglu", f_swiglu, S, [_ri(gu, col=0), _ri(gu, col=1)], [], [(BF16, 1)], W=DFF, tr=128,
                      cw=_pick(DFF, 512))[0]
        f = matmul("down", act, W_down[l], "nn", F32)
        nsc, nsh = (mods[l + 1][1], mods[l + 1][0]) if l + 1 < L else (sc_a, sh_a)
        x2, h_next = rowwise("ln2", f_ln, S, [_ri(x1), _ri(f)], [gt_f, vec(ln2_g, l), vec(ln2_b, l), nsc, nsh],
                             [(F32, 1), (BF16, 1)], W=D)
        saved.append(dict(x=xl, h=h, qkv=qkv, sinkcol=sinkcol, att=att, p_glu=p_glu, p_gate=p_gate, y_a=y_a, y_b=y_b,
                          uc=uc, z=z, merged=merged, r1=r1, x1=x1, h2=h2, gu=gu, act=act, f=f))
        xl, h = x2, h_next
        if l + 1 < L:
            (W_in[l + 1], W_dw[l + 1], W_gu[l + 1], W_oa[l + 1], W_ob[l + 1], W_out[l + 1],
             W_down[l + 1]) = exchange_wait("ag%d_wait" % (l + 1), ag_next, x2)

    def f_loss(rows, vecs, c_, i_):
        y, t = rows
        e = y - t
        return [[e * (1.0 / D)]], [e * e]

    dy, err = rowwise("loss", f_loss, S, [_ri(xl), _ri(target)], [], [(F32, 1)], 1, W=D)
    loss = lax.psum(0.5 * jnp.sum(err) / D, AXES)

    def f_ln_bwd_last(rows, vecs, c_, i_):
        (dout, xin, r), (gt, g) = rows, vecs
        xhat, rstd = _ln_stats(alpha * xin + (1.0 + gt) * r)
        dt = _ln_bwd(dout * g, xhat, rstd)
        return [[(1.0 + gt) * dt], [alpha * dt]], [dout * xhat, dout, dt * r]

    def f_ln_bwd(rows, vecs, c_, i_):
        (dres, dh, xout, xin, r), (sc, gt, g) = rows, vecs
        dout = dres + dh * (1.0 + sc)
        xhat, rstd = _ln_stats(alpha * xin + (1.0 + gt) * r)
        dt = _ln_bwd(dout * g, xhat, rstd)
        return [[(1.0 + gt) * dt], [alpha * dt]], [dout * xhat, dout, dt * r, dh * xout, dh]

    def f_swiglu_bwd(rows, vecs, c_, i_):
        da, gate, up = rows
        sg = _sig(gate)
        return [[da * up * sg * (1.0 + gate * (1.0 - sg)), da * (gate * sg)]], []

    def f_gate_bwd(rows, vecs, c_, i_):
        dm, g_a, g_b, y_a, y_b = rows
        sa, sb = _sig(g_a), _sig(g_b)
        return [[dm * sa], [dm * sb], [dm * y_a * sa * (1.0 - sa), dm * y_b * sb * (1.0 - sb)]], []

    def f_convln_bwd(rows, vecs, c_, i_):
        (dz, uc), (g, b) = rows, vecs
        xhat, rstd = _ln_stats(uc)
        nrm = xhat * g + b
        sg = _sig(nrm)
        dn = dz * sg * (1.0 + nrm * (1.0 - sg))
        return [[_ln_bwd(dn * g, xhat, rstd)]], [dn * xhat, dn]

    def f_unrope_q(rows, vecs, c_, i_):
        dq, tc_, tl, th = rows
        return [[_rope(dq, tc_, tl, th, -1.0)]], []

    def f_kv_combine(rows, vecs, c_, i_):
        k_lo, k_mid, k_hi, v_lo, v_mid, v_hi, tc_, tl, th = rows
        lo, hi = i_ > 0, i_ < nb - 1
        dk = jnp.where(lo, k_lo, 0.0) + k_mid + jnp.where(hi, k_hi, 0.0)
        dv = jnp.where(lo, v_lo, 0.0) + v_mid + jnp.where(hi, v_hi, 0.0)
        return [[_rope(dk, tc_, tl, th, -1.0)], [dv]], []

    def f_mod_bwd(rows, vecs, c_, i_):
        (dres, dh, xv), (sc,) = rows, vecs
        return [[dres + dh * (1.0 + sc)]], [dh * xv, dh]

    def flat(a):
        return a.reshape(-1, a.shape[-1])

    big_names = ("w_in", "w_gu", "w_oa", "w_ob", "w_out", "w_down")
    big_w = dict(w_in=w_in, w_gu=w_gu, w_oa=w_oa, w_ob=w_ob, w_out=w_out, w_down=w_down)
    big_m = dict(w_in=m_w_in, w_gu=m_w_gu, w_oa=m_w_oa, w_ob=m_w_ob, w_out=m_w_out, w_down=m_w_down)
    big_v = dict(w_in=v_w_in, w_gu=v_w_gu, w_oa=v_w_oa, w_ob=v_w_ob, w_out=v_w_out, w_down=v_w_down)
    big_axis = dict(w_in=1, w_gu=1, w_oa=0, w_ob=0, w_out=0, w_down=0)
    big_res = {nm: [None] * L for nm in big_names}
    dmod = [None] * L
    small = dict(sink=[None] * L, conv_ln_g=[None] * L, conv_ln_b=[None] * L, ln1_g=[None] * L, ln1_b=[None] * L,
                 ln2_g=[None] * L, ln2_b=[None] * L)
    dwdw = [None] * L

    early, late = ("w_down", "w_gu"), ("w_out", "w_oa", "w_ob", "w_in")

    def finish(names, tag, handle, after, l):
        landed = exchange_wait(tag + "_wait", handle, after)
        for nm, land in zip(names, landed):
            big_res[nm][l] = adamw_update("adamw_" + nm, big_w[nm], big_m[nm], big_v[nm], layer=l, landed=land)
        return landed[0]

    def after_token(token):
        return [token] if token is not None else []

    dres, dh_next = dy, None
    rs_late, token = None, None
    for l in reversed(range(L)):
        sv = saved[l]
        sh_a, sc_a, gt_a, sh_f, sc_f, gt_f = mods[l]
        if dh_next is None:
            df, dres, d_g2, d_b2, d_gtf = rowwise(
                "ln2_bwd_last", f_ln_bwd_last, S, [_ri(dres), _ri(sv["x1"]), _ri(sv["f"])], [gt_f, vec(ln2_g, l)],
                [(BF16, 1), (F32, 1)], 3, W=D)
            d_sca_next = d_sha_next = None
        else:
            df, dres, d_g2, d_b2, d_gtf, d_sca_next, d_sha_next = rowwise(
                "ln2_bwd", f_ln_bwd, S, [_ri(dres), _ri(dh_next), _ri(saved[l + 1]["x"]), _ri(sv["x1"]), _ri(sv["f"])],
                [mods[l + 1][1], gt_f, vec(ln2_g, l)], [(BF16, 1), (F32, 1)], 5, W=D)
            dmod[l + 1][1], dmod[l + 1][0] = d_sca_next, d_sha_next
        dmod[l] = [None] * N_MOD
        dmod[l][5] = d_gtf
        small["ln2_g"][l], small["ln2_b"][l] = d_g2, d_b2
        g_down = matmul("d_w_down", sv["act"], df, "tn", BF16, deps=after_token(token))
        dact = matmul("d_act", df, W_down[l], "nt", F32)
        dgu = rowwise("swiglu_bwd", f_swiglu_bwd, S, [_ri(dact), _ri(sv["gu"], col=0), _ri(sv["gu"], col=1)], [],
                      [(BF16, 2)], W=DFF, tr=128, cw=_pick(DFF, 512))[0]
        g_gu = matmul("d_w_gu", sv["h2"], dgu, "tn", BF16)
        dh2 = matmul("d_h2", dgu, W_gu[l], "nt", F32)
        dr1, dres, d_g1, d_b1, d_gta, d_scf, d_shf = rowwise(
            "ln1_bwd", f_ln_bwd, S, [_ri(dres), _ri(dh2), _ri(sv["x1"]), _ri(sv["x"]), _ri(sv["r1"])],
            [sc_f, gt_a, vec(ln1_g, l)], [(BF16, 1), (F32, 1)], 5, W=D)
        dmod[l][2], dmod[l][4], dmod[l][3] = d_gta, d_scf, d_shf
        small["ln1_g"][l], small["ln1_b"][l] = d_g1, d_b1
        g_out = matmul("d_w_out", sv["merged"], dr1, "tn", BF16)
        dmerged = matmul("d_merged", dr1, W_out[l], "nt", F32)
        before = finish(late, "rsb%d" % (l + 1), rs_late, dmerged, l + 1) if rs_late is not None else g_gu
        rs_early, token = begin("rsa%d" % l, [op_scatter_axis(g_down, 0), op_scatter_axis(g_gu, 1)], before)
        dy_a, dy_b, dp_gate = rowwise(
            "gate_bwd", f_gate_bwd, S,
            [_ri(dmerged), _ri(sv["p_gate"], col=0), _ri(sv["p_gate"], col=1), _ri(sv["y_a"]), _ri(sv["y_b"])], [],
            [(BF16, 1), (BF16, 1), (BF16, 2)], W=D, cw=_pick(D, 512))
        g_oa = matmul("d_w_oa", sv["att"], dy_a, "tn", BF16, deps=[token])
        datt = matmul("d_att", dy_a, W_oa[l], "nt", BF16)
        g_ob = matmul("d_w_ob", sv["z"], dy_b, "tn", BF16)
        dz = matmul("d_z", dy_b, W_ob[l], "nt", F32)
        duc, d_cg, d_cb = rowwise("conv_ln_bwd", f_convln_bwd, S, [_ri(dz), _ri(sv["uc"])],
                                  [vec(conv_ln_g, l), vec(conv_ln_b, l)], [(F32, 1)], 2, W=D)
        small["conv_ln_g"][l], small["conv_ln_b"][l] = d_cg, d_cb
        dga, dgb, dwdw[l] = conv_bwd(duc, sv["p_glu"], W_dw[l], S, D)
        dq, dkp, dvp, dsink = attn_bwd(sv["qkv"], sv["sinkcol"], datt, S, D, Dkv)
        small["sink"][l] = dsink[:, :GQA_GROUP, 0].reshape(1, nq)
        dq_r = rowwise("unrope_q", f_unrope_q, S, [_ri(dq), _ri(t_cos, whole=True), _ri(t_lo, whole=True), _ri(t_hi, whole=True)],
                       [], [(BF16, 1)], W=D, cw=HEAD_DIM, rc=64)[0]
        dk_r, dv_r = rowwise(
            "kv_combine", f_kv_combine, S,
            [_ri(dkp, lead=2, shift=-1), _ri(dkp, lead=1), _ri(dkp, lead=0, shift=1),
             _ri(dvp, lead=2, shift=-1), _ri(dvp, lead=1), _ri(dvp, lead=0, shift=1),
             _ri(t_cos, whole=True), _ri(t_lo, whole=True), _ri(t_hi, whole=True)],
            [], [(BF16, 1), (BF16, 1)], W=Dkv, tr=BLOCK, cw=HEAD_DIM, rc=32)
        dp = jnp.concatenate([dq_r, dk_r, dv_r, dga, dgb, dp_gate], axis=1)
        g_in = matmul("d_w_in", sv["h"], dp, "tn", BF16)
        dh_next = matmul("d_h", dp, W_in[l], "nt", F32)

        before = finish(early, "rsa%d" % l, rs_early, dh_next, l)
        rs_late, token = begin("rsb%d" % l, [op_scatter_axis(g_out, 0), op_scatter_axis(g_oa, 0),
                                             op_scatter_axis(g_ob, 0), op_scatter_axis(g_in, 1)], before)

    grad_x, d_sca0, d_sha0 = rowwise("mod_bwd", f_mod_bwd, S, [_ri(dres), _ri(dh_next), _ri(x0)], [mods[0][1]],
                                     [(F32, 1)], 2, W=D)
    dmod[0][1], dmod[0][0] = d_sca0, d_sha0
    finish(late, "rsb0", rs_late, grad_x, 0)

    small_names = ("b_ada", "sink", "conv_ln_g", "conv_ln_b", "ln1_g", "ln1_b", "ln2_g", "ln2_b")
    small_w = dict(b_ada=b_ada, sink=sink, conv_ln_g=conv_ln_g, conv_ln_b=conv_ln_b, ln1_g=ln1_g, ln1_b=ln1_b,
                   ln2_g=ln2_g, ln2_b=ln2_b)
    small_m = dict(b_ada=m_b_ada, sink=m_sink, conv_ln_g=m_conv_ln_g, conv_ln_b=m_conv_ln_b, ln1_g=m_ln1_g,
                   ln1_b=m_ln1_b, ln2_g=m_ln2_g, ln2_b=m_ln2_b)
    small_v = dict(b_ada=v_b_ada, sink=v_sink, conv_ln_g=v_conv_ln_g, conv_ln_b=v_conv_ln_b, ln1_g=v_ln1_g,
                   ln1_b=v_ln1_b, ln2_g=v_ln2_g, ln2_b=v_ln2_b)
    small_g = dict(small)
    small_g["b_ada"] = [jnp.concatenate(dmod[l], axis=1) for l in range(L)]
    sizes = [small_w[nm].size for nm in small_names]
    total = sum(sizes)
    padded = -(-total // (SMALL_W * ROW_CHUNK)) * (SMALL_W * ROW_CHUNK)

    def pack(parts):
        flat_ = jnp.concatenate([p.reshape(-1) for p in parts] + [jnp.zeros((padded - total,), F32)])
        return flat_.reshape(padded // SMALL_W, SMALL_W)

    g_pack = pack([jnp.concatenate(small_g[nm], axis=0) for nm in small_names])
    small_land, dwdw_land = exchange("gather_small", [op_gather_stack(g_pack),
                                                      op_scatter_axis(jnp.stack(dwdw, axis=0), 2)])
    small_out = adamw_update("adamw_small", pack([small_w[nm] for nm in small_names]),
                             pack([small_m[nm] for nm in small_names]), pack([small_v[nm] for nm in small_names]),
                             landed=small_land)

    def unpack(buf):
        flat_, out, o = buf.reshape(-1), {}, 0
        for nm, sz in zip(small_names, sizes):
            out[nm] = flat_[o:o + sz].reshape(small_w[nm].shape)
            o += sz
        return out
    small_res = [unpack(b) for b in small_out]

    dw_cols = w_dw.shape[2]

    def pad_dw(a):
        return jnp.pad(a, ((0, 0), (0, 32 - CONV_WIDTH), (0, 0))).reshape(L * 32, dw_cols)
    dw_out = adamw_update("adamw_w_dw", pad_dw(w_dw), pad_dw(m_w_dw), pad_dw(v_w_dw),
                          landed=dwdw_land.reshape(N_DEV, L * 32, dw_cols))
    dw_res = [a.reshape(L, 32, dw_cols)[:, :CONV_WIDTH] for a in dw_out]

    dmod_all = small_land.reshape(N_DEV, -1)[:, :L * N_MOD * D].reshape(N_DEV, L, N_MOD * D)
    dmod_mine = lax.dynamic_slice_in_dim(dmod_all, me * ada_cols, ada_cols, axis=2)
    dmod_pad = jnp.concatenate([dmod_mine, jnp.zeros_like(dmod_mine)], axis=0).astype(BF16)
    g_ada = jnp.stack([matmul("d_w_ada", c_pad, dmod_pad[:, l], "tn", F32) for l in range(L)], axis=0)
    ada_out = adamw_update("adamw_w_ada", flat(w_ada), flat(m_w_ada), flat(v_w_ada), grad=flat(g_ada))
    ada_res = [a.reshape(w_ada.shape) for a in ada_out]

    order = ("w_ada", "b_ada", "w_in", "sink", "w_dw", "conv_ln_g", "conv_ln_b", "w_oa", "w_ob", "w_out",
             "ln1_g", "ln1_b", "w_gu", "w_down", "ln2_g", "ln2_b")

    def result(nm, j):
        if nm == "w_ada":
            return ada_res[j]
        if nm == "w_dw":
            return dw_res[j]
        if nm in big_res:
            return jnp.stack([big_res[nm][l][j] for l in range(L)], axis=0)
        return small_res[j][nm]

    outs = [loss, grad_x.reshape(x.shape)]
    for j in range(4):
        outs += [result(nm, j) for nm in order]
    return tuple(outs)
```

```python
import math

import jax
import jax.numpy as jnp
from jax import lax
from jax.experimental import pallas as pl
from jax.experimental.pallas import tpu as pltpu

F32 = jnp.float32
BF16 = jnp.bfloat16
N_DEV = 8
AXES = ("x", "y", "c")
HEAD_DIM = 128
GQA_GROUP = 4
BLOCK = 128
ROPE_DIM = HEAD_DIM // 4
ROPE_HALF = ROPE_DIM // 2
ROPE_THETA = 500000.0
CONV_WIDTH = 31
CONV_HALO = 16
N_MOD = 6
LN_EPS = 1e-5
NEG_INF = -1e30
ADAM_LR, ADAM_B1, ADAM_B2, ADAM_EPS, ADAM_WD, ADAM_STEP = 0.001, 0.9, 0.999, 1e-08, 0.01, 10
VMEM_LIMIT = 56 * 1024 * 1024
LANE = 128
ROW_CHUNK = 16
SMALL_W = 512


def _params(*sem):
    return pltpu.CompilerParams(dimension_semantics=sem, vmem_limit_bytes=VMEM_LIMIT)


def _pick(dim, pref, mult=LANE):
    if dim <= pref:
        return dim
    best = None
    for t in range(mult, pref + 1, mult):
        if dim % t == 0:
            best = t
    assert best is not None, (dim, pref)
    return best


def matmul(name, a, b, mode, out_dtype, *, b_off=0, n=None, tm=1024, tn=1408, tk=2816, deps=(), epilogue=None,
           row_extras=()):
    if mode == "nn":
        (M, K), N = a.shape, (n or b.shape[1])
    elif mode == "tn":
        (K, M), N = a.shape, b.shape[1]
    else:
        (M, K), N = a.shape, b.shape[0]
    tm = _pick(M, tn, LANE) if mode == "tn" else _pick(M, tm, 16)
    tn, tk = _pick(math.gcd(N, b_off) if b_off else N, tn), _pick(K, tk)
    assert b_off % tn == 0 and N % tn == 0
    boff = b_off // tn
    nk = K // tk
    if mode == "nn":
        a_spec = pl.BlockSpec((tm, tk), lambda i, j, k: (i, k))
        b_spec = pl.BlockSpec((tk, tn), lambda i, j, k: (k, j + boff))
        dims = (((1,), (0,)), ((), ()))
    elif mode == "tn":
        a_spec = pl.BlockSpec((tk, tm), lambda i, j, k: (k, i))
        b_spec = pl.BlockSpec((tk, tn), lambda i, j, k: (k, j))
        dims = (((0,), (0,)), ((), ()))
    else:
        a_spec = pl.BlockSpec((tm, tk), lambda i, j, k: (i, k))
        b_spec = pl.BlockSpec((tn, tk), lambda i, j, k: (j, k))
        dims = (((1,), (1,)), ((), ()))
    n_ex, n_dep = len(row_extras), len(deps)
    assert epilogue is None or nk == 1

    def body(a_ref, b_ref, *rest):
        o_ref = rest[n_ex + n_dep]

        def dot():
            return lax.dot_general(a_ref[...].astype(BF16), b_ref[...].astype(BF16), dims, preferred_element_type=F32)
        if epilogue is not None:
            o_ref[...] = epilogue(dot(), [r[...] for r in rest[:n_ex]], pl.program_id(1)).astype(out_dtype)
        elif nk == 1:
            o_ref[...] = dot().astype(out_dtype)
        else:
            acc_ref = rest[n_ex + n_dep + 1]
            k = pl.program_id(2)

            @pl.when(k == 0)
            def _():
                acc_ref[...] = jnp.zeros_like(acc_ref)
            acc_ref[...] += dot()

            @pl.when(k == nk - 1)
            def _():
                o_ref[...] = acc_ref[...].astype(out_dtype)

    return pl.pallas_call(
        body, name=name, grid=(M // tm, N // tn, nk),
        in_specs=[a_spec, b_spec] + [pl.BlockSpec((tm, e.shape[1]), lambda i, j, k: (i, 0)) for e in row_extras]
        + [pl.BlockSpec(memory_space=pl.ANY)] * n_dep,
        out_specs=pl.BlockSpec((tm, tn), lambda i, j, k: (i, j)),
        out_shape=jax.ShapeDtypeStruct((M, N), out_dtype),
        scratch_shapes=[pltpu.VMEM((tm, tn), F32)] if nk > 1 else [],
        compiler_params=_params("parallel", "parallel", "arbitrary"),
    )(a, b, *row_extras, *deps)


def matmul_swiglu(h, w_gu, dff, *, tm=1024, tn=512):
    M, K = h.shape
    tm, tn = _pick(M, tm, 16), _pick(dff, tn)
    nj = dff // tn

    def body(a_ref, bg_ref, bu_ref, act_ref, gate_ref, up_ref):
        a = a_ref[...]
        gate = jnp.dot(a, bg_ref[...], preferred_element_type=F32)
        up = jnp.dot(a, bu_ref[...], preferred_element_type=F32)
        act_ref[...] = (gate * _sig(gate) * up).astype(BF16)
        gate_ref[...] = gate.astype(BF16)
        up_ref[...] = up.astype(BF16)

    out_spec = pl.BlockSpec((tm, tn), lambda i, j: (i, j))
    return pl.pallas_call(
        body, name="gu_swiglu", grid=(M // tm, nj),
        in_specs=[pl.BlockSpec((tm, K), lambda i, j: (i, 0)), pl.BlockSpec((K, tn), lambda i, j: (0, j)),
                  pl.BlockSpec((K, tn), lambda i, j: (0, nj + j))],
        out_specs=[out_spec] * 3,
        out_shape=[jax.ShapeDtypeStruct((M, dff), BF16)] * 3,
        compiler_params=_params("parallel", "parallel"),
    )(h, w_gu, w_gu)


def rowwise(name, fn, nrows, row_ins, vec_ins, row_outs, n_sums=0, *, W, tr=256, cw=None, rc=ROW_CHUNK):
    cw = cw or W
    tr = _pick(nrows, tr, ROW_CHUNK)
    rc = min(rc, tr)
    assert W % cw == 0 and nrows % ROW_CHUNK == 0 and tr % rc == 0 and rc % 8 == 0
    nrt = nrows // tr
    n_ri, n_v, n_ro = len(row_ins), len(vec_ins), len(row_outs)

    def row_spec(lead, colblk, shift, whole):
        w = cw if whole else W

        def rmap(i):
            return jnp.clip(i + shift, 0, nrt - 1) if shift else i
        if lead is None:
            return pl.BlockSpec((tr, w), lambda i: (rmap(i), colblk))
        return pl.BlockSpec((None, tr, w), lambda i: (lead, rmap(i), colblk))

    in_specs = [row_spec(*ri[1:]) for ri in row_ins]
    in_specs += [pl.BlockSpec((1, W), lambda i: (0, 0)) for _ in vec_ins]
    out_specs = [pl.BlockSpec((tr, p * W), lambda i: (i, 0)) for _, p in row_outs]
    out_specs += [pl.BlockSpec((1, W), lambda i: (0, 0)) for _ in range(n_sums)]
    out_shape = [jax.ShapeDtypeStruct((nrows, p * W), dt) for dt, p in row_outs]
    out_shape += [jax.ShapeDtypeStruct((1, W), F32) for _ in range(n_sums)]

    def body(*refs):
        rin, vin = refs[:n_ri], refs[n_ri:n_ri + n_v]
        rout = refs[n_ri + n_v:n_ri + n_v + n_ro]
        sout = refs[n_ri + n_v + n_ro:n_ri + n_v + n_ro + n_sums]
        acc = refs[n_ri + n_v + n_ro + n_sums:]
        i = pl.program_id(0)
        if n_sums:
            @pl.when(i == 0)
            def _():
                for a in acc:
                    a[...] = jnp.zeros_like(a)
        for c in range(W // cw):
            c0 = c * cw
            vecs = [v[:, c0:c0 + cw] for v in vin]

            def step(r, carry, c=c, c0=c0, vecs=vecs):
                r0 = pl.multiple_of(r * rc, rc)
                rows = [ref[pl.ds(r0, rc), :] if ri[4] else ref[pl.ds(r0, rc), c0:c0 + cw]
                        for ref, ri in zip(rin, row_ins)]
                outs, sums = fn(rows, vecs, c, i)
                for oref, (dt, _), pieces in zip(rout, row_outs, outs):
                    for pi, piece in enumerate(pieces):
                        oref[pl.ds(r0, rc), pi * W + c0:pi * W + c0 + cw] = piece.astype(dt)
                for a, s in zip(acc, sums):
                    part = s[0:8]
                    for q in range(1, rc // 8):
                        part = part + s[8 * q:8 * q + 8]
                    a[:, c0:c0 + cw] += part
                return carry
            lax.fori_loop(0, tr // rc, step, 0)
        if n_sums:
            @pl.when(i == nrt - 1)
            def _():
                for o, a in zip(sout, acc):
                    o[...] = jnp.sum(a[...], axis=0, keepdims=True)

    res = pl.pallas_call(
        body, name=name, grid=(nrt,), in_specs=in_specs, out_specs=out_specs, out_shape=out_shape,
        scratch_shapes=[pltpu.VMEM((8, W), F32) for _ in range(n_sums)],
        compiler_params=_params("arbitrary"),
    )(*[ri[0] for ri in row_ins], *vec_ins)
    return res


def _ri(arr, lead=None, col=0, shift=0, whole=False):
    return (arr, lead, col, shift, whole)


def _sig(x):
    return jax.nn.sigmoid(x)


def _ln_stats(t):
    mu = jnp.mean(t, axis=-1, keepdims=True)
    xc = t - mu
    var = jnp.mean(xc * xc, axis=-1, keepdims=True)
    rstd = lax.rsqrt(var + LN_EPS)
    return xc * rstd, rstd


def _ln_bwd(dy_g, xhat, rstd):
    m1 = jnp.mean(dy_g, axis=-1, keepdims=True)
    m2 = jnp.mean(dy_g * xhat, axis=-1, keepdims=True)
    return rstd * (dy_g - m1 - xhat * m2)


def _rope(x, cos, s_lo, s_hi, sign):
    up = pltpu.roll(x, HEAD_DIM - ROPE_HALF, 1)
    down = pltpu.roll(x, ROPE_HALF, 1)
    return x * cos + sign * (up * s_lo + down * s_hi)


def rope_tables(S):
    pos = jnp.arange(S, dtype=F32)
    inv_freq = ROPE_THETA ** (-jnp.arange(0, ROPE_DIM, 2, dtype=F32) / ROPE_DIM)
    ang = pos[:, None] * inv_freq[None, :]
    cos, sin = jnp.cos(ang), jnp.sin(ang)
    ones = jnp.ones((S, HEAD_DIM - ROPE_DIM), F32)
    zeros = jnp.zeros((S, HEAD_DIM - ROPE_DIM), F32)
    zh = jnp.zeros((S, ROPE_HALF), F32)
    t_cos = jnp.concatenate([cos, cos, ones], axis=1)
    t_lo = jnp.concatenate([-sin, zh, zeros], axis=1)
    t_hi = jnp.concatenate([zh, sin, zeros], axis=1)
    return t_cos, t_lo, t_hi


def _attn_specs(S, D, Dkv):
    nb, nkv, qb = S // BLOCK, Dkv // HEAD_DIM, D // HEAD_DIM
    gw = GQA_GROUP * HEAD_DIM
    q_spec = pl.BlockSpec((BLOCK, gw), lambda h, n: (n, h))

    def band(col0):
        return [pl.BlockSpec((BLOCK, HEAD_DIM), lambda h, n: (jnp.maximum(n - 1, 0), col0 + h)),
                pl.BlockSpec((BLOCK, HEAD_DIM), lambda h, n: (n, col0 + h)),
                pl.BlockSpec((BLOCK, HEAD_DIM), lambda h, n: (jnp.minimum(n + 1, nb - 1), col0 + h))]
    sink_spec = pl.BlockSpec((None, GQA_GROUP * BLOCK, 1), lambda h, n: (h, 0, 0))
    return nb, nkv, gw, q_spec, band(qb), band(qb + nkv), sink_spec


def _attn_probs(q_ref, k_refs, sink_ref, n, S):
    q = q_ref[...]
    qs = jnp.concatenate([q[:, g * HEAD_DIM:(g + 1) * HEAD_DIM] for g in range(GQA_GROUP)], axis=0)
    kb = jnp.concatenate([r[...] for r in k_refs], axis=0)
    s = lax.dot_general(qs, kb, (((1,), (1,)), ((), ())), preferred_element_type=F32) * (HEAD_DIM ** -0.5)
    shape = (GQA_GROUP * BLOCK, 3 * BLOCK)
    row = lax.broadcasted_iota(jnp.int32, shape, 0) & (BLOCK - 1)
    col = lax.broadcasted_iota(jnp.int32, shape, 1)
    rel = col - BLOCK - row
    kpos = (n - 1) * BLOCK + col
    valid = (jnp.abs(rel) <= BLOCK) & (kpos >= 0) & (kpos < S)
    s = jnp.where(valid, s, NEG_INF)
    sink = sink_ref[...]
    m = jnp.maximum(jnp.max(s, axis=-1, keepdims=True), sink)
    p = jnp.exp(s - m)
    e_sink = jnp.exp(sink - m)
    denom = jnp.sum(p, axis=-1, keepdims=True) + e_sink
    return qs, kb, p / denom, e_sink / denom


def attn_fwd(qkv, sinkcol, S, D, Dkv):
    nb, nkv, gw, q_spec, k_specs, v_specs, sink_spec = _attn_specs(S, D, Dkv)

    def body(q_ref, k0, k1, k2, v0, v1, v2, sink_ref, o_ref):
        n = pl.program_id(1)
        _, _, w, _ = _attn_probs(q_ref, (k0, k1, k2), sink_ref, n, S)
        vb = jnp.concatenate([v0[...], v1[...], v2[...]], axis=0)
        o = jnp.dot(w.astype(BF16), vb, preferred_element_type=F32)
        for g in range(GQA_GROUP):
            o_ref[:, g * HEAD_DIM:(g + 1) * HEAD_DIM] = o[g * BLOCK:(g + 1) * BLOCK].astype(BF16)

    return pl.pallas_call(
        body, name="attn_fwd", grid=(nkv, nb),
        in_specs=[q_spec, *k_specs, *v_specs, sink_spec],
        out_specs=pl.BlockSpec((BLOCK, gw), lambda h, n: (n, h)),
        out_shape=jax.ShapeDtypeStruct((S, D), BF16),
        compiler_params=_params("parallel", "arbitrary"),
    )(qkv, qkv, qkv, qkv, qkv, qkv, qkv, sinkcol)


def attn_bwd(qkv, sinkcol, datt, tables, S, D, Dkv):
    nb, nkv, gw, q_spec, k_specs, v_specs, sink_spec = _attn_specs(S, D, Dkv)

    def body(q_ref, k0, k1, k2, v0, v1, v2, sink_ref, do_ref, tc_ref, tl_ref, th_ref, dq_ref, dkp_ref, dvp_ref,
             dsink_ref):
        n = pl.program_id(1)
        qs, kb, w, w_sink = _attn_probs(q_ref, (k0, k1, k2), sink_ref, n, S)
        vb = jnp.concatenate([v0[...], v1[...], v2[...]], axis=0)
        do = do_ref[...]
        dos = jnp.concatenate([do[:, g * HEAD_DIM:(g + 1) * HEAD_DIM] for g in range(GQA_GROUP)], axis=0)
        dv = lax.dot_general(w.astype(BF16), dos, (((0,), (0,)), ((), ())), preferred_element_type=F32)
        dw = lax.dot_general(dos, vb, (((1,), (1,)), ((), ())), preferred_element_type=F32)
        delta = jnp.sum(w * dw, axis=-1, keepdims=True)
        ds = (w * (dw - delta) * (HEAD_DIM ** -0.5)).astype(BF16)
        dq = jnp.dot(ds, kb, preferred_element_type=F32)
        dk = lax.dot_general(ds, qs, (((0,), (0,)), ((), ())), preferred_element_type=F32)
        tc_, tl, th = tc_ref[...], tl_ref[...], th_ref[...]
        for g in range(GQA_GROUP):
            dq_ref[:, g * HEAD_DIM:(g + 1) * HEAD_DIM] = _rope(dq[g * BLOCK:(g + 1) * BLOCK], tc_, tl, th, -1.0).astype(BF16)
        for j in range(3):
            dkp_ref[j] = dk[j * BLOCK:(j + 1) * BLOCK]
            dvp_ref[j] = dv[j * BLOCK:(j + 1) * BLOCK]

        @pl.when(n == 0)
        def _():
            dsink_ref[...] = jnp.zeros_like(dsink_ref)
        t = w_sink * delta
        for g in range(GQA_GROUP):
            dsink_ref[g:g + 1, :] -= jnp.sum(t[g * BLOCK:(g + 1) * BLOCK], axis=0, keepdims=True)

    part_spec = pl.BlockSpec((3, BLOCK, HEAD_DIM), lambda h, n: (0, n, h))
    return pl.pallas_call(
        body, name="attn_bwd", grid=(nkv, nb),
        in_specs=[q_spec, *k_specs, *v_specs, sink_spec, pl.BlockSpec((BLOCK, gw), lambda h, n: (n, h))]
        + [pl.BlockSpec((BLOCK, HEAD_DIM), lambda h, n: (n, 0))] * 3,
        out_specs=[pl.BlockSpec((BLOCK, gw), lambda h, n: (n, h)), part_spec, part_spec,
                   pl.BlockSpec((None, 8, LANE), lambda h, n: (h, 0, 0))],
        out_shape=[jax.ShapeDtypeStruct((S, D), BF16), jax.ShapeDtypeStruct((3, S, Dkv), F32),
                   jax.ShapeDtypeStruct((3, S, Dkv), F32), jax.ShapeDtypeStruct((nkv, 8, LANE), F32)],
        compiler_params=_params("parallel", "arbitrary"),
    )(qkv, qkv, qkv, qkv, qkv, qkv, qkv, sinkcol, datt, *tables)


CONV_TC = 128
CONV_ROWS = 64


def _shift_rows(win, b):
    return win if b == 0 else pltpu.roll(win, win.shape[0] - b, 0)


def _conv_fill_u(ga_ref, gb_ref, upad_ref, S):
    tc = upad_ref.shape[1]
    zero = jnp.zeros((CONV_HALO, tc), F32)
    upad_ref[0:CONV_HALO, :] = zero
    upad_ref[S + CONV_HALO:S + 2 * CONV_HALO, :] = zero

    def fill(r, carry):
        r0 = pl.multiple_of(r * CONV_ROWS, CONV_ROWS)
        upad_ref[pl.ds(r0 + CONV_HALO, CONV_ROWS), :] = ga_ref[pl.ds(r0, CONV_ROWS), :] * _sig(gb_ref[pl.ds(r0, CONV_ROWS), :])
        return carry
    lax.fori_loop(0, S // CONV_ROWS, fill, 0)


def conv_fwd(pglu, wdw, S, D):
    tc = min(CONV_TC, D)
    nct = D // tc

    def body(ga_ref, gb_ref, w_ref, o_ref, upad_ref):
        _conv_fill_u(ga_ref, gb_ref, upad_ref, S)
        w = w_ref[...]

        def step(r, carry):
            r0 = pl.multiple_of(r * CONV_ROWS, CONV_ROWS)
            win = upad_ref[pl.ds(r0, CONV_ROWS + 2 * CONV_HALO), :]
            acc = jnp.zeros((CONV_ROWS, tc), F32)
            for b in range(8):
                shifted = _shift_rows(win, b)
                for k in range(CONV_WIDTH):
                    o = k + CONV_HALO - CONV_WIDTH // 2
                    if o % 8 == b:
                        acc = acc + w[k:k + 1, :] * shifted[o - b:o - b + CONV_ROWS, :]
            o_ref[pl.ds(r0, CONV_ROWS), :] = acc
            return carry
        lax.fori_loop(0, S // CONV_ROWS, step, 0)

    return pl.pallas_call(
        body, name="conv_fwd", grid=(nct,),
        in_specs=[pl.BlockSpec((S, tc), lambda j: (0, j)), pl.BlockSpec((S, tc), lambda j: (0, nct + j)),
                  pl.BlockSpec((CONV_WIDTH, tc), lambda j: (0, j))],
        out_specs=pl.BlockSpec((S, tc), lambda j: (0, j)),
        out_shape=jax.ShapeDtypeStruct((S, D), F32),
        scratch_shapes=[pltpu.VMEM((S + 2 * CONV_HALO, tc), F32)],
        compiler_params=_params("parallel"),
    )(pglu, pglu, wdw)


def conv_bwd(duc, pglu, wdw, S, D):
    tc = min(CONV_TC, D)
    nct = D // tc
    half = CONV_WIDTH // 2

    def body(d_ref, ga_ref, gb_ref, w_ref, dga_ref, dgb_ref, dw_ref, upad_ref, dpad_ref, dwacc_ref):
        _conv_fill_u(ga_ref, gb_ref, upad_ref, S)
        zero = jnp.zeros((CONV_HALO, tc), F32)
        dpad_ref[0:CONV_HALO, :] = zero
        dpad_ref[S + CONV_HALO:S + 2 * CONV_HALO, :] = zero

        def fill(r, carry):
            r0 = pl.multiple_of(r * CONV_ROWS, CONV_ROWS)
            dpad_ref[pl.ds(r0 + CONV_HALO, CONV_ROWS), :] = d_ref[pl.ds(r0, CONV_ROWS), :]
            return carry
        lax.fori_loop(0, S // CONV_ROWS, fill, 0)
        dwacc_ref[...] = jnp.zeros_like(dwacc_ref)
        w = w_ref[...]

        def step(r, carry):
            r0 = pl.multiple_of(r * CONV_ROWS, CONV_ROWS)
            uwin = upad_ref[pl.ds(r0, CONV_ROWS + 2 * CONV_HALO), :]
            dwin = dpad_ref[pl.ds(r0, CONV_ROWS + 2 * CONV_HALO), :]
            d = dwin[CONV_HALO:CONV_HALO + CONV_ROWS, :]
            du = jnp.zeros((CONV_ROWS, tc), F32)
            for b in range(8):
                d_shifted, u_shifted = _shift_rows(dwin, b), _shift_rows(uwin, b)
                for k in range(CONV_WIDTH):
                    o = CONV_HALO + half - k
                    if o % 8 == b:
                        du = du + w[k:k + 1, :] * d_shifted[o - b:o - b + CONV_ROWS, :]
                    o = CONV_HALO + k - half
                    if o % 8 == b:
                        prod = d * u_shifted[o - b:o - b + CONV_ROWS, :]
                        part = prod[0:8]
                        for q in range(1, CONV_ROWS // 8):
                            part = part + prod[8 * q:8 * q + 8]
                        dwacc_ref[k] += part
            ga = ga_ref[pl.ds(r0, CONV_ROWS), :]
            sg = _sig(gb_ref[pl.ds(r0, CONV_ROWS), :])
            dga_ref[pl.ds(r0, CONV_ROWS), :] = (du * sg).astype(BF16)
            dgb_ref[pl.ds(r0, CONV_ROWS), :] = (du * ga * sg * (1.0 - sg)).astype(BF16)
            return carry
        lax.fori_loop(0, S // CONV_ROWS, step, 0)
        dw_ref[...] = jnp.sum(dwacc_ref[...], axis=1)

    return pl.pallas_call(
        body, name="conv_bwd", grid=(nct,),
        in_specs=[pl.BlockSpec((S, tc), lambda j: (0, j)), pl.BlockSpec((S, tc), lambda j: (0, j)),
                  pl.BlockSpec((S, tc), lambda j: (0, nct + j)), pl.BlockSpec((CONV_WIDTH, tc), lambda j: (0, j))],
        out_specs=[pl.BlockSpec((S, tc), lambda j: (0, j)), pl.BlockSpec((S, tc), lambda j: (0, j)),
                   pl.BlockSpec((32, tc), lambda j: (0, j))],
        out_shape=[jax.ShapeDtypeStruct((S, D), BF16), jax.ShapeDtypeStruct((S, D), BF16),
                   jax.ShapeDtypeStruct((32, D), F32)],
        scratch_shapes=[pltpu.VMEM((S + 2 * CONV_HALO, tc), F32), pltpu.VMEM((S + 2 * CONV_HALO, tc), F32),
                        pltpu.VMEM((32, 8, tc), F32)],
        compiler_params=_params("parallel"),
    )(duc, pglu, pglu, wdw)


def exchange(name, ops):
    n = len(ops)

    def body(*refs):
        xs, outs = refs[:n], refs[n:2 * n]
        send, recv, lsem = refs[2 * n:]
        mx, my, mc = lax.axis_index("x"), lax.axis_index("y"), lax.axis_index("c")
        me = 4 * mx + 2 * my + mc
        local = [pltpu.make_async_copy(op[3](xs[i], me), op[4](outs[i], me), lsem.at[i]) for i, op in enumerate(ops)]
        for cp in local:
            cp.start()
        copies = []
        for k in range(1, N_DEV):
            px = 1 - mx if k & 4 else mx
            py = 1 - my if k & 2 else my
            pc = 1 - mc if k & 1 else mc
            peer = 4 * px + 2 * py + pc
            for i, op in enumerate(ops):
                cp = pltpu.make_async_remote_copy(
                    src_ref=op[3](xs[i], peer), dst_ref=op[4](outs[i], me),
                    send_sem=send.at[i, k - 1], recv_sem=recv.at[i, k - 1],
                    device_id=(px, py, pc), device_id_type=pl.DeviceIdType.MESH)
                cp.start()
                copies.append(cp)
        for cp in copies:
            cp.wait()
        for cp in local:
            cp.wait()

    any_spec = pl.BlockSpec(memory_space=pl.ANY)
    return pl.pallas_call(
        body, name=name,
        in_specs=[any_spec] * n, out_specs=[any_spec] * n,
        out_shape=[jax.ShapeDtypeStruct(op[1], op[2]) for op in ops],
        scratch_shapes=[pltpu.SemaphoreType.DMA((n, N_DEV - 1)), pltpu.SemaphoreType.DMA((n, N_DEV - 1)),
                        pltpu.SemaphoreType.DMA((n,))],
        compiler_params=pltpu.CompilerParams(has_side_effects=True),
    )(*[op[0] for op in ops])


_HBM = pl.BlockSpec(memory_space=pltpu.HBM)
_SEM = pl.BlockSpec(memory_space=pltpu.SEMAPHORE)
_EFFECT = pltpu.SideEffectType.DATAFLOW_SIDE_EFFECTING


def _me_and_peers():
    mx, my, mc = lax.axis_index("x"), lax.axis_index("y"), lax.axis_index("c")
    peers = []
    for k in range(1, N_DEV):
        px = 1 - mx if k & 4 else mx
        py = 1 - my if k & 2 else my
        pc = 1 - mc if k & 1 else mc
        peers.append((k, (px, py, pc), 4 * px + 2 * py + pc))
    return 4 * mx + 2 * my + mc, peers


PLACE_STEPS = 4


def exchange_local(name, ops, me_arr, deps=()):
    n = len(ops)
    in_specs, out_specs = [], []
    for op in ops:
        kind, axis = op[5]
        rows, cols = op[0].shape if kind == "gather" else op[1][1:]
        steps = PLACE_STEPS if rows % (PLACE_STEPS * ROW_CHUNK) == 0 else 1
        tr = rows // steps

        def row(i, steps=steps):
            return i if steps > 1 else 0

        def window(i, me, axis=axis, steps=steps):
            return (me[0] * steps + row(i, steps), 0) if axis == 0 else (row(i, steps), me[0])
        if kind == "gather":
            in_specs.append(pl.BlockSpec((tr, cols), lambda i, me, row=row: (row(i), 0)))
            out_specs.append(pl.BlockSpec((tr, cols), window))
        else:
            in_specs.append(pl.BlockSpec((tr, cols), window))
            out_specs.append(pl.BlockSpec((None, tr, cols), lambda i, me, row=row: (me[0], row(i), 0)))

    def body(me_ref, *refs):
        for i in range(n):
            refs[n + len(deps) + i][...] = refs[i][...]

    return pl.pallas_call(
        body, name=name,
        grid_spec=pltpu.PrefetchScalarGridSpec(
            num_scalar_prefetch=1, grid=(PLACE_STEPS,),
            in_specs=in_specs + [pl.BlockSpec(memory_space=pl.ANY)] * len(deps), out_specs=out_specs),
        out_shape=[jax.ShapeDtypeStruct(op[1], op[2]) for op in ops],
        compiler_params=_params("arbitrary"),
    )(me_arr, *[op[0] for op in ops], *deps)


ALL_PEERS = (1, 2, 3, 4, 5, 6, 7)
SIBLING = 1
SAME_CORE = (2, 4, 6)


def _flipped(pos, k):
    p = (1 - pos[0] if k & 4 else pos[0], 1 - pos[1] if k & 2 else pos[1], 1 - pos[2] if k & 1 else pos[2])
    return p, 4 * p[0] + 2 * p[1] + p[2]


def direct_copies(ops, ks):
    n = len(ops)

    def build(refs, send, recv):
        xs, lands = refs[:n], refs[n:2 * n]
        pos = (lax.axis_index("x"), lax.axis_index("y"), lax.axis_index("c"))
        _, me = _flipped(pos, 0)
        copies = []
        for j, k in enumerate(ks):
            peer_id, peer = _flipped(pos, k)
            for i, op in enumerate(ops):
                s = i * len(ks) + j
                copies.append(pltpu.make_async_remote_copy(
                    src_ref=op[3](xs[i], peer), dst_ref=op[4](lands[i], me), send_sem=send.at[s], recv_sem=recv.at[s],
                    device_id=peer_id, device_id_type=pl.DeviceIdType.MESH))
        return copies
    return n * len(ks), build


def forward_copies(ops, ks):
    n = len(ops)

    def build(refs, send, recv):
        lands = refs[:n]
        pos = (lax.axis_index("x"), lax.axis_index("y"), lax.axis_index("c"))
        sibling_id, _ = _flipped(pos, SIBLING)
        copies = []
        for j, k in enumerate(ks):
            _, origin = _flipped(pos, k)
            for i, op in enumerate(ops):
                s = i * len(ks) + j
                region = op[4](lands[i], origin)
                copies.append(pltpu.make_async_remote_copy(
                    src_ref=region, dst_ref=region, send_sem=send.at[s], recv_sem=recv.at[s],
                    device_id=sibling_id, device_id_type=pl.DeviceIdType.MESH))
        return copies
    return n * len(ks), build


def exchange_start(name, arrays, plan):
    n_sems, build = plan
    n = len(arrays)

    def body(*refs):
        for cp in build(refs[:n], refs[n], refs[n + 1]):
            cp.start()
        refs[-1][...] = jnp.zeros_like(refs[-1])

    args = [pltpu.with_memory_space_constraint(a, pltpu.HBM) for a in arrays]
    sems = pltpu.SemaphoreType.DMA((n_sems,))
    outs = pl.pallas_call(
        body, name=name,
        out_shape=(sems, sems, *[pltpu.HBM(a.shape, a.dtype) for a in args], jax.ShapeDtypeStruct((8, LANE), F32)),
        in_specs=[_HBM] * n,
        out_specs=(_SEM, _SEM, *[_HBM] * n, pl.BlockSpec(memory_space=pltpu.VMEM)),
        input_output_aliases={i: 2 + i for i in range(n)},
        compiler_params=pltpu.CompilerParams(has_side_effects=_EFFECT),
    )(*args)
    return (build, outs[0], outs[1], outs[2:2 + n]), outs[-1]


def exchange_wait(name, handle, after):
    build, send, recv, thru = handle
    n = len(thru)

    def body(*refs):
        for cp in build(refs[:n], refs[n], refs[n + 1]):
            cp.wait_send()
            cp.wait_recv()

    return pl.pallas_call(
        body, name=name,
        out_shape=[pltpu.HBM(a.shape, a.dtype) for a in thru],
        in_specs=[_HBM] * n + [_SEM, _SEM, pl.BlockSpec(memory_space=pl.ANY)],
        out_specs=[_HBM] * n,
        input_output_aliases={i: i for i in range(n)},
        compiler_params=pltpu.CompilerParams(has_side_effects=_EFFECT),
    )(*thru, send, recv, after)


def _whole(ref, q):
    return ref


def _slot(ref, q):
    return ref.at[q]


def op_gather_stack(x):
    return (x, (N_DEV,) + x.shape, x.dtype, _whole, _slot)


def op_gather_axis(x, axis):
    size = x.shape[axis]
    shape = x.shape[:axis] + (N_DEV * size,) + x.shape[axis + 1:]

    def dst(ref, q):
        idx = [slice(None)] * len(shape)
        idx[axis] = pl.ds(pl.multiple_of(q * size, size), size)
        return ref.at[tuple(idx)]
    return (x, shape, x.dtype, _whole, dst, ("gather", axis))


def op_scatter_axis(x, axis):
    size = x.shape[axis] // N_DEV
    shape = x.shape[:axis] + (size,) + x.shape[axis + 1:]

    def src(ref, q):
        idx = [slice(None)] * len(shape)
        idx[axis] = pl.ds(pl.multiple_of(q * size, size), size)
        return ref.at[tuple(idx)]
    return (x, (N_DEV,) + shape, x.dtype, src, _slot, ("scatter", axis))


def _adamw(w, g, m, v):
    m = ADAM_B1 * m + (1.0 - ADAM_B1) * g
    v = ADAM_B2 * v + (1.0 - ADAM_B2) * (g * g)
    m_hat = m / (1.0 - ADAM_B1 ** ADAM_STEP)
    v_hat = v / (1.0 - ADAM_B2 ** ADAM_STEP)
    delta = -ADAM_LR * (m_hat / (jnp.sqrt(v_hat) + ADAM_EPS) + ADAM_WD * w)
    return delta, m, v


def adamw_update(name, w, m, v, *, layer=None, landed=None, grad=None):
    R, W = w.shape[-2:]
    n_g = N_DEV if landed is not None else 1

    def fn(rows, vecs, c, i):
        g = rows[0].astype(F32)
        for q in range(1, n_g):
            g = g + rows[q].astype(F32)
        wv, mv, vv = rows[n_g:]
        delta, m2, v2 = _adamw(wv, g, mv, vv)
        return [[g], [delta], [m2], [v2]], []

    g_ins = [_ri(landed, lead=q) for q in range(N_DEV)] if landed is not None else [_ri(grad)]
    cw = LANE if W % LANE == 0 else W
    return rowwise(name, fn, R, g_ins + [_ri(w, lead=layer), _ri(m, lead=layer), _ri(v, lead=layer)], [],
                   [(F32, 1)] * 4, W=W, tr=128, cw=cw, rc=32)


def kernel(x, c, w_ada, b_ada, w_in, sink, w_dw, conv_ln_g, conv_ln_b, w_oa, w_ob, w_out, ln1_g, ln1_b, w_gu, w_down, ln2_g, ln2_b, loss_target, m_w_ada, m_b_ada, m_w_in, m_sink, m_w_dw, m_conv_ln_g, m_conv_ln_b, m_w_oa, m_w_ob, m_w_out, m_ln1_g, m_ln1_b, m_w_gu, m_w_down, m_ln2_g, m_ln2_b, v_w_ada, v_b_ada, v_w_in, v_sink, v_w_dw, v_conv_ln_g, v_conv_ln_b, v_w_oa, v_w_ob, v_w_out, v_ln1_g, v_ln1_b, v_w_gu, v_w_down, v_ln2_g, v_ln2_b):
    L = w_ada.shape[0]
    S, D = x.shape[1], x.shape[2]
    Dkv = D // GQA_GROUP
    Dqkv = D + 2 * Dkv
    DFF = w_down.shape[1] * N_DEV
    nq, nkv, nb = D // HEAD_DIM, Dkv // HEAD_DIM, S // BLOCK
    alpha = (2.0 * L) ** 0.25
    me = 4 * lax.axis_index("x") + 2 * lax.axis_index("y") + lax.axis_index("c")
    x0 = x.reshape(S, D)
    target = loss_target.reshape(S, D)
    t_cos, t_lo, t_hi = rope_tables(S)

    c_act = jax.nn.silu(c)
    c_all = exchange("gather_c", [op_gather_stack(c_act)])[0].reshape(N_DEV, D)
    c_pad = jnp.concatenate([c_all, jnp.zeros_like(c_all)], axis=0).astype(BF16)
    ada_cols = w_ada.shape[2]
    mod_part = jnp.stack([matmul("mod_mm", c_pad, w_ada[l], "nn", F32)[:N_DEV] for l in range(L)], axis=1)
    mod_land = exchange("scatter_mod", [op_scatter_axis(mod_part, 0)])[0]
    mod = jnp.transpose(mod_land.reshape(N_DEV, L, ada_cols), (1, 0, 2)).reshape(L, N_MOD * D) + b_ada
    mods = [[mod[l:l + 1, j * D:(j + 1) * D] for j in range(N_MOD)] for l in range(L)]

    me_arr = me.astype(jnp.int32).reshape(1)

    def begin(tag, ops, dep, ks=ALL_PEERS):
        lands = exchange_local(tag + "_local", ops, me_arr, [dep])
        return exchange_start(tag + "_start", [op[0] for op in ops] + list(lands), direct_copies(ops, ks))

    def gather_begin(tag, ops, dep):
        return begin(tag, ops, dep, (SIBLING,) + SAME_CORE)

    def gather_forward(tag, ops, handle, after):
        lands = exchange_wait(tag + "_wait", handle, after)[len(ops):]
        return exchange_start(tag + "_fwd_start", list(lands), forward_copies(ops, SAME_CORE))

    def gather_end(tag, handle, after):
        return exchange_wait(tag + "_fwd_wait", handle, after)

    def gather_first(l):
        return [op_gather_axis(w_in[l].astype(BF16), 1), op_gather_axis(w_dw[l], 1)]

    def gather_rest(l):
        return [op_gather_axis(w_gu[l].astype(BF16), 1), op_gather_axis(w_oa[l].astype(BF16), 0),
                op_gather_axis(w_ob[l].astype(BF16), 0), op_gather_axis(w_out[l].astype(BF16), 0),
                op_gather_axis(w_down[l].astype(BF16), 0)]

    W_in, W_gu, W_oa, W_ob, W_out, W_down, W_dw = ([None] * L for _ in range(7))
    first_ops = gather_first(0)
    ag_first, _ = gather_begin("ag0a", first_ops, mod_land)

    def vec(a, l):
        return a[l:l + 1]

    def f_mod(rows, vecs, c_, i_):
        (xv,), (sc, sh) = rows, vecs
        return [[xv * (1.0 + sc) + sh]], []

    def f_convln(rows, vecs, c_, i_):
        (uc,), (g, b) = rows, vecs
        xhat, _ = _ln_stats(uc)
        nrm = xhat * g + b
        return [[nrm * _sig(nrm)]], []

    def f_merge(rows, vecs, c_, i_):
        g_a, g_b, y_a, y_b = rows
        return [[_sig(g_a) * y_a + _sig(g_b) * y_b]], []

    def f_ln(rows, vecs, c_, i_):
        (xv, r), (gt, g, b, sc, sh) = rows, vecs
        xhat, _ = _ln_stats(alpha * xv + (1.0 + gt) * r)
        y = xhat * g + b
        return [[y], [y * (1.0 + sc) + sh]], []

    saved = []
    xl = x0
    h = rowwise("modulate", f_mod, S, [_ri(x0)], [mods[0][1], mods[0][0]], [(BF16, 1)], W=D)[0]
    ag_first, _ = gather_forward("ag0a", first_ops, ag_first, h)
    W_in[0], W_dw[0] = gather_end("ag0a", ag_first, h)
    ag_next, next_ops = None, None

    def after_token(token):
        return [token] if token is not None else []

    def rope_heads(tile, extras, j):
        tc_, tl, th = extras
        heads = []
        for hh in range(tile.shape[1] // HEAD_DIM):
            xh = tile[:, hh * HEAD_DIM:(hh + 1) * HEAD_DIM]
            is_qk = j * (tile.shape[1] // HEAD_DIM) + hh < nq + nkv
            heads.append(jnp.where(is_qk, _rope(xh, tc_, tl, th, 1.0), xh))
        return jnp.concatenate(heads, axis=1)

    for l in range(L):
        sh_a, sc_a, gt_a, sh_f, sc_f, gt_f = mods[l]
        if l == 0:
            rest_ops = gather_rest(0)
            ag_rest, token = gather_begin("ag0b", rest_ops, W_in[0])
        elif l + 1 < L:
            next_ops = gather_first(l + 1) + gather_rest(l + 1)
            ag_next, token = gather_begin("ag%d" % (l + 1), next_ops, W_in[l])
        else:
            token = None
        qkv = matmul("in_qkv", h, W_in[l], "nn", BF16, n=Dqkv, deps=after_token(token),
                     epilogue=rope_heads, row_extras=[t_cos, t_lo, t_hi])
        p_glu = matmul("in_glu", h, W_in[l], "nn", F32, b_off=Dqkv, n=2 * D)
        p_gate = matmul("in_gate", h, W_in[l], "nn", F32, b_off=Dqkv + 2 * D, n=2 * D)
        sinkcol = jnp.repeat(sink[l].reshape(nkv, GQA_GROUP), BLOCK, axis=1).reshape(nkv, GQA_GROUP * BLOCK, 1)
        att = attn_fwd(qkv, sinkcol, S, D, Dkv)
        uc = conv_fwd(p_glu, W_dw[l], S, D)
        token = None
        if l == 0:
            ag_rest, _ = gather_forward("ag0b", rest_ops, ag_rest, uc)
            W_gu[0], W_oa[0], W_ob[0], W_out[0], W_down[0] = gather_end("ag0b", ag_rest, uc)
            if L > 1:
                next_ops = gather_first(1) + gather_rest(1)
                ag_next, token = gather_begin("ag1", next_ops, W_gu[0])
        y_a = matmul("oa", att, W_oa[l], "nn", F32, deps=after_token(token))
        z = rowwise("conv_ln", f_convln, S, [_ri(uc)], [vec(conv_ln_g, l), vec(conv_ln_b, l)], [(BF16, 1)], W=D)[0]
        y_b = matmul("ob", z, W_ob[l], "nn", F32)
        merged = rowwise("merge", f_merge, S, [_ri(p_gate, col=0), _ri(p_gate, col=1), _ri(y_a), _ri(y_b)], [],
                         [(BF16, 1)], W=D, cw=_pick(D, 512))[0]
        r1 = matmul("out", merged, W_out[l], "nn", F32)
        x1, h2 = rowwise("ln1", f_ln, S, [_ri(xl), _ri(r1)], [gt_a, vec(ln1_g, l), vec(ln1_b, l), sc_f, sh_f],
                         [(F32, 1), (BF16, 1)], W=D)
        act, gate, up = matmul_swiglu(h2, W_gu[l], DFF)
        token = None
        if l + 1 < L:
            ag_next, token = gather_forward("ag%d" % (l + 1), next_ops, ag_next, act)
        f = matmul("down", act, W_down[l], "nn", F32, deps=after_token(token))
        nsc, nsh = (mods[l + 1][1], mods[l + 1][0]) if l + 1 < L else (sc_a, sh_a)
        x2, h_next = rowwise("ln2", f_ln, S, [_ri(x1), _ri(f)], [gt_f, vec(ln2_g, l), vec(ln2_b, l), nsc, nsh],
                             [(F32, 1), (BF16, 1)], W=D)
        saved.append(dict(x=xl, h=h, qkv=qkv, sinkcol=sinkcol, att=att, p_glu=p_glu, p_gate=p_gate, y_a=y_a, y_b=y_b,
                          uc=uc, z=z, merged=merged, r1=r1, x1=x1, h2=h2, gate=gate, up=up, act=act, f=f))
        xl, h = x2, h_next
        if l + 1 < L:
            (W_in[l + 1], W_dw[l + 1], W_gu[l + 1], W_oa[l + 1], W_ob[l + 1], W_out[l + 1],
             W_down[l + 1]) = gather_end("ag%d" % (l + 1), ag_next, x2)

    def f_loss(rows, vecs, c_, i_):
        y, t = rows
        e = y - t
        return [[e * (1.0 / D)]], [e * e]

    dy, err = rowwise("loss", f_loss, S, [_ri(xl), _ri(target)], [], [(F32, 1)], 1, W=D)
    loss = lax.psum(0.5 * jnp.sum(err) / D, AXES)

    def f_ln_bwd_last(rows, vecs, c_, i_):
        (dout, xin, r), (gt, g) = rows, vecs
        xhat, rstd = _ln_stats(alpha * xin + (1.0 + gt) * r)
        dt = _ln_bwd(dout * g, xhat, rstd)
        return [[(1.0 + gt) * dt], [alpha * dt]], [dout * xhat, dout, dt * r]

    def f_ln_bwd(rows, vecs, c_, i_):
        (dres, dh, xout, xin, r), (sc, gt, g) = rows, vecs
        dout = dres + dh * (1.0 + sc)
        xhat, rstd = _ln_stats(alpha * xin + (1.0 + gt) * r)
        dt = _ln_bwd(dout * g, xhat, rstd)
        return [[(1.0 + gt) * dt], [alpha * dt]], [dout * xhat, dout, dt * r, dh * xout, dh]

    def f_swiglu_bwd(rows, vecs, c_, i_):
        da, gate, up = rows[0], rows[1].astype(F32), rows[2].astype(F32)
        sg = _sig(gate)
        return [[da * up * sg * (1.0 + gate * (1.0 - sg)), da * (gate * sg)]], []

    def f_gate_bwd(rows, vecs, c_, i_):
        dm, g_a, g_b, y_a, y_b = rows
        sa, sb = _sig(g_a), _sig(g_b)
        return [[dm * sa], [dm * sb], [dm * y_a * sa * (1.0 - sa), dm * y_b * sb * (1.0 - sb)]], []

    def f_convln_bwd(rows, vecs, c_, i_):
        (dz, uc), (g, b) = rows, vecs
        xhat, rstd = _ln_stats(uc)
        nrm = xhat * g + b
        sg = _sig(nrm)
        dn = dz * sg * (1.0 + nrm * (1.0 - sg))
        return [[_ln_bwd(dn * g, xhat, rstd)]], [dn * xhat, dn]

    def f_kv_combine(rows, vecs, c_, i_):
        k_lo, k_mid, k_hi, v_lo, v_mid, v_hi, tc_, tl, th = rows
        lo, hi = i_ > 0, i_ < nb - 1
        dk = jnp.where(lo, k_lo, 0.0) + k_mid + jnp.where(hi, k_hi, 0.0)
        dv = jnp.where(lo, v_lo, 0.0) + v_mid + jnp.where(hi, v_hi, 0.0)
        return [[_rope(dk, tc_, tl, th, -1.0)], [dv]], []

    def f_mod_bwd(rows, vecs, c_, i_):
        (dres, dh, xv), (sc,) = rows, vecs
        return [[dres + dh * (1.0 + sc)]], [dh * xv, dh]

    def flat(a):
        return a.reshape(-1, a.shape[-1])

    big_names = ("w_in", "w_gu", "w_oa", "w_ob", "w_out", "w_down")
    big_w = dict(w_in=w_in, w_gu=w_gu, w_oa=w_oa, w_ob=w_ob, w_out=w_out, w_down=w_down)
    big_m = dict(w_in=m_w_in, w_gu=m_w_gu, w_oa=m_w_oa, w_ob=m_w_ob, w_out=m_w_out, w_down=m_w_down)
    big_v = dict(w_in=v_w_in, w_gu=v_w_gu, w_oa=v_w_oa, w_ob=v_w_ob, w_out=v_w_out, w_down=v_w_down)
    big_axis = dict(w_in=1, w_gu=1, w_oa=0, w_ob=0, w_out=0, w_down=0)
    big_res = {nm: [None] * L for nm in big_names}
    dmod = [None] * L
    small = dict(sink=[None] * L, conv_ln_g=[None] * L, conv_ln_b=[None] * L, ln1_g=[None] * L, ln1_b=[None] * L,
                 ln2_g=[None] * L, ln2_b=[None] * L)
    dwdw = [None] * L

    early, late = ("w_down", "w_gu"), ("w_out", "w_oa", "w_ob", "w_in")

    def finish(names, tag, handle, after, l):
        landed = exchange_wait(tag + "_wait", handle, after)[len(names):]
        for nm, land in zip(names, landed):
            big_res[nm][l] = adamw_update("adamw_" + nm, big_w[nm], big_m[nm], big_v[nm], layer=l, landed=land)
        return landed[0]

    dres, dh_next = dy, None
    rs_late, token = None, None
    for l in reversed(range(L)):
        sv = saved[l]
        sh_a, sc_a, gt_a, sh_f, sc_f, gt_f = mods[l]
        if dh_next is None:
            df, dres, d_g2, d_b2, d_gtf = rowwise(
                "ln2_bwd_last", f_ln_bwd_last, S, [_ri(dres), _ri(sv["x1"]), _ri(sv["f"])], [gt_f, vec(ln2_g, l)],
                [(BF16, 1), (F32, 1)], 3, W=D)
            d_sca_next = d_sha_next = None
        else:
            df, dres, d_g2, d_b2, d_gtf, d_sca_next, d_sha_next = rowwise(
                "ln2_bwd", f_ln_bwd, S, [_ri(dres), _ri(dh_next), _ri(saved[l + 1]["x"]), _ri(sv["x1"]), _ri(sv["f"])],
                [mods[l + 1][1], gt_f, vec(ln2_g, l)], [(BF16, 1), (F32, 1)], 5, W=D)
            dmod[l + 1][1], dmod[l + 1][0] = d_sca_next, d_sha_next
        dmod[l] = [None] * N_MOD
        dmod[l][5] = d_gtf
        small["ln2_g"][l], small["ln2_b"][l] = d_g2, d_b2
        g_down = matmul("d_w_down", sv["act"], df, "tn", BF16, deps=after_token(token))
        dact = matmul("d_act", df, W_down[l], "nt", F32)
        dgu = rowwise("swiglu_bwd", f_swiglu_bwd, S, [_ri(dact), _ri(sv["gate"]), _ri(sv["up"])], [],
                      [(BF16, 2)], W=DFF, tr=128, cw=_pick(DFF, 512))[0]
        g_gu = matmul("d_w_gu", sv["h2"], dgu, "tn", BF16)
        dh2 = matmul("d_h2", dgu, W_gu[l], "nt", F32)
        dr1, dres, d_g1, d_b1, d_gta, d_scf, d_shf = rowwise(
            "ln1_bwd", f_ln_bwd, S, [_ri(dres), _ri(dh2), _ri(sv["x1"]), _ri(sv["x"]), _ri(sv["r1"])],
            [sc_f, gt_a, vec(ln1_g, l)], [(BF16, 1), (F32, 1)], 5, W=D)
        dmod[l][2], dmod[l][4], dmod[l][3] = d_gta, d_scf, d_shf
        small["ln1_g"][l], small["ln1_b"][l] = d_g1, d_b1
        g_out = matmul("d_w_out", sv["merged"], dr1, "tn", BF16)
        dmerged = matmul("d_merged", dr1, W_out[l], "nt", F32)
        before = finish(late, "rsb%d" % (l + 1), rs_late, dmerged, l + 1) if rs_late is not None else g_gu
        rs_early, token = begin("rsa%d" % l, [op_scatter_axis(g_down, 0), op_scatter_axis(g_gu, 1)], before)
        dy_a, dy_b, dp_gate = rowwise(
            "gate_bwd", f_gate_bwd, S,
            [_ri(dmerged), _ri(sv["p_gate"], col=0), _ri(sv["p_gate"], col=1), _ri(sv["y_a"]), _ri(sv["y_b"])], [],
            [(BF16, 1), (BF16, 1), (BF16, 2)], W=D, cw=_pick(D, 512))
        g_oa = matmul("d_w_oa", sv["att"], dy_a, "tn", BF16, deps=[token])
        datt = matmul("d_att", dy_a, W_oa[l], "nt", BF16)
        g_ob = matmul("d_w_ob", sv["z"], dy_b, "tn", BF16)
        dz = matmul("d_z", dy_b, W_ob[l], "nt", F32)
        duc, d_cg, d_cb = rowwise("conv_ln_bwd", f_convln_bwd, S, [_ri(dz), _ri(sv["uc"])],
                                  [vec(conv_ln_g, l), vec(conv_ln_b, l)], [(F32, 1)], 2, W=D)
        small["conv_ln_g"][l], small["conv_ln_b"][l] = d_cg, d_cb
        dga, dgb, dwdw[l] = conv_bwd(duc, sv["p_glu"], W_dw[l], S, D)
        dq_r, dkp, dvp, dsink = attn_bwd(sv["qkv"], sv["sinkcol"], datt, (t_cos, t_lo, t_hi), S, D, Dkv)
        small["sink"][l] = dsink[:, :GQA_GROUP, 0].reshape(1, nq)
        dk_r, dv_r = rowwise(
            "kv_combine", f_kv_combine, S,
            [_ri(dkp, lead=2, shift=-1), _ri(dkp, lead=1), _ri(dkp, lead=0, shift=1),
             _ri(dvp, lead=2, shift=-1), _ri(dvp, lead=1), _ri(dvp, lead=0, shift=1),
             _ri(t_cos, whole=True), _ri(t_lo, whole=True), _ri(t_hi, whole=True)],
            [], [(BF16, 1), (BF16, 1)], W=Dkv, tr=BLOCK, cw=HEAD_DIM, rc=32)
        dp = jnp.concatenate([dq_r, dk_r, dv_r, dga, dgb, dp_gate], axis=1)
        g_in = matmul("d_w_in", sv["h"], dp, "tn", BF16)
        dh_next = matmul("d_h", dp, W_in[l], "nt", F32)

        before = finish(early, "rsa%d" % l, rs_early, dh_next, l)
        rs_late, token = begin("rsb%d" % l, [op_scatter_axis(g_out, 0), op_scatter_axis(g_oa, 0),
                                             op_scatter_axis(g_ob, 0), op_scatter_axis(g_in, 1)], before)

    grad_x, d_sca0, d_sha0 = rowwise("mod_bwd", f_mod_bwd, S, [_ri(dres), _ri(dh_next), _ri(x0)], [mods[0][1]],
                                     [(F32, 1)], 2, W=D)
    dmod[0][1], dmod[0][0] = d_sca0, d_sha0
    finish(late, "rsb0", rs_late, grad_x, 0)

    small_names = ("b_ada", "sink", "conv_ln_g", "conv_ln_b", "ln1_g", "ln1_b", "ln2_g", "ln2_b")
    small_w = dict(b_ada=b_ada, sink=sink, conv_ln_g=conv_ln_g, conv_ln_b=conv_ln_b, ln1_g=ln1_g, ln1_b=ln1_b,
                   ln2_g=ln2_g, ln2_b=ln2_b)
    small_m = dict(b_ada=m_b_ada, sink=m_sink, conv_ln_g=m_conv_ln_g, conv_ln_b=m_conv_ln_b, ln1_g=m_ln1_g,
                   ln1_b=m_ln1_b, ln2_g=m_ln2_g, ln2_b=m_ln2_b)
    small_v = dict(b_ada=v_b_ada, sink=v_sink, conv_ln_g=v_conv_ln_g, conv_ln_b=v_conv_ln_b, ln1_g=v_ln1_g,
                   ln1_b=v_ln1_b, ln2_g=v_ln2_g, ln2_b=v_ln2_b)
    small_g = dict(small)
    small_g["b_ada"] = [jnp.concatenate(dmod[l], axis=1) for l in range(L)]
    sizes = [small_w[nm].size for nm in small_names]
    total = sum(sizes)
    padded = -(-total // (SMALL_W * ROW_CHUNK)) * (SMALL_W * ROW_CHUNK)

    def pack(parts):
        flat_ = jnp.concatenate([p.reshape(-1) for p in parts] + [jnp.zeros((padded - total,), F32)])
        return flat_.reshape(padded // SMALL_W, SMALL_W)

    g_pack = pack([jnp.concatenate(small_g[nm], axis=0) for nm in small_names])
    small_land, dwdw_land = exchange("gather_small", [op_gather_stack(g_pack),
                                                      op_scatter_axis(jnp.stack(dwdw, axis=0), 2)])
    small_out = adamw_update("adamw_small", pack([small_w[nm] for nm in small_names]),
                             pack([small_m[nm] for nm in small_names]), pack([small_v[nm] for nm in small_names]),
                             landed=small_land)

    def unpack(buf):
        flat_, out, o = buf.reshape(-1), {}, 0
        for nm, sz in zip(small_names, sizes):
            out[nm] = flat_[o:o + sz].reshape(small_w[nm].shape)
            o += sz
        return out
    small_res = [unpack(b) for b in small_out]

    dw_cols = w_dw.shape[2]

    def pad_dw(a):
        return jnp.pad(a, ((0, 0), (0, 32 - CONV_WIDTH), (0, 0))).reshape(L * 32, dw_cols)
    dw_out = adamw_update("adamw_w_dw", pad_dw(w_dw), pad_dw(m_w_dw), pad_dw(v_w_dw),
                          landed=dwdw_land.reshape(N_DEV, L * 32, dw_cols))
    dw_res = [a.reshape(L, 32, dw_cols)[:, :CONV_WIDTH] for a in dw_out]

    dmod_all = small_land.reshape(N_DEV, -1)[:, :L * N_MOD * D].reshape(N_DEV, L, N_MOD * D)
    dmod_mine = lax.dynamic_slice_in_dim(dmod_all, me * ada_cols, ada_cols, axis=2)
    dmod_pad = jnp.concatenate([dmod_mine, jnp.zeros_like(dmod_mine)], axis=0).astype(BF16)
    g_ada = jnp.stack([matmul("d_w_ada", c_pad, dmod_pad[:, l], "tn", F32) for l in range(L)], axis=0)
    ada_out = adamw_update("adamw_w_ada", flat(w_ada), flat(m_w_ada), flat(v_w_ada), grad=flat(g_ada))
    ada_res = [a.reshape(w_ada.shape) for a in ada_out]

    order = ("w_ada", "b_ada", "w_in", "sink", "w_dw", "conv_ln_g", "conv_ln_b", "w_oa", "w_ob", "w_out",
             "ln1_g", "ln1_b", "w_gu", "w_down", "ln2_g", "ln2_b")

    def result(nm, j):
        if nm == "w_ada":
            return ada_res[j]
        if nm == "w_dw":
            return dw_res[j]
        if nm in big_res:
            return jnp.stack([big_res[nm][l][j] for l in range(L)], axis=0)
        return small_res[j][nm]

    outs = [loss, grad_x.reshape(x.shape)]
    for j in range(4):
        outs += [result(nm, j) for nm in order]
    return tuple(outs)
```

```python
import math

import jax
import jax.numpy as jnp
from jax import lax
from jax.experimental import pallas as pl
from jax.experimental.pallas import tpu as pltpu

F32 = jnp.float32
BF16 = jnp.bfloat16
N_DEV = 8
AXES = ("x", "y", "c")
HEAD_DIM = 128
GQA_GROUP = 4
BLOCK = 128
ROPE_DIM = HEAD_DIM // 4
ROPE_HALF = ROPE_DIM // 2
ROPE_THETA = 500000.0
CONV_WIDTH = 31
CONV_HALO = 16
N_MOD = 6
LN_EPS = 1e-5
NEG_INF = -1e30
ADAM_LR, ADAM_B1, ADAM_B2, ADAM_EPS, ADAM_WD, ADAM_STEP = 0.001, 0.9, 0.999, 1e-08, 0.01, 10
VMEM_LIMIT = 56 * 1024 * 1024
LANE = 128
ROW_CHUNK = 16
SMALL_W = 512


def _params(*sem):
    return pltpu.CompilerParams(dimension_semantics=sem, vmem_limit_bytes=VMEM_LIMIT)


def _pick(dim, pref, mult=LANE):
    if dim <= pref:
        return dim
    best = None
    for t in range(mult, pref + 1, mult):
        if dim % t == 0:
            best = t
    assert best is not None, (dim, pref)
    return best


def matmul(name, a, b, mode, out_dtype, *, b_off=0, n=None, tm=1024, tn=1408, tk=2816, deps=(), epilogue=None,
           row_extras=()):
    if mode == "nn":
        (M, K), N = a.shape, (n or b.shape[1])
    elif mode == "tn":
        (K, M), N = a.shape, b.shape[1]
    else:
        (M, K), N = a.shape, b.shape[0]
    tm = _pick(M, tn, LANE) if mode == "tn" else _pick(M, tm, 16)
    tn, tk = _pick(math.gcd(N, b_off) if b_off else N, tn), _pick(K, tk)
    assert b_off % tn == 0 and N % tn == 0
    boff = b_off // tn
    nk = K // tk
    if mode == "nn":
        a_spec = pl.BlockSpec((tm, tk), lambda i, j, k: (i, k))
        b_spec = pl.BlockSpec((tk, tn), lambda i, j, k: (k, j + boff))
        dims = (((1,), (0,)), ((), ()))
    elif mode == "tn":
        a_spec = pl.BlockSpec((tk, tm), lambda i, j, k: (k, i))
        b_spec = pl.BlockSpec((tk, tn), lambda i, j, k: (k, j))
        dims = (((0,), (0,)), ((), ()))
    else:
        a_spec = pl.BlockSpec((tm, tk), lambda i, j, k: (i, k))
        b_spec = pl.BlockSpec((tn, tk), lambda i, j, k: (j, k))
        dims = (((1,), (1,)), ((), ()))
    n_ex, n_dep = len(row_extras), len(deps)
    assert epilogue is None or nk == 1

    def body(a_ref, b_ref, *rest):
        o_ref = rest[n_ex + n_dep]

        def dot():
            return lax.dot_general(a_ref[...].astype(BF16), b_ref[...].astype(BF16), dims, preferred_element_type=F32)
        if epilogue is not None:
            o_ref[...] = epilogue(dot(), [r[...] for r in rest[:n_ex]], pl.program_id(1)).astype(out_dtype)
        elif nk == 1:
            o_ref[...] = dot().astype(out_dtype)
        else:
            acc_ref = rest[n_ex + n_dep + 1]
            k = pl.program_id(2)

            @pl.when(k == 0)
            def _():
                acc_ref[...] = jnp.zeros_like(acc_ref)
            acc_ref[...] += dot()

            @pl.when(k == nk - 1)
            def _():
                o_ref[...] = acc_ref[...].astype(out_dtype)

    return pl.pallas_call(
        body, name=name, grid=(M // tm, N // tn, nk),
        in_specs=[a_spec, b_spec] + [pl.BlockSpec((tm, e.shape[1]), lambda i, j, k: (i, 0)) for e in row_extras]
        + [pl.BlockSpec(memory_space=pl.ANY)] * n_dep,
        out_specs=pl.BlockSpec((tm, tn), lambda i, j, k: (i, j)),
        out_shape=jax.ShapeDtypeStruct((M, N), out_dtype),
        scratch_shapes=[pltpu.VMEM((tm, tn), F32)] if nk > 1 else [],
        compiler_params=_params("parallel", "parallel", "arbitrary"),
    )(a, b, *row_extras, *deps)


def matmul_swiglu(h, w_gu, dff, *, tm=1024, tn=512):
    M, K = h.shape
    tm, tn = _pick(M, tm, 16), _pick(dff, tn)
    nj = dff // tn

    def body(a_ref, bg_ref, bu_ref, act_ref, gate_ref, up_ref):
        a = a_ref[...]
        gate = jnp.dot(a, bg_ref[...], preferred_element_type=F32)
        up = jnp.dot(a, bu_ref[...], preferred_element_type=F32)
        act_ref[...] = (gate * _sig(gate) * up).astype(BF16)
        gate_ref[...] = gate.astype(BF16)
        up_ref[...] = up.astype(BF16)

    out_spec = pl.BlockSpec((tm, tn), lambda i, j: (i, j))
    return pl.pallas_call(
        body, name="gu_swiglu", grid=(M // tm, nj),
        in_specs=[pl.BlockSpec((tm, K), lambda i, j: (i, 0)), pl.BlockSpec((K, tn), lambda i, j: (0, j)),
                  pl.BlockSpec((K, tn), lambda i, j: (0, nj + j))],
        out_specs=[out_spec] * 3,
        out_shape=[jax.ShapeDtypeStruct((M, dff), BF16)] * 3,
        compiler_params=_params("parallel", "parallel"),
    )(h, w_gu, w_gu)


def rowwise(name, fn, nrows, row_ins, vec_ins, row_outs, n_sums=0, *, W, tr=256, cw=None, rc=ROW_CHUNK,
            out_layer=None, into=None):
    cw = cw or W
    tr = _pick(nrows, tr, ROW_CHUNK)
    rc = min(rc, tr)
    assert W % cw == 0 and nrows % ROW_CHUNK == 0 and tr % rc == 0 and rc % 8 == 0
    nrt = nrows // tr
    n_ri, n_v, n_ro = len(row_ins), len(vec_ins), len(row_outs)

    def row_spec(lead, colblk, shift, whole):
        w = cw if whole else W

        def rmap(i):
            return jnp.clip(i + shift, 0, nrt - 1) if shift else i
        if lead is None:
            return pl.BlockSpec((tr, w), lambda i: (rmap(i), colblk))
        return pl.BlockSpec((None, tr, w), lambda i: (lead, rmap(i), colblk))

    n_prev = len(into) if into is not None else 0
    in_specs = [row_spec(*ri[1:]) for ri in row_ins]
    in_specs += [pl.BlockSpec((1, W), lambda i: (0, 0)) for _ in vec_ins]
    in_specs += [pl.BlockSpec(memory_space=pl.ANY)] * n_prev
    if out_layer is None:
        out_specs = [pl.BlockSpec((tr, p * W), lambda i: (i, 0)) for _, p in row_outs]
        out_shape = [jax.ShapeDtypeStruct((nrows, p * W), dt) for dt, p in row_outs]
    else:
        out_specs = [pl.BlockSpec((None, tr, p * W), lambda i: (out_layer[0], i, 0)) for _, p in row_outs]
        out_shape = [jax.ShapeDtypeStruct((out_layer[1], nrows, p * W), dt) for dt, p in row_outs]
    out_specs += [pl.BlockSpec((1, W), lambda i: (0, 0)) for _ in range(n_sums)]
    out_shape += [jax.ShapeDtypeStruct((1, W), F32) for _ in range(n_sums)]

    def body(*refs):
        rin, vin = refs[:n_ri], refs[n_ri:n_ri + n_v]
        refs = refs[n_prev:]
        rout = refs[n_ri + n_v:n_ri + n_v + n_ro]
        sout = refs[n_ri + n_v + n_ro:n_ri + n_v + n_ro + n_sums]
        acc = refs[n_ri + n_v + n_ro + n_sums:]
        i = pl.program_id(0)
        if n_sums:
            @pl.when(i == 0)
            def _():
                for a in acc:
                    a[...] = jnp.zeros_like(a)
        for c in range(W // cw):
            c0 = c * cw
            vecs = [v[:, c0:c0 + cw] for v in vin]

            def step(r, carry, c=c, c0=c0, vecs=vecs):
                r0 = pl.multiple_of(r * rc, rc)
                rows = [ref[pl.ds(r0, rc), :] if ri[4] else ref[pl.ds(r0, rc), c0:c0 + cw]
                        for ref, ri in zip(rin, row_ins)]
                outs, sums = fn(rows, vecs, c, i)
                for oref, (dt, _), pieces in zip(rout, row_outs, outs):
                    for pi, piece in enumerate(pieces):
                        oref[pl.ds(r0, rc), pi * W + c0:pi * W + c0 + cw] = piece.astype(dt)
                for a, s in zip(acc, sums):
                    part = s[0:8]
                    for q in range(1, rc // 8):
                        part = part + s[8 * q:8 * q + 8]
                    a[:, c0:c0 + cw] += part
                return carry
            lax.fori_loop(0, tr // rc, step, 0)
        if n_sums:
            @pl.when(i == nrt - 1)
            def _():
                for o, a in zip(sout, acc):
                    o[...] = jnp.sum(a[...], axis=0, keepdims=True)

    res = pl.pallas_call(
        body, name=name, grid=(nrt,), in_specs=in_specs, out_specs=out_specs, out_shape=out_shape,
        scratch_shapes=[pltpu.VMEM((8, W), F32) for _ in range(n_sums)],
        input_output_aliases={n_ri + n_v + q: q for q in range(n_prev)},
        compiler_params=_params("arbitrary"),
    )(*[ri[0] for ri in row_ins], *vec_ins, *(into or ()))
    return res


def _ri(arr, lead=None, col=0, shift=0, whole=False):
    return (arr, lead, col, shift, whole)


def _sig(x):
    return jax.nn.sigmoid(x)


def _ln_stats(t):
    mu = jnp.mean(t, axis=-1, keepdims=True)
    xc = t - mu
    var = jnp.mean(xc * xc, axis=-1, keepdims=True)
    rstd = lax.rsqrt(var + LN_EPS)
    return xc * rstd, rstd


def _ln_bwd(dy_g, xhat, rstd):
    m1 = jnp.mean(dy_g, axis=-1, keepdims=True)
    m2 = jnp.mean(dy_g * xhat, axis=-1, keepdims=True)
    return rstd * (dy_g - m1 - xhat * m2)


def _rope(x, cos, s_lo, s_hi, sign):
    up = pltpu.roll(x, HEAD_DIM - ROPE_HALF, 1)
    down = pltpu.roll(x, ROPE_HALF, 1)
    return x * cos + sign * (up * s_lo + down * s_hi)


def rope_tables(S):
    pos = jnp.arange(S, dtype=F32)
    inv_freq = ROPE_THETA ** (-jnp.arange(0, ROPE_DIM, 2, dtype=F32) / ROPE_DIM)
    ang = pos[:, None] * inv_freq[None, :]
    cos, sin = jnp.cos(ang), jnp.sin(ang)
    ones = jnp.ones((S, HEAD_DIM - ROPE_DIM), F32)
    zeros = jnp.zeros((S, HEAD_DIM - ROPE_DIM), F32)
    zh = jnp.zeros((S, ROPE_HALF), F32)
    t_cos = jnp.concatenate([cos, cos, ones], axis=1)
    t_lo = jnp.concatenate([-sin, zh, zeros], axis=1)
    t_hi = jnp.concatenate([zh, sin, zeros], axis=1)
    return t_cos, t_lo, t_hi


ATTN_HEADS_PER_STEP = 4


def _attn_specs(S, D, Dkv):
    nb, nkv, qb = S // BLOCK, Dkv // HEAD_DIM, D // HEAD_DIM
    hps = math.gcd(ATTN_HEADS_PER_STEP, nkv)
    assert qb % hps == 0 and (qb + nkv) % hps == 0
    gw = GQA_GROUP * HEAD_DIM
    q_spec = pl.BlockSpec((BLOCK, hps * gw), lambda h, n: (n, h))

    def band(col0):
        c0 = col0 // hps
        return [pl.BlockSpec((BLOCK, hps * HEAD_DIM), lambda h, n: (jnp.maximum(n - 1, 0), c0 + h)),
                pl.BlockSpec((BLOCK, hps * HEAD_DIM), lambda h, n: (n, c0 + h)),
                pl.BlockSpec((BLOCK, hps * HEAD_DIM), lambda h, n: (jnp.minimum(n + 1, nb - 1), c0 + h))]
    sink_spec = pl.BlockSpec((hps, GQA_GROUP * BLOCK, 1), lambda h, n: (h, 0, 0))
    return nb, nkv, hps, gw, q_spec, band(qb), band(qb + nkv), sink_spec


def _attn_valid(n, S):
    shape = (GQA_GROUP * BLOCK, 3 * BLOCK)
    row = lax.broadcasted_iota(jnp.int32, shape, 0) & (BLOCK - 1)
    col = lax.broadcasted_iota(jnp.int32, shape, 1)
    rel = col - BLOCK - row
    kpos = (n - 1) * BLOCK + col
    return (jnp.abs(rel) <= BLOCK) & (kpos >= 0) & (kpos < S)


def _head(ref, hh, width):
    return ref[:, hh * width:(hh + 1) * width]


def _attn_probs(q, k_blocks, sink, valid):
    qs = jnp.concatenate([q[:, g * HEAD_DIM:(g + 1) * HEAD_DIM] for g in range(GQA_GROUP)], axis=0)
    kb = jnp.concatenate(k_blocks, axis=0)
    s = lax.dot_general(qs, kb, (((1,), (1,)), ((), ())), preferred_element_type=F32) * (HEAD_DIM ** -0.5)
    s = jnp.where(valid, s, NEG_INF)
    m = jnp.maximum(jnp.max(s, axis=-1, keepdims=True), sink)
    p = jnp.exp(s - m)
    e_sink = jnp.exp(sink - m)
    denom = jnp.sum(p, axis=-1, keepdims=True) + e_sink
    return qs, kb, p / denom, e_sink / denom


def attn_fwd(qkv, sinkcol, S, D, Dkv):
    nb, nkv, hps, gw, q_spec, k_specs, v_specs, sink_spec = _attn_specs(S, D, Dkv)

    def body(q_ref, k0, k1, k2, v0, v1, v2, sink_ref, o_ref):
        valid = _attn_valid(pl.program_id(1), S)
        for hh in range(hps):
            _, _, w, _ = _attn_probs(_head(q_ref, hh, gw), [_head(r, hh, HEAD_DIM) for r in (k0, k1, k2)],
                                     sink_ref[hh], valid)
            vb = jnp.concatenate([_head(r, hh, HEAD_DIM) for r in (v0, v1, v2)], axis=0)
            o = jnp.dot(w.astype(BF16), vb, preferred_element_type=F32)
            for g in range(GQA_GROUP):
                c0 = hh * gw + g * HEAD_DIM
                o_ref[:, c0:c0 + HEAD_DIM] = o[g * BLOCK:(g + 1) * BLOCK].astype(BF16)

    return pl.pallas_call(
        body, name="attn_fwd", grid=(nkv // hps, nb),
        in_specs=[q_spec, *k_specs, *v_specs, sink_spec],
        out_specs=pl.BlockSpec((BLOCK, hps * gw), lambda h, n: (n, h)),
        out_shape=jax.ShapeDtypeStruct((S, D), BF16),
        compiler_params=_params("parallel", "arbitrary"),
    )(qkv, qkv, qkv, qkv, qkv, qkv, qkv, sinkcol)


def attn_bwd(qkv, sinkcol, datt, tables, S, D, Dkv):
    nb, nkv, hps, gw, q_spec, k_specs, v_specs, sink_spec = _attn_specs(S, D, Dkv)

    def body(q_ref, k0, k1, k2, v0, v1, v2, sink_ref, do_ref, tc_ref, tl_ref, th_ref, dq_ref, dkp_ref, dvp_ref,
             dsink_ref):
        n = pl.program_id(1)
        valid = _attn_valid(n, S)
        tc_, tl, th = tc_ref[...], tl_ref[...], th_ref[...]

        @pl.when(n == 0)
        def _():
            dsink_ref[...] = jnp.zeros_like(dsink_ref)
        for hh in range(hps):
            qs, kb, w, w_sink = _attn_probs(_head(q_ref, hh, gw), [_head(r, hh, HEAD_DIM) for r in (k0, k1, k2)],
                                            sink_ref[hh], valid)
            vb = jnp.concatenate([_head(r, hh, HEAD_DIM) for r in (v0, v1, v2)], axis=0)
            do = _head(do_ref, hh, gw)
            dos = jnp.concatenate([do[:, g * HEAD_DIM:(g + 1) * HEAD_DIM] for g in range(GQA_GROUP)], axis=0)
            dv = lax.dot_general(w.astype(BF16), dos, (((0,), (0,)), ((), ())), preferred_element_type=F32)
            dw = lax.dot_general(dos, vb, (((1,), (1,)), ((), ())), preferred_element_type=F32)
            delta = jnp.sum(w * dw, axis=-1, keepdims=True)
            ds = (w * (dw - delta) * (HEAD_DIM ** -0.5)).astype(BF16)
            dq = jnp.dot(ds, kb, preferred_element_type=F32)
            dk = lax.dot_general(ds, qs, (((0,), (0,)), ((), ())), preferred_element_type=F32)
            for g in range(GQA_GROUP):
                c0 = hh * gw + g * HEAD_DIM
                dq_ref[:, c0:c0 + HEAD_DIM] = _rope(dq[g * BLOCK:(g + 1) * BLOCK], tc_, tl, th, -1.0).astype(BF16)
            for j in range(3):
                dkp_ref[j, :, hh * HEAD_DIM:(hh + 1) * HEAD_DIM] = dk[j * BLOCK:(j + 1) * BLOCK]
                dvp_ref[j, :, hh * HEAD_DIM:(hh + 1) * HEAD_DIM] = dv[j * BLOCK:(j + 1) * BLOCK]
            t = w_sink * delta
            for g in range(GQA_GROUP):
                dsink_ref[hh, g:g + 1, :] -= jnp.sum(t[g * BLOCK:(g + 1) * BLOCK], axis=0, keepdims=True)

    part_spec = pl.BlockSpec((3, BLOCK, hps * HEAD_DIM), lambda h, n: (0, n, h))
    return pl.pallas_call(
        body, name="attn_bwd", grid=(nkv // hps, nb),
        in_specs=[q_spec, *k_specs, *v_specs, sink_spec, pl.BlockSpec((BLOCK, hps * gw), lambda h, n: (n, h))]
        + [pl.BlockSpec((BLOCK, HEAD_DIM), lambda h, n: (n, 0))] * 3,
        out_specs=[pl.BlockSpec((BLOCK, hps * gw), lambda h, n: (n, h)), part_spec, part_spec,
                   pl.BlockSpec((hps, 8, LANE), lambda h, n: (h, 0, 0))],
        out_shape=[jax.ShapeDtypeStruct((S, D), BF16), jax.ShapeDtypeStruct((3, S, Dkv), F32),
                   jax.ShapeDtypeStruct((3, S, Dkv), F32), jax.ShapeDtypeStruct((nkv, 8, LANE), F32)],
        compiler_params=_params("parallel", "arbitrary"),
    )(qkv, qkv, qkv, qkv, qkv, qkv, qkv, sinkcol, datt, *tables)


CONV_TC = 128
CONV_ROWS = 64


def _shift_rows(win, b):
    return win if b == 0 else pltpu.roll(win, win.shape[0] - b, 0)


def _conv_fill_u(ga_ref, gb_ref, upad_ref, S):
    tc = upad_ref.shape[1]
    zero = jnp.zeros((CONV_HALO, tc), F32)
    upad_ref[0:CONV_HALO, :] = zero
    upad_ref[S + CONV_HALO:S + 2 * CONV_HALO, :] = zero

    def fill(r, carry):
        r0 = pl.multiple_of(r * CONV_ROWS, CONV_ROWS)
        upad_ref[pl.ds(r0 + CONV_HALO, CONV_ROWS), :] = ga_ref[pl.ds(r0, CONV_ROWS), :] * _sig(gb_ref[pl.ds(r0, CONV_ROWS), :])
        return carry
    lax.fori_loop(0, S // CONV_ROWS, fill, 0)


def conv_fwd(pglu, wdw, S, D):
    tc = min(CONV_TC, D)
    nct = D // tc

    def body(ga_ref, gb_ref, w_ref, o_ref, upad_ref):
        _conv_fill_u(ga_ref, gb_ref, upad_ref, S)
        w = w_ref[...]

        def step(r, carry):
            r0 = pl.multiple_of(r * CONV_ROWS, CONV_ROWS)
            win = upad_ref[pl.ds(r0, CONV_ROWS + 2 * CONV_HALO), :]
            acc = jnp.zeros((CONV_ROWS, tc), F32)
            for b in range(8):
                shifted = _shift_rows(win, b)
                for k in range(CONV_WIDTH):
                    o = k + CONV_HALO - CONV_WIDTH // 2
                    if o % 8 == b:
                        acc = acc + w[k:k + 1, :] * shifted[o - b:o - b + CONV_ROWS, :]
            o_ref[pl.ds(r0, CONV_ROWS), :] = acc
            return carry
        lax.fori_loop(0, S // CONV_ROWS, step, 0)

    return pl.pallas_call(
        body, name="conv_fwd", grid=(nct,),
        in_specs=[pl.BlockSpec((S, tc), lambda j: (0, j)), pl.BlockSpec((S, tc), lambda j: (0, nct + j)),
                  pl.BlockSpec((CONV_WIDTH, tc), lambda j: (0, j))],
        out_specs=pl.BlockSpec((S, tc), lambda j: (0, j)),
        out_shape=jax.ShapeDtypeStruct((S, D), F32),
        scratch_shapes=[pltpu.VMEM((S + 2 * CONV_HALO, tc), F32)],
        compiler_params=_params("parallel"),
    )(pglu, pglu, wdw)


def conv_bwd(duc, pglu, wdw, S, D):
    tc = min(CONV_TC, D)
    nct = D // tc
    half = CONV_WIDTH // 2

    def body(d_ref, ga_ref, gb_ref, w_ref, dga_ref, dgb_ref, dw_ref, upad_ref, dpad_ref, dwacc_ref):
        _conv_fill_u(ga_ref, gb_ref, upad_ref, S)
        zero = jnp.zeros((CONV_HALO, tc), F32)
        dpad_ref[0:CONV_HALO, :] = zero
        dpad_ref[S + CONV_HALO:S + 2 * CONV_HALO, :] = zero

        def fill(r, carry):
            r0 = pl.multiple_of(r * CONV_ROWS, CONV_ROWS)
            dpad_ref[pl.ds(r0 + CONV_HALO, CONV_ROWS), :] = d_ref[pl.ds(r0, CONV_ROWS), :]
            return carry
        lax.fori_loop(0, S // CONV_ROWS, fill, 0)
        dwacc_ref[...] = jnp.zeros_like(dwacc_ref)
        w = w_ref[...]

        def step(r, carry):
            r0 = pl.multiple_of(r * CONV_ROWS, CONV_ROWS)
            uwin = upad_ref[pl.ds(r0, CONV_ROWS + 2 * CONV_HALO), :]
            dwin = dpad_ref[pl.ds(r0, CONV_ROWS + 2 * CONV_HALO), :]
            d = dwin[CONV_HALO:CONV_HALO + CONV_ROWS, :]
            du = jnp.zeros((CONV_ROWS, tc), F32)
            for b in range(8):
                d_shifted, u_shifted = _shift_rows(dwin, b), _shift_rows(uwin, b)
                for k in range(CONV_WIDTH):
                    o = CONV_HALO + half - k
                    if o % 8 == b:
                        du = du + w[k:k + 1, :] * d_shifted[o - b:o - b + CONV_ROWS, :]
                    o = CONV_HALO + k - half
                    if o % 8 == b:
                        prod = d * u_shifted[o - b:o - b + CONV_ROWS, :]
                        part = prod[0:8]
                        for q in range(1, CONV_ROWS // 8):
                            part = part + prod[8 * q:8 * q + 8]
                        dwacc_ref[k] += part
            ga = ga_ref[pl.ds(r0, CONV_ROWS), :]
            sg = _sig(gb_ref[pl.ds(r0, CONV_ROWS), :])
            dga_ref[pl.ds(r0, CONV_ROWS), :] = (du * sg).astype(BF16)
            dgb_ref[pl.ds(r0, CONV_ROWS), :] = (du * ga * sg * (1.0 - sg)).astype(BF16)
            return carry
        lax.fori_loop(0, S // CONV_ROWS, step, 0)
        dw_ref[...] = jnp.sum(dwacc_ref[...], axis=1)

    return pl.pallas_call(
        body, name="conv_bwd", grid=(nct,),
        in_specs=[pl.BlockSpec((S, tc), lambda j: (0, j)), pl.BlockSpec((S, tc), lambda j: (0, j)),
                  pl.BlockSpec((S, tc), lambda j: (0, nct + j)), pl.BlockSpec((CONV_WIDTH, tc), lambda j: (0, j))],
        out_specs=[pl.BlockSpec((S, tc), lambda j: (0, j)), pl.BlockSpec((S, tc), lambda j: (0, j)),
                   pl.BlockSpec((32, tc), lambda j: (0, j))],
        out_shape=[jax.ShapeDtypeStruct((S, D), BF16), jax.ShapeDtypeStruct((S, D), BF16),
                   jax.ShapeDtypeStruct((32, D), F32)],
        scratch_shapes=[pltpu.VMEM((S + 2 * CONV_HALO, tc), F32), pltpu.VMEM((S + 2 * CONV_HALO, tc), F32),
                        pltpu.VMEM((32, 8, tc), F32)],
        compiler_params=_params("parallel"),
    )(duc, pglu, pglu, wdw)


def exchange(name, ops):
    n = len(ops)

    def body(*refs):
        xs, outs = refs[:n], refs[n:2 * n]
        send, recv, lsem = refs[2 * n:]
        mx, my, mc = lax.axis_index("x"), lax.axis_index("y"), lax.axis_index("c")
        me = 4 * mx + 2 * my + mc
        local = [pltpu.make_async_copy(op[3](xs[i], me), op[4](outs[i], me), lsem.at[i]) for i, op in enumerate(ops)]
        for cp in local:
            cp.start()
        copies = []
        for k in range(1, N_DEV):
            px = 1 - mx if k & 4 else mx
            py = 1 - my if k & 2 else my
            pc = 1 - mc if k & 1 else mc
            peer = 4 * px + 2 * py + pc
            for i, op in enumerate(ops):
                cp = pltpu.make_async_remote_copy(
                    src_ref=op[3](xs[i], peer), dst_ref=op[4](outs[i], me),
                    send_sem=send.at[i, k - 1], recv_sem=recv.at[i, k - 1],
                    device_id=(px, py, pc), device_id_type=pl.DeviceIdType.MESH)
                cp.start()
                copies.append(cp)
        for cp in copies:
            cp.wait()
        for cp in local:
            cp.wait()

    any_spec = pl.BlockSpec(memory_space=pl.ANY)
    return pl.pallas_call(
        body, name=name,
        in_specs=[any_spec] * n, out_specs=[any_spec] * n,
        out_shape=[jax.ShapeDtypeStruct(op[1], op[2]) for op in ops],
        scratch_shapes=[pltpu.SemaphoreType.DMA((n, N_DEV - 1)), pltpu.SemaphoreType.DMA((n, N_DEV - 1)),
                        pltpu.SemaphoreType.DMA((n,))],
        compiler_params=pltpu.CompilerParams(has_side_effects=True),
    )(*[op[0] for op in ops])


_HBM = pl.BlockSpec(memory_space=pltpu.HBM)
_SEM = pl.BlockSpec(memory_space=pltpu.SEMAPHORE)
_EFFECT = pltpu.SideEffectType.DATAFLOW_SIDE_EFFECTING


def _me_and_peers():
    mx, my, mc = lax.axis_index("x"), lax.axis_index("y"), lax.axis_index("c")
    peers = []
    for k in range(1, N_DEV):
        px = 1 - mx if k & 4 else mx
        py = 1 - my if k & 2 else my
        pc = 1 - mc if k & 1 else mc
        peers.append((k, (px, py, pc), 4 * px + 2 * py + pc))
    return 4 * mx + 2 * my + mc, peers


PLACE_STEPS = 4


def exchange_local(name, ops, me_arr, deps=()):
    n = len(ops)
    in_specs, out_specs = [], []
    for op in ops:
        kind, axis = op[5]
        rows, cols = op[0].shape if kind == "gather" else op[1][1:]
        steps = PLACE_STEPS if rows % (PLACE_STEPS * ROW_CHUNK) == 0 else 1
        tr = rows // steps

        def row(i, steps=steps):
            return i if steps > 1 else 0

        def window(i, me, axis=axis, steps=steps):
            return (me[0] * steps + row(i, steps), 0) if axis == 0 else (row(i, steps), me[0])
        if kind == "gather":
            in_specs.append(pl.BlockSpec((tr, cols), lambda i, me, row=row: (row(i), 0)))
            out_specs.append(pl.BlockSpec((tr, cols), window))
        else:
            in_specs.append(pl.BlockSpec((tr, cols), window))
            out_specs.append(pl.BlockSpec((None, tr, cols), lambda i, me, row=row: (me[0], row(i), 0)))

    def body(me_ref, *refs):
        for i in range(n):
            refs[n + len(deps) + i][...] = refs[i][...]

    return pl.pallas_call(
        body, name=name,
        grid_spec=pltpu.PrefetchScalarGridSpec(
            num_scalar_prefetch=1, grid=(PLACE_STEPS,),
            in_specs=in_specs + [pl.BlockSpec(memory_space=pl.ANY)] * len(deps), out_specs=out_specs),
        out_shape=[jax.ShapeDtypeStruct(op[1], op[2]) for op in ops],
        compiler_params=_params("arbitrary"),
    )(me_arr, *[op[0] for op in ops], *deps)


ALL_PEERS = (1, 2, 3, 4, 5, 6, 7)
SIBLING = 1
SAME_CORE = (2, 4, 6)


def _flipped(pos, k):
    p = (1 - pos[0] if k & 4 else pos[0], 1 - pos[1] if k & 2 else pos[1], 1 - pos[2] if k & 1 else pos[2])
    return p, 4 * p[0] + 2 * p[1] + p[2]


def direct_copies(ops, ks):
    n = len(ops)

    def build(refs, send, recv):
        xs, lands = refs[:n], refs[n:2 * n]
        pos = (lax.axis_index("x"), lax.axis_index("y"), lax.axis_index("c"))
        _, me = _flipped(pos, 0)
        copies = []
        for j, k in enumerate(ks):
            peer_id, peer = _flipped(pos, k)
            for i, op in enumerate(ops):
                s = i * len(ks) + j
                copies.append(pltpu.make_async_remote_copy(
                    src_ref=op[3](xs[i], peer), dst_ref=op[4](lands[i], me), send_sem=send.at[s], recv_sem=recv.at[s],
                    device_id=peer_id, device_id_type=pl.DeviceIdType.MESH))
        return copies
    return n * len(ks), build


def forward_copies(ops, ks):
    n = len(ops)

    def build(refs, send, recv):
        lands = refs[:n]
        pos = (lax.axis_index("x"), lax.axis_index("y"), lax.axis_index("c"))
        sibling_id, _ = _flipped(pos, SIBLING)
        copies = []
        for j, k in enumerate(ks):
            _, origin = _flipped(pos, k)
            for i, op in enumerate(ops):
                s = i * len(ks) + j
                region = op[4](lands[i], origin)
                copies.append(pltpu.make_async_remote_copy(
                    src_ref=region, dst_ref=region, send_sem=send.at[s], recv_sem=recv.at[s],
                    device_id=sibling_id, device_id_type=pl.DeviceIdType.MESH))
        return copies
    return n * len(ks), build


def exchange_start(name, arrays, plan):
    n_sems, build = plan
    n = len(arrays)

    def body(*refs):
        for cp in build(refs[:n], refs[n], refs[n + 1]):
            cp.start()
        refs[-1][...] = jnp.zeros_like(refs[-1])

    args = [pltpu.with_memory_space_constraint(a, pltpu.HBM) for a in arrays]
    sems = pltpu.SemaphoreType.DMA((n_sems,))
    outs = pl.pallas_call(
        body, name=name,
        out_shape=(sems, sems, *[pltpu.HBM(a.shape, a.dtype) for a in args], jax.ShapeDtypeStruct((8, LANE), F32)),
        in_specs=[_HBM] * n,
        out_specs=(_SEM, _SEM, *[_HBM] * n, pl.BlockSpec(memory_space=pltpu.VMEM)),
        input_output_aliases={i: 2 + i for i in range(n)},
        compiler_params=pltpu.CompilerParams(has_side_effects=_EFFECT),
    )(*args)
    return (build, outs[0], outs[1], outs[2:2 + n]), outs[-1]


def exchange_wait(name, handle, after):
    build, send, recv, thru = handle
    n = len(thru)

    def body(*refs):
        for cp in build(refs[:n], refs[n], refs[n + 1]):
            cp.wait_send()
            cp.wait_recv()

    return pl.pallas_call(
        body, name=name,
        out_shape=[pltpu.HBM(a.shape, a.dtype) for a in thru],
        in_specs=[_HBM] * n + [_SEM, _SEM, pl.BlockSpec(memory_space=pl.ANY)],
        out_specs=[_HBM] * n,
        input_output_aliases={i: i for i in range(n)},
        compiler_params=pltpu.CompilerParams(has_side_effects=_EFFECT),
    )(*thru, send, recv, after)


def _whole(ref, q):
    return ref


def _slot(ref, q):
    return ref.at[q]


def op_gather_stack(x):
    return (x, (N_DEV,) + x.shape, x.dtype, _whole, _slot)


def op_gather_axis(x, axis):
    size = x.shape[axis]
    shape = x.shape[:axis] + (N_DEV * size,) + x.shape[axis + 1:]

    def dst(ref, q):
        idx = [slice(None)] * len(shape)
        idx[axis] = pl.ds(pl.multiple_of(q * size, size), size)
        return ref.at[tuple(idx)]
    return (x, shape, x.dtype, _whole, dst, ("gather", axis))


def op_scatter_axis(x, axis):
    size = x.shape[axis] // N_DEV
    shape = x.shape[:axis] + (size,) + x.shape[axis + 1:]

    def src(ref, q):
        idx = [slice(None)] * len(shape)
        idx[axis] = pl.ds(pl.multiple_of(q * size, size), size)
        return ref.at[tuple(idx)]
    return (x, (N_DEV,) + shape, x.dtype, src, _slot, ("scatter", axis))


def _adamw(w, g, m, v):
    m = ADAM_B1 * m + (1.0 - ADAM_B1) * g
    v = ADAM_B2 * v + (1.0 - ADAM_B2) * (g * g)
    m_hat = m / (1.0 - ADAM_B1 ** ADAM_STEP)
    v_hat = v / (1.0 - ADAM_B2 ** ADAM_STEP)
    delta = -ADAM_LR * (m_hat / (jnp.sqrt(v_hat) + ADAM_EPS) + ADAM_WD * w)
    return delta, m, v


def adamw_update(name, w, m, v, *, layer=None, landed=None, grad=None, into=None):
    R, W = w.shape[-2:]
    n_g = N_DEV if landed is not None else 1

    def fn(rows, vecs, c, i):
        g = rows[0].astype(F32)
        for q in range(1, n_g):
            g = g + rows[q].astype(F32)
        wv, mv, vv = rows[n_g:]
        delta, m2, v2 = _adamw(wv, g, mv, vv)
        return [[g], [delta], [m2], [v2]], []

    g_ins = [_ri(landed, lead=q) for q in range(N_DEV)] if landed is not None else [_ri(grad)]
    cw = LANE if W % LANE == 0 else W
    return rowwise(name, fn, R, g_ins + [_ri(w, lead=layer), _ri(m, lead=layer), _ri(v, lead=layer)], [],
                   [(F32, 1)] * 4, W=W, tr=128, cw=cw, rc=32,
                   out_layer=None if layer is None else (layer, w.shape[0]), into=into)


def kernel(x, c, w_ada, b_ada, w_in, sink, w_dw, conv_ln_g, conv_ln_b, w_oa, w_ob, w_out, ln1_g, ln1_b, w_gu, w_down, ln2_g, ln2_b, loss_target, m_w_ada, m_b_ada, m_w_in, m_sink, m_w_dw, m_conv_ln_g, m_conv_ln_b, m_w_oa, m_w_ob, m_w_out, m_ln1_g, m_ln1_b, m_w_gu, m_w_down, m_ln2_g, m_ln2_b, v_w_ada, v_b_ada, v_w_in, v_sink, v_w_dw, v_conv_ln_g, v_conv_ln_b, v_w_oa, v_w_ob, v_w_out, v_ln1_g, v_ln1_b, v_w_gu, v_w_down, v_ln2_g, v_ln2_b):
    L = w_ada.shape[0]
    S, D = x.shape[1], x.shape[2]
    Dkv = D // GQA_GROUP
    Dqkv = D + 2 * Dkv
    DFF = w_down.shape[1] * N_DEV
    nq, nkv, nb = D // HEAD_DIM, Dkv // HEAD_DIM, S // BLOCK
    alpha = (2.0 * L) ** 0.25
    me = 4 * lax.axis_index("x") + 2 * lax.axis_index("y") + lax.axis_index("c")
    x0 = x.reshape(S, D)
    target = loss_target.reshape(S, D)
    t_cos, t_lo, t_hi = rope_tables(S)

    c_act = jax.nn.silu(c)
    c_all = exchange("gather_c", [op_gather_stack(c_act)])[0].reshape(N_DEV, D)
    c_pad = jnp.concatenate([c_all, jnp.zeros_like(c_all)], axis=0).astype(BF16)
    ada_cols = w_ada.shape[2]
    mod_part = jnp.stack([matmul("mod_mm", c_pad, w_ada[l], "nn", F32)[:N_DEV] for l in range(L)], axis=1)
    mod_land = exchange("scatter_mod", [op_scatter_axis(mod_part, 0)])[0]
    mod = jnp.transpose(mod_land.reshape(N_DEV, L, ada_cols), (1, 0, 2)).reshape(L, N_MOD * D) + b_ada
    mods = [[mod[l:l + 1, j * D:(j + 1) * D] for j in range(N_MOD)] for l in range(L)]

    me_arr = me.astype(jnp.int32).reshape(1)

    def begin(tag, ops, dep, ks=ALL_PEERS):
        lands = exchange_local(tag + "_local", ops, me_arr, [dep])
        return exchange_start(tag + "_start", [op[0] for op in ops] + list(lands), direct_copies(ops, ks))

    def gather_begin(tag, ops, dep):
        return begin(tag, ops, dep, (SIBLING,) + SAME_CORE)

    def gather_forward(tag, ops, handle, after):
        lands = exchange_wait(tag + "_wait", handle, after)[len(ops):]
        return exchange_start(tag + "_fwd_start", list(lands), forward_copies(ops, SAME_CORE))

    def gather_end(tag, handle, after):
        return exchange_wait(tag + "_fwd_wait", handle, after)

    def gather_first(l):
        return [op_gather_axis(w_in[l].astype(BF16), 1), op_gather_axis(w_dw[l], 1)]

    def gather_rest(l):
        return [op_gather_axis(w_gu[l].astype(BF16), 1), op_gather_axis(w_oa[l].astype(BF16), 0),
                op_gather_axis(w_ob[l].astype(BF16), 0), op_gather_axis(w_out[l].astype(BF16), 0),
                op_gather_axis(w_down[l].astype(BF16), 0)]

    W_in, W_gu, W_oa, W_ob, W_out, W_down, W_dw = ([None] * L for _ in range(7))
    first_ops = gather_first(0)
    ag_first, _ = gather_begin("ag0a", first_ops, mod_land)

    def vec(a, l):
        return a[l:l + 1]

    def f_mod(rows, vecs, c_, i_):
        (xv,), (sc, sh) = rows, vecs
        return [[xv * (1.0 + sc) + sh]], []

    def f_convln(rows, vecs, c_, i_):
        (uc,), (g, b) = rows, vecs
        xhat, _ = _ln_stats(uc)
        nrm = xhat * g + b
        return [[nrm * _sig(nrm)]], []

    def f_merge(rows, vecs, c_, i_):
        g_a, g_b, y_a, y_b = rows
        return [[_sig(g_a) * y_a + _sig(g_b) * y_b]], []

    def f_ln(rows, vecs, c_, i_):
        (xv, r), (gt, g, b, sc, sh) = rows, vecs
        xhat, _ = _ln_stats(alpha * xv + (1.0 + gt) * r)
        y = xhat * g + b
        return [[y], [y * (1.0 + sc) + sh]], []

    saved = []
    xl = x0
    h = rowwise("modulate", f_mod, S, [_ri(x0)], [mods[0][1], mods[0][0]], [(BF16, 1)], W=D)[0]
    ag_first, _ = gather_forward("ag0a", first_ops, ag_first, h)
    W_in[0], W_dw[0] = gather_end("ag0a", ag_first, h)
    ag_next, next_ops = None, None

    def after_token(token):
        return [token] if token is not None else []

    def rope_heads(tile, extras, j):
        tc_, tl, th = extras
        heads = []
        for hh in range(tile.shape[1] // HEAD_DIM):
            xh = tile[:, hh * HEAD_DIM:(hh + 1) * HEAD_DIM]
            is_qk = j * (tile.shape[1] // HEAD_DIM) + hh < nq + nkv
            heads.append(jnp.where(is_qk, _rope(xh, tc_, tl, th, 1.0), xh))
        return jnp.concatenate(heads, axis=1)

    for l in range(L):
        sh_a, sc_a, gt_a, sh_f, sc_f, gt_f = mods[l]
        if l == 0:
            rest_ops = gather_rest(0)
            ag_rest, token = gather_begin("ag0b", rest_ops, W_in[0])
        elif l + 1 < L:
            next_ops = gather_first(l + 1) + gather_rest(l + 1)
            ag_next, token = gather_begin("ag%d" % (l + 1), next_ops, W_in[l])
        else:
            token = None
        qkv = matmul("in_qkv", h, W_in[l], "nn", BF16, n=Dqkv, deps=after_token(token),
                     epilogue=rope_heads, row_extras=[t_cos, t_lo, t_hi])
        p_glu = matmul("in_glu", h, W_in[l], "nn", F32, b_off=Dqkv, n=2 * D)
        p_gate = matmul("in_gate", h, W_in[l], "nn", F32, b_off=Dqkv + 2 * D, n=2 * D)
        sinkcol = jnp.repeat(sink[l].reshape(nkv, GQA_GROUP), BLOCK, axis=1).reshape(nkv, GQA_GROUP * BLOCK, 1)
        att = attn_fwd(qkv, sinkcol, S, D, Dkv)
        uc = conv_fwd(p_glu, W_dw[l], S, D)
        token = None
        if l == 0:
            ag_rest, _ = gather_forward("ag0b", rest_ops, ag_rest, uc)
            W_gu[0], W_oa[0], W_ob[0], W_out[0], W_down[0] = gather_end("ag0b", ag_rest, uc)
            if L > 1:
                next_ops = gather_first(1) + gather_rest(1)
                ag_next, token = gather_begin("ag1", next_ops, W_gu[0])
        y_a = matmul("oa", att, W_oa[l], "nn", F32, deps=after_token(token))
        z = rowwise("conv_ln", f_convln, S, [_ri(uc)], [vec(conv_ln_g, l), vec(conv_ln_b, l)], [(BF16, 1)], W=D)[0]
        y_b = matmul("ob", z, W_ob[l], "nn", F32)
        merged = rowwise("merge", f_merge, S, [_ri(p_gate, col=0), _ri(p_gate, col=1), _ri(y_a), _ri(y_b)], [],
                         [(BF16, 1)], W=D, cw=_pick(D, 512))[0]
        r1 = matmul("out", merged, W_out[l], "nn", F32)
        x1, h2 = rowwise("ln1", f_ln, S, [_ri(xl), _ri(r1)], [gt_a, vec(ln1_g, l), vec(ln1_b, l), sc_f, sh_f],
                         [(F32, 1), (BF16, 1)], W=D)
        act, gate, up = matmul_swiglu(h2, W_gu[l], DFF)
        token = None
        if l + 1 < L:
            ag_next, token = gather_forward("ag%d" % (l + 1), next_ops, ag_next, act)
        f = matmul("down", act, W_down[l], "nn", F32, deps=after_token(token))
        nsc, nsh = (mods[l + 1][1], mods[l + 1][0]) if l + 1 < L else (sc_a, sh_a)
        x2, h_next = rowwise("ln2", f_ln, S, [_ri(x1), _ri(f)], [gt_f, vec(ln2_g, l), vec(ln2_b, l), nsc, nsh],
                             [(F32, 1), (BF16, 1)], W=D)
        saved.append(dict(x=xl, h=h, qkv=qkv, sinkcol=sinkcol, att=att, p_glu=p_glu, p_gate=p_gate, y_a=y_a, y_b=y_b,
                          uc=uc, z=z, merged=merged, r1=r1, x1=x1, h2=h2, gate=gate, up=up, act=act, f=f))
        xl, h = x2, h_next
        if l + 1 < L:
            (W_in[l + 1], W_dw[l + 1], W_gu[l + 1], W_oa[l + 1], W_ob[l + 1], W_out[l + 1],
             W_down[l + 1]) = gather_end("ag%d" % (l + 1), ag_next, x2)

    def f_loss(rows, vecs, c_, i_):
        y, t = rows
        e = y - t
        return [[e * (1.0 / D)]], [e * e]

    dy, err = rowwise("loss", f_loss, S, [_ri(xl), _ri(target)], [], [(F32, 1)], 1, W=D)
    loss = lax.psum(0.5 * jnp.sum(err) / D, AXES)

    def f_ln_bwd_last(rows, vecs, c_, i_):
        (dout, xin, r), (gt, g) = rows, vecs
        xhat, rstd = _ln_stats(alpha * xin + (1.0 + gt) * r)
        dt = _ln_bwd(dout * g, xhat, rstd)
        return [[(1.0 + gt) * dt], [alpha * dt]], [dout * xhat, dout, dt * r]

    def f_ln_bwd(rows, vecs, c_, i_):
        (dres, dh, xout, xin, r), (sc, gt, g) = rows, vecs
        dout = dres + dh * (1.0 + sc)
        xhat, rstd = _ln_stats(alpha * xin + (1.0 + gt) * r)
        dt = _ln_bwd(dout * g, xhat, rstd)
        return [[(1.0 + gt) * dt], [alpha * dt]], [dout * xhat, dout, dt * r, dh * xout, dh]

    def f_swiglu_bwd(rows, vecs, c_, i_):
        da, gate, up = rows[0], rows[1].astype(F32), rows[2].astype(F32)
        sg = _sig(gate)
        return [[da * up * sg * (1.0 + gate * (1.0 - sg)), da * (gate * sg)]], []

    def f_gate_bwd(rows, vecs, c_, i_):
        dm, g_a, g_b, y_a, y_b = rows
        sa, sb = _sig(g_a), _sig(g_b)
        return [[dm * sa], [dm * sb], [dm * y_a * sa * (1.0 - sa), dm * y_b * sb * (1.0 - sb)]], []

    def f_convln_bwd(rows, vecs, c_, i_):
        (dz, uc), (g, b) = rows, vecs
        xhat, rstd = _ln_stats(uc)
        nrm = xhat * g + b
        sg = _sig(nrm)
        dn = dz * sg * (1.0 + nrm * (1.0 - sg))
        return [[_ln_bwd(dn * g, xhat, rstd)]], [dn * xhat, dn]

    def f_kv_combine(rows, vecs, c_, i_):
        k_lo, k_mid, k_hi, v_lo, v_mid, v_hi, tc_, tl, th = rows
        lo, hi = i_ > 0, i_ < nb - 1
        dk = jnp.where(lo, k_lo, 0.0) + k_mid + jnp.where(hi, k_hi, 0.0)
        dv = jnp.where(lo, v_lo, 0.0) + v_mid + jnp.where(hi, v_hi, 0.0)
        return [[_rope(dk, tc_, tl, th, -1.0)], [dv]], []

    def f_mod_bwd(rows, vecs, c_, i_):
        (dres, dh, xv), (sc,) = rows, vecs
        return [[dres + dh * (1.0 + sc)]], [dh * xv, dh]

    def flat(a):
        return a.reshape(-1, a.shape[-1])

    big_names = ("w_in", "w_gu", "w_oa", "w_ob", "w_out", "w_down")
    big_w = dict(w_in=w_in, w_gu=w_gu, w_oa=w_oa, w_ob=w_ob, w_out=w_out, w_down=w_down)
    big_m = dict(w_in=m_w_in, w_gu=m_w_gu, w_oa=m_w_oa, w_ob=m_w_ob, w_out=m_w_out, w_down=m_w_down)
    big_v = dict(w_in=v_w_in, w_gu=v_w_gu, w_oa=v_w_oa, w_ob=v_w_ob, w_out=v_w_out, w_down=v_w_down)
    big_res = {nm: None for nm in big_names}
    dmod = [None] * L
    small = dict(sink=[None] * L, conv_ln_g=[None] * L, conv_ln_b=[None] * L, ln1_g=[None] * L, ln1_b=[None] * L,
                 ln2_g=[None] * L, ln2_b=[None] * L)
    dwdw = [None] * L

    first, second, last = ("w_down", "w_gu"), ("w_out", "w_oa", "w_ob"), ("w_in",)

    def finish(names, tag, handle, after, l):
        landed = exchange_wait(tag + "_wait", handle, after)[len(names):]
        for nm, land in zip(names, landed):
            big_res[nm] = adamw_update("adamw_" + nm, big_w[nm], big_m[nm], big_v[nm], layer=l, landed=land,
                                       into=big_res[nm])
        return landed[0]

    dres, dh_next = dy, None
    rs_last, token = None, None
    for l in reversed(range(L)):
        sv = saved[l]
        sh_a, sc_a, gt_a, sh_f, sc_f, gt_f = mods[l]
        if dh_next is None:
            df, dres, d_g2, d_b2, d_gtf = rowwise(
                "ln2_bwd_last", f_ln_bwd_last, S, [_ri(dres), _ri(sv["x1"]), _ri(sv["f"])], [gt_f, vec(ln2_g, l)],
                [(BF16, 1), (F32, 1)], 3, W=D)
            d_sca_next = d_sha_next = None
        else:
            df, dres, d_g2, d_b2, d_gtf, d_sca_next, d_sha_next = rowwise(
                "ln2_bwd", f_ln_bwd, S, [_ri(dres), _ri(dh_next), _ri(saved[l + 1]["x"]), _ri(sv["x1"]), _ri(sv["f"])],
                [mods[l + 1][1], gt_f, vec(ln2_g, l)], [(BF16, 1), (F32, 1)], 5, W=D)
            dmod[l + 1][1], dmod[l + 1][0] = d_sca_next, d_sha_next
        dmod[l] = [None] * N_MOD
        dmod[l][5] = d_gtf
        small["ln2_g"][l], small["ln2_b"][l] = d_g2, d_b2
        g_down = matmul("d_w_down", sv["act"], df, "tn", BF16, deps=after_token(token))
        dact = matmul("d_act", df, W_down[l], "nt", F32)
        dgu = rowwise("swiglu_bwd", f_swiglu_bwd, S, [_ri(dact), _ri(sv["gate"]), _ri(sv["up"])], [],
                      [(BF16, 2)], W=DFF, tr=128, cw=_pick(DFF, 512))[0]
        g_gu = matmul("d_w_gu", sv["h2"], dgu, "tn", BF16)
        rs_first, token = begin("rsa%d" % l, [op_scatter_axis(g_down, 0), op_scatter_axis(g_gu, 1)], g_gu)
        dh2 = matmul("d_h2", dgu, W_gu[l], "nt", F32, deps=[token])
        dr1, dres, d_g1, d_b1, d_gta, d_scf, d_shf = rowwise(
            "ln1_bwd", f_ln_bwd, S, [_ri(dres), _ri(dh2), _ri(sv["x1"]), _ri(sv["x"]), _ri(sv["r1"])],
            [sc_f, gt_a, vec(ln1_g, l)], [(BF16, 1), (F32, 1)], 5, W=D)
        dmod[l][2], dmod[l][4], dmod[l][3] = d_gta, d_scf, d_shf
        small["ln1_g"][l], small["ln1_b"][l] = d_g1, d_b1
        g_out = matmul("d_w_out", sv["merged"], dr1, "tn", BF16)
        dmerged = matmul("d_merged", dr1, W_out[l], "nt", F32)
        if rs_last is not None:
            finish(last, "rsc%d" % (l + 1), rs_last, dmerged, l + 1)
        dy_a, dy_b, dp_gate = rowwise(
            "gate_bwd", f_gate_bwd, S,
            [_ri(dmerged), _ri(sv["p_gate"], col=0), _ri(sv["p_gate"], col=1), _ri(sv["y_a"]), _ri(sv["y_b"])], [],
            [(BF16, 1), (BF16, 1), (BF16, 2)], W=D, cw=_pick(D, 512))
        g_oa = matmul("d_w_oa", sv["att"], dy_a, "tn", BF16)
        datt = matmul("d_att", dy_a, W_oa[l], "nt", BF16)
        g_ob = matmul("d_w_ob", sv["z"], dy_b, "tn", BF16)
        dz = matmul("d_z", dy_b, W_ob[l], "nt", F32)
        duc, d_cg, d_cb = rowwise("conv_ln_bwd", f_convln_bwd, S, [_ri(dz), _ri(sv["uc"])],
                                  [vec(conv_ln_g, l), vec(conv_ln_b, l)], [(F32, 1)], 2, W=D)
        small["conv_ln_g"][l], small["conv_ln_b"][l] = d_cg, d_cb
        dga, dgb, dwdw[l] = conv_bwd(duc, sv["p_glu"], W_dw[l], S, D)
        before = finish(first, "rsa%d" % l, rs_first, dga, l)
        rs_second, token = begin("rsb%d" % l, [op_scatter_axis(g_out, 0), op_scatter_axis(g_oa, 0),
                                               op_scatter_axis(g_ob, 0)], before)
        dq_r, dkp, dvp, dsink = attn_bwd(sv["qkv"], sv["sinkcol"], datt, (t_cos, t_lo, t_hi), S, D, Dkv)
        small["sink"][l] = dsink[:, :GQA_GROUP, 0].reshape(1, nq)
        dk_r, dv_r = rowwise(
            "kv_combine", f_kv_combine, S,
            [_ri(dkp, lead=2, shift=-1), _ri(dkp, lead=1), _ri(dkp, lead=0, shift=1),
             _ri(dvp, lead=2, shift=-1), _ri(dvp, lead=1), _ri(dvp, lead=0, shift=1),
             _ri(t_cos, whole=True), _ri(t_lo, whole=True), _ri(t_hi, whole=True)],
            [], [(BF16, 1), (BF16, 1)], W=Dkv, tr=BLOCK, cw=HEAD_DIM, rc=32)
        dp = jnp.concatenate([dq_r, dk_r, dv_r, dga, dgb, dp_gate], axis=1)
        g_in = matmul("d_w_in", sv["h"], dp, "tn", BF16, deps=[token])
        rs_last, token = begin("rsc%d" % l, [op_scatter_axis(g_in, 1)], g_in)
        dh_next = matmul("d_h", dp, W_in[l], "nt", F32, deps=[token])
        finish(second, "rsb%d" % l, rs_second, dh_next, l)
        token = None

    grad_x, d_sca0, d_sha0 = rowwise("mod_bwd", f_mod_bwd, S, [_ri(dres), _ri(dh_next), _ri(x0)], [mods[0][1]],
                                     [(F32, 1)], 2, W=D)
    dmod[0][1], dmod[0][0] = d_sca0, d_sha0
    finish(last, "rsc0", rs_last, grad_x, 0)

    small_names = ("b_ada", "sink", "conv_ln_g", "conv_ln_b", "ln1_g", "ln1_b", "ln2_g", "ln2_b")
    small_w = dict(b_ada=b_ada, sink=sink, conv_ln_g=conv_ln_g, conv_ln_b=conv_ln_b, ln1_g=ln1_g, ln1_b=ln1_b,
                   ln2_g=ln2_g, ln2_b=ln2_b)
    small_m = dict(b_ada=m_b_ada, sink=m_sink, conv_ln_g=m_conv_ln_g, conv_ln_b=m_conv_ln_b, ln1_g=m_ln1_g,
                   ln1_b=m_ln1_b, ln2_g=m_ln2_g, ln2_b=m_ln2_b)
    small_v = dict(b_ada=v_b_ada, sink=v_sink, conv_ln_g=v_conv_ln_g, conv_ln_b=v_conv_ln_b, ln1_g=v_ln1_g,
                   ln1_b=v_ln1_b, ln2_g=v_ln2_g, ln2_b=v_ln2_b)
    small_g = dict(small)
    small_g["b_ada"] = [jnp.concatenate(dmod[l], axis=1) for l in range(L)]
    sizes = [small_w[nm].size for nm in small_names]
    total = sum(sizes)
    padded = -(-total // (SMALL_W * ROW_CHUNK)) * (SMALL_W * ROW_CHUNK)

    def pack(parts):
        flat_ = jnp.concatenate([p.reshape(-1) for p in parts] + [jnp.zeros((padded - total,), F32)])
        return flat_.reshape(padded // SMALL_W, SMALL_W)

    g_pack = pack([jnp.concatenate(small_g[nm], axis=0) for nm in small_names])
    small_land, dwdw_land = exchange("gather_small", [op_gather_stack(g_pack),
                                                      op_scatter_axis(jnp.stack(dwdw, axis=0), 2)])
    small_out = adamw_update("adamw_small", pack([small_w[nm] for nm in small_names]),
                             pack([small_m[nm] for nm in small_names]), pack([small_v[nm] for nm in small_names]),
                             landed=small_land)

    def unpack(buf):
        flat_, out, o = buf.reshape(-1), {}, 0
        for nm, sz in zip(small_names, sizes):
            out[nm] = flat_[o:o + sz].reshape(small_w[nm].shape)
            o += sz
        return out
    small_res = [unpack(b) for b in small_out]

    dw_cols = w_dw.shape[2]

    def pad_dw(a):
        return jnp.pad(a, ((0, 0), (0, 32 - CONV_WIDTH), (0, 0))).reshape(L * 32, dw_cols)
    dw_out = adamw_update("adamw_w_dw", pad_dw(w_dw), pad_dw(m_w_dw), pad_dw(v_w_dw),
                          landed=dwdw_land.reshape(N_DEV, L * 32, dw_cols))
    dw_res = [a.reshape(L, 32, dw_cols)[:, :CONV_WIDTH] for a in dw_out]

    dmod_all = small_land.reshape(N_DEV, -1)[:, :L * N_MOD * D].reshape(N_DEV, L, N_MOD * D)
    dmod_mine = lax.dynamic_slice_in_dim(dmod_all, me * ada_cols, ada_cols, axis=2)
    dmod_pad = jnp.concatenate([dmod_mine, jnp.zeros_like(dmod_mine)], axis=0).astype(BF16)
    g_ada = jnp.stack([matmul("d_w_ada", c_pad, dmod_pad[:, l], "tn", F32) for l in range(L)], axis=0)
    ada_out = adamw_update("adamw_w_ada", flat(w_ada), flat(m_w_ada), flat(v_w_ada), grad=flat(g_ada))
    ada_res = [a.reshape(w_ada.shape) for a in ada_out]

    order = ("w_ada", "b_ada", "w_in", "sink", "w_dw", "conv_ln_g", "conv_ln_b", "w_oa", "w_ob", "w_out",
             "ln1_g", "ln1_b", "w_gu", "w_down", "ln2_g", "ln2_b")

    def result(nm, j):
        if nm == "w_ada":
            return ada_res[j]
        if nm == "w_dw":
            return dw_res[j]
        if nm in big_res:
            return big_res[nm][j]
        return small_res[j][nm]

    outs = [loss, grad_x.reshape(x.shape)]
    for j in range(4):
        outs += [result(nm, j) for nm in order]
    return tuple(outs)
```

```python
import math

import jax
import jax.numpy as jnp
from jax import lax
from jax.experimental import pallas as pl
from jax.experimental.pallas import tpu as pltpu

F32 = jnp.float32
BF16 = jnp.bfloat16
N_DEV = 8
AXES = ("x", "y", "c")
HEAD_DIM = 128
GQA_GROUP = 4
BLOCK = 128
ROPE_DIM = HEAD_DIM // 4
ROPE_HALF = ROPE_DIM // 2
ROPE_THETA = 500000.0
CONV_WIDTH = 31
CONV_HALO = 16
N_MOD = 6
LN_EPS = 1e-5
NEG_INF = -1e30
ADAM_LR, ADAM_B1, ADAM_B2, ADAM_EPS, ADAM_WD, ADAM_STEP = 0.001, 0.9, 0.999, 1e-08, 0.01, 10
VMEM_LIMIT = 56 * 1024 * 1024
LANE = 128
ROW_CHUNK = 16
SMALL_W = 512


def _params(*sem):
    return pltpu.CompilerParams(dimension_semantics=sem, vmem_limit_bytes=VMEM_LIMIT)


def _pick(dim, pref, mult=LANE):
    if dim <= pref:
        return dim
    best = None
    for t in range(mult, pref + 1, mult):
        if dim % t == 0:
            best = t
    assert best is not None, (dim, pref)
    return best


def matmul(name, a, b, mode, out_dtype, *, b_off=0, n=None, tm=1024, tn=1408, tk=2816, deps=(), epilogue=None,
           row_extras=()):
    if mode == "nn":
        (M, K), N = a.shape, (n or b.shape[1])
    elif mode == "tn":
        (K, M), N = a.shape, b.shape[1]
    else:
        (M, K), N = a.shape, b.shape[0]
    if mode == "tn":
        tn, tk = min(tn, 1024), max(tk, 4096)
    tm = _pick(M, tn, LANE) if mode == "tn" else _pick(M, tm, 16)
    tn, tk = _pick(math.gcd(N, b_off) if b_off else N, tn), _pick(K, tk)
    assert b_off % tn == 0 and N % tn == 0
    boff = b_off // tn
    nk = K // tk
    if mode == "nn":
        a_spec = pl.BlockSpec((tm, tk), lambda i, j, k: (i, k))
        b_spec = pl.BlockSpec((tk, tn), lambda i, j, k: (k, j + boff))
        dims = (((1,), (0,)), ((), ()))
    elif mode == "tn":
        a_spec = pl.BlockSpec((tk, tm), lambda i, j, k: (k, i))
        b_spec = pl.BlockSpec((tk, tn), lambda i, j, k: (k, j))
        dims = (((0,), (0,)), ((), ()))
    else:
        a_spec = pl.BlockSpec((tm, tk), lambda i, j, k: (i, k))
        b_spec = pl.BlockSpec((tn, tk), lambda i, j, k: (j, k))
        dims = (((1,), (1,)), ((), ()))
    n_ex, n_dep = len(row_extras), len(deps)
    assert epilogue is None or nk == 1

    def body(a_ref, b_ref, *rest):
        o_ref = rest[n_ex + n_dep]

        def dot():
            return lax.dot_general(a_ref[...].astype(BF16), b_ref[...].astype(BF16), dims, preferred_element_type=F32)
        if epilogue is not None:
            o_ref[...] = epilogue(dot(), [r[...] for r in rest[:n_ex]], pl.program_id(1)).astype(out_dtype)
        elif nk == 1:
            o_ref[...] = dot().astype(out_dtype)
        else:
            acc_ref = rest[n_ex + n_dep + 1]
            k = pl.program_id(2)

            @pl.when(k == 0)
            def _():
                acc_ref[...] = jnp.zeros_like(acc_ref)
            acc_ref[...] += dot()

            @pl.when(k == nk - 1)
            def _():
                o_ref[...] = acc_ref[...].astype(out_dtype)

    return pl.pallas_call(
        body, name=name, grid=(M // tm, N // tn, nk),
        in_specs=[a_spec, b_spec] + [pl.BlockSpec((tm, e.shape[1]), lambda i, j, k: (i, 0)) for e in row_extras]
        + [pl.BlockSpec(memory_space=pl.ANY)] * n_dep,
        out_specs=pl.BlockSpec((tm, tn), lambda i, j, k: (i, j)),
        out_shape=jax.ShapeDtypeStruct((M, N), out_dtype),
        scratch_shapes=[pltpu.VMEM((tm, tn), F32)] if nk > 1 else [],
        compiler_params=_params("parallel", "parallel", "arbitrary"),
    )(a, b, *row_extras, *deps)


def matmul_swiglu(h, w_gu, dff, *, tm=1024, tn=512):
    M, K = h.shape
    tm, tn = _pick(M, tm, 16), _pick(dff, tn)
    nj = dff // tn

    def body(a_ref, bg_ref, bu_ref, act_ref, gate_ref, up_ref):
        a = a_ref[...]
        gate = jnp.dot(a, bg_ref[...], preferred_element_type=F32)
        up = jnp.dot(a, bu_ref[...], preferred_element_type=F32)
        act_ref[...] = (gate * _sig(gate) * up).astype(BF16)
        gate_ref[...] = gate.astype(BF16)
        up_ref[...] = up.astype(BF16)

    out_spec = pl.BlockSpec((tm, tn), lambda i, j: (i, j))
    return pl.pallas_call(
        body, name="gu_swiglu", grid=(M // tm, nj),
        in_specs=[pl.BlockSpec((tm, K), lambda i, j: (i, 0)), pl.BlockSpec((K, tn), lambda i, j: (0, j)),
                  pl.BlockSpec((K, tn), lambda i, j: (0, nj + j))],
        out_specs=[out_spec] * 3,
        out_shape=[jax.ShapeDtypeStruct((M, dff), BF16)] * 3,
        compiler_params=_params("parallel", "parallel"),
    )(h, w_gu, w_gu)


def rowwise(name, fn, nrows, row_ins, vec_ins, row_outs, n_sums=0, *, W, tr=256, cw=None, rc=ROW_CHUNK,
            out_layer=None, into=None, lazy=False):
    cw = cw or W
    tr = _pick(nrows, tr, ROW_CHUNK)
    rc = min(rc, tr)
    assert W % cw == 0 and nrows % ROW_CHUNK == 0 and tr % rc == 0 and rc % 8 == 0
    nrt = nrows // tr
    n_ri, n_v, n_ro = len(row_ins), len(vec_ins), len(row_outs)

    def row_spec(lead, colblk, shift, whole):
        w = cw if whole else W

        def rmap(i):
            return jnp.clip(i + shift, 0, nrt - 1) if shift else i
        if lead is None:
            return pl.BlockSpec((tr, w), lambda i: (rmap(i), colblk))
        return pl.BlockSpec((None, tr, w), lambda i: (lead, rmap(i), colblk))

    n_prev = len(into) if into is not None else 0
    in_specs = [row_spec(*ri[1:]) for ri in row_ins]
    in_specs += [pl.BlockSpec((1, W), lambda i: (0, 0)) for _ in vec_ins]
    in_specs += [pl.BlockSpec(memory_space=pl.ANY)] * n_prev
    if out_layer is None:
        out_specs = [pl.BlockSpec((tr, p * W), lambda i: (i, 0)) for _, p in row_outs]
        out_shape = [jax.ShapeDtypeStruct((nrows, p * W), dt) for dt, p in row_outs]
    else:
        out_specs = [pl.BlockSpec((None, tr, p * W), lambda i: (out_layer[0], i, 0)) for _, p in row_outs]
        out_shape = [jax.ShapeDtypeStruct((out_layer[1], nrows, p * W), dt) for dt, p in row_outs]
    out_specs += [pl.BlockSpec((1, W), lambda i: (0, 0)) for _ in range(n_sums)]
    out_shape += [jax.ShapeDtypeStruct((1, W), F32) for _ in range(n_sums)]

    def body(*refs):
        rin, vin = refs[:n_ri], refs[n_ri:n_ri + n_v]
        refs = refs[n_prev:]
        rout = refs[n_ri + n_v:n_ri + n_v + n_ro]
        sout = refs[n_ri + n_v + n_ro:n_ri + n_v + n_ro + n_sums]
        acc = refs[n_ri + n_v + n_ro + n_sums:]
        i = pl.program_id(0)
        if n_sums:
            @pl.when(i == 0)
            def _():
                for a in acc:
                    a[...] = jnp.zeros_like(a)
        def fold8(s):
            part = s[0:8]
            for q in range(1, rc // 8):
                part = part + s[8 * q:8 * q + 8]
            return part

        def lazy_step(r, carry):
            r0 = pl.multiple_of(r * rc, rc)

            def load(idx, c0, w):
                return rin[idx][pl.ds(r0, rc), c0:c0 + w]

            def vec(idx, c0, w):
                return vin[idx][:, c0:c0 + w]

            def store(o, piece, c0, val):
                rout[o][pl.ds(r0, rc), piece * W + c0:piece * W + c0 + val.shape[1]] = val.astype(row_outs[o][0])

            def add_sum(k, c0, val):
                acc[k][:, c0:c0 + val.shape[1]] += fold8(val)
            fn(load, vec, store, add_sum, i)
            return carry
        if lazy:
            lax.fori_loop(0, tr // rc, lazy_step, 0, unroll=2)
        for c in range(0 if lazy else W // cw):
            c0 = c * cw
            vecs = [v[:, c0:c0 + cw] for v in vin]

            def step(r, carry, c=c, c0=c0, vecs=vecs):
                r0 = pl.multiple_of(r * rc, rc)
                rows = [ref[pl.ds(r0, rc), :] if ri[4] else ref[pl.ds(r0, rc), c0:c0 + cw]
                        for ref, ri in zip(rin, row_ins)]
                outs, sums = fn(rows, vecs, c, i)
                for oref, (dt, _), pieces in zip(rout, row_outs, outs):
                    for pi, piece in enumerate(pieces):
                        oref[pl.ds(r0, rc), pi * W + c0:pi * W + c0 + cw] = piece.astype(dt)
                for a, s in zip(acc, sums):
                    a[:, c0:c0 + cw] += fold8(s)
                return carry
            lax.fori_loop(0, tr // rc, step, 0)
        if n_sums:
            @pl.when(i == nrt - 1)
            def _():
                for o, a in zip(sout, acc):
                    o[...] = jnp.sum(a[...], axis=0, keepdims=True)

    res = pl.pallas_call(
        body, name=name, grid=(nrt,), in_specs=in_specs, out_specs=out_specs, out_shape=out_shape,
        scratch_shapes=[pltpu.VMEM((8, W), F32) for _ in range(n_sums)],
        input_output_aliases={n_ri + n_v + q: q for q in range(n_prev)},
        compiler_params=_params("arbitrary"),
    )(*[ri[0] for ri in row_ins], *vec_ins, *(into or ()))
    return res


def _ri(arr, lead=None, col=0, shift=0, whole=False):
    return (arr, lead, col, shift, whole)


def _sig(x):
    return jax.nn.sigmoid(x)


LN_CHUNK = 256
LN_ROWS = 64


def _row_mean(make, width, cw):
    acc = None
    for c0 in range(0, width, cw):
        t = make(c0)
        for q in range(cw // LANE):
            part = t[:, q * LANE:(q + 1) * LANE]
            acc = part if acc is None else acc + part
    return jnp.sum(acc, axis=-1, keepdims=True) * (1.0 / width)


def _ln_stats(t_of, width, cw):
    mu = _row_mean(t_of, width, cw)

    def sq(c0):
        d = t_of(c0) - mu
        return d * d
    return mu, lax.rsqrt(_row_mean(sq, width, cw) + LN_EPS)


def _rope(x, cos, s_lo, s_hi, sign):
    up = pltpu.roll(x, HEAD_DIM - ROPE_HALF, 1)
    down = pltpu.roll(x, ROPE_HALF, 1)
    return x * cos + sign * (up * s_lo + down * s_hi)


def rope_tables(S):
    pos = jnp.arange(S, dtype=F32)
    inv_freq = ROPE_THETA ** (-jnp.arange(0, ROPE_DIM, 2, dtype=F32) / ROPE_DIM)
    ang = pos[:, None] * inv_freq[None, :]
    cos, sin = jnp.cos(ang), jnp.sin(ang)
    ones = jnp.ones((S, HEAD_DIM - ROPE_DIM), F32)
    zeros = jnp.zeros((S, HEAD_DIM - ROPE_DIM), F32)
    zh = jnp.zeros((S, ROPE_HALF), F32)
    t_cos = jnp.concatenate([cos, cos, ones], axis=1)
    t_lo = jnp.concatenate([-sin, zh, zeros], axis=1)
    t_hi = jnp.concatenate([zh, sin, zeros], axis=1)
    return t_cos, t_lo, t_hi


ATTN_HEADS_PER_STEP = 4


def _attn_specs(S, D, Dkv):
    nb, nkv, qb = S // BLOCK, Dkv // HEAD_DIM, D // HEAD_DIM
    hps = math.gcd(ATTN_HEADS_PER_STEP, nkv)
    assert qb % hps == 0 and (qb + nkv) % hps == 0
    gw = GQA_GROUP * HEAD_DIM
    q_spec = pl.BlockSpec((BLOCK, hps * gw), lambda h, n: (n, h))

    def band(col0):
        c0 = col0 // hps
        return [pl.BlockSpec((BLOCK, hps * HEAD_DIM), lambda h, n: (jnp.maximum(n - 1, 0), c0 + h)),
                pl.BlockSpec((BLOCK, hps * HEAD_DIM), lambda h, n: (n, c0 + h)),
                pl.BlockSpec((BLOCK, hps * HEAD_DIM), lambda h, n: (jnp.minimum(n + 1, nb - 1), c0 + h))]
    sink_spec = pl.BlockSpec((hps, GQA_GROUP * BLOCK, 1), lambda h, n: (h, 0, 0))
    return nb, nkv, hps, gw, q_spec, band(qb), band(qb + nkv), sink_spec


def _attn_valid(n, S):
    shape = (GQA_GROUP * BLOCK, 3 * BLOCK)
    row = lax.broadcasted_iota(jnp.int32, shape, 0) & (BLOCK - 1)
    col = lax.broadcasted_iota(jnp.int32, shape, 1)
    rel = col - BLOCK - row
    kpos = (n - 1) * BLOCK + col
    return (jnp.abs(rel) <= BLOCK) & (kpos >= 0) & (kpos < S)


def _head(ref, hh, width):
    return ref[:, hh * width:(hh + 1) * width]


def _attn_probs(q, k_blocks, sink, valid):
    qs = jnp.concatenate([q[:, g * HEAD_DIM:(g + 1) * HEAD_DIM] for g in range(GQA_GROUP)], axis=0)
    kb = jnp.concatenate(k_blocks, axis=0)
    s = lax.dot_general(qs, kb, (((1,), (1,)), ((), ())), preferred_element_type=F32) * (HEAD_DIM ** -0.5)
    s = jnp.where(valid, s, NEG_INF)
    m = jnp.maximum(jnp.max(s, axis=-1, keepdims=True), sink)
    p = jnp.exp(s - m)
    e_sink = jnp.exp(sink - m)
    denom = jnp.sum(p, axis=-1, keepdims=True) + e_sink
    return qs, kb, p / denom, e_sink / denom


def attn_fwd(qkv, sinkcol, S, D, Dkv):
    nb, nkv, hps, gw, q_spec, k_specs, v_specs, sink_spec = _attn_specs(S, D, Dkv)

    def body(q_ref, k0, k1, k2, v0, v1, v2, sink_ref, o_ref):
        valid = _attn_valid(pl.program_id(1), S)
        for hh in range(hps):
            _, _, w, _ = _attn_probs(_head(q_ref, hh, gw), [_head(r, hh, HEAD_DIM) for r in (k0, k1, k2)],
                                     sink_ref[hh], valid)
            vb = jnp.concatenate([_head(r, hh, HEAD_DIM) for r in (v0, v1, v2)], axis=0)
            o = jnp.dot(w.astype(BF16), vb, preferred_element_type=F32)
            for g in range(GQA_GROUP):
                c0 = hh * gw + g * HEAD_DIM
                o_ref[:, c0:c0 + HEAD_DIM] = o[g * BLOCK:(g + 1) * BLOCK].astype(BF16)

    return pl.pallas_call(
        body, name="attn_fwd", grid=(nkv // hps, nb),
        in_specs=[q_spec, *k_specs, *v_specs, sink_spec],
        out_specs=pl.BlockSpec((BLOCK, hps * gw), lambda h, n: (n, h)),
        out_shape=jax.ShapeDtypeStruct((S, D), BF16),
        compiler_params=_params("parallel", "arbitrary"),
    )(qkv, qkv, qkv, qkv, qkv, qkv, qkv, sinkcol)


def attn_bwd(qkv, sinkcol, datt, tables, S, D, Dkv):
    nb, nkv, hps, gw, q_spec, k_specs, v_specs, sink_spec = _attn_specs(S, D, Dkv)

    def body(q_ref, k0, k1, k2, v0, v1, v2, sink_ref, do_ref, tc_ref, tl_ref, th_ref, dq_ref, dkp_ref, dvp_ref,
             dsink_ref):
        n = pl.program_id(1)
        valid = _attn_valid(n, S)
        tc_, tl, th = tc_ref[...], tl_ref[...], th_ref[...]

        @pl.when(n == 0)
        def _():
            dsink_ref[...] = jnp.zeros_like(dsink_ref)
        for hh in range(hps):
            qs, kb, w, w_sink = _attn_probs(_head(q_ref, hh, gw), [_head(r, hh, HEAD_DIM) for r in (k0, k1, k2)],
                                            sink_ref[hh], valid)
            vb = jnp.concatenate([_head(r, hh, HEAD_DIM) for r in (v0, v1, v2)], axis=0)
            do = _head(do_ref, hh, gw)
            dos = jnp.concatenate([do[:, g * HEAD_DIM:(g + 1) * HEAD_DIM] for g in range(GQA_GROUP)], axis=0)
            dv = lax.dot_general(w.astype(BF16), dos, (((0,), (0,)), ((), ())), preferred_element_type=F32)
            dw = lax.dot_general(dos, vb, (((1,), (1,)), ((), ())), preferred_element_type=F32)
            delta = jnp.sum(w * dw, axis=-1, keepdims=True)
            ds = (w * (dw - delta) * (HEAD_DIM ** -0.5)).astype(BF16)
            dq = jnp.dot(ds, kb, preferred_element_type=F32)
            dk = lax.dot_general(ds, qs, (((0,), (0,)), ((), ())), preferred_element_type=F32)
            for g in range(GQA_GROUP):
                c0 = hh * gw + g * HEAD_DIM
                dq_ref[:, c0:c0 + HEAD_DIM] = _rope(dq[g * BLOCK:(g + 1) * BLOCK], tc_, tl, th, -1.0).astype(BF16)
            for j in range(3):
                dkp_ref[j, :, hh * HEAD_DIM:(hh + 1) * HEAD_DIM] = dk[j * BLOCK:(j + 1) * BLOCK]
                dvp_ref[j, :, hh * HEAD_DIM:(hh + 1) * HEAD_DIM] = dv[j * BLOCK:(j + 1) * BLOCK]
            t = w_sink * delta
            for g in range(GQA_GROUP):
                dsink_ref[hh, g:g + 1, :] -= jnp.sum(t[g * BLOCK:(g + 1) * BLOCK], axis=0, keepdims=True)

    part_spec = pl.BlockSpec((3, BLOCK, hps * HEAD_DIM), lambda h, n: (0, n, h))
    return pl.pallas_call(
        body, name="attn_bwd", grid=(nkv // hps, nb),
        in_specs=[q_spec, *k_specs, *v_specs, sink_spec, pl.BlockSpec((BLOCK, hps * gw), lambda h, n: (n, h))]
        + [pl.BlockSpec((BLOCK, HEAD_DIM), lambda h, n: (n, 0))] * 3,
        out_specs=[pl.BlockSpec((BLOCK, hps * gw), lambda h, n: (n, h)), part_spec, part_spec,
                   pl.BlockSpec((hps, 8, LANE), lambda h, n: (h, 0, 0))],
        out_shape=[jax.ShapeDtypeStruct((S, D), BF16), jax.ShapeDtypeStruct((3, S, Dkv), F32),
                   jax.ShapeDtypeStruct((3, S, Dkv), F32), jax.ShapeDtypeStruct((nkv, 8, LANE), F32)],
        compiler_params=_params("parallel", "arbitrary"),
    )(qkv, qkv, qkv, qkv, qkv, qkv, qkv, sinkcol, datt, *tables)


CONV_TC = 128
CONV_ROWS = 64


def _shift_rows(win, b):
    return win if b == 0 else pltpu.roll(win, win.shape[0] - b, 0)


def _conv_fill_u(ga_ref, gb_ref, upad_ref, S):
    tc = upad_ref.shape[1]
    zero = jnp.zeros((CONV_HALO, tc), F32)
    upad_ref[0:CONV_HALO, :] = zero
    upad_ref[S + CONV_HALO:S + 2 * CONV_HALO, :] = zero

    def fill(r, carry):
        r0 = pl.multiple_of(r * CONV_ROWS, CONV_ROWS)
        upad_ref[pl.ds(r0 + CONV_HALO, CONV_ROWS), :] = ga_ref[pl.ds(r0, CONV_ROWS), :] * _sig(gb_ref[pl.ds(r0, CONV_ROWS), :])
        return carry
    lax.fori_loop(0, S // CONV_ROWS, fill, 0)


def conv_fwd(pglu, wdw, S, D):
    tc = min(CONV_TC, D)
    nct = D // tc

    def body(ga_ref, gb_ref, w_ref, o_ref, upad_ref):
        _conv_fill_u(ga_ref, gb_ref, upad_ref, S)
        w = w_ref[...]

        def step(r, carry):
            r0 = pl.multiple_of(r * CONV_ROWS, CONV_ROWS)
            win = upad_ref[pl.ds(r0, CONV_ROWS + 2 * CONV_HALO), :]
            acc = jnp.zeros((CONV_ROWS, tc), F32)
            for b in range(8):
                shifted = _shift_rows(win, b)
                for k in range(CONV_WIDTH):
                    o = k + CONV_HALO - CONV_WIDTH // 2
                    if o % 8 == b:
                        acc = acc + w[k:k + 1, :] * shifted[o - b:o - b + CONV_ROWS, :]
            o_ref[pl.ds(r0, CONV_ROWS), :] = acc
            return carry
        lax.fori_loop(0, S // CONV_ROWS, step, 0)

    return pl.pallas_call(
        body, name="conv_fwd", grid=(nct,),
        in_specs=[pl.BlockSpec((S, tc), lambda j: (0, j)), pl.BlockSpec((S, tc), lambda j: (0, nct + j)),
                  pl.BlockSpec((CONV_WIDTH, tc), lambda j: (0, j))],
        out_specs=pl.BlockSpec((S, tc), lambda j: (0, j)),
        out_shape=jax.ShapeDtypeStruct((S, D), F32),
        scratch_shapes=[pltpu.VMEM((S + 2 * CONV_HALO, tc), F32)],
        compiler_params=_params("parallel"),
    )(pglu, pglu, wdw)


def conv_bwd(duc, pglu, wdw, S, D):
    tc = min(CONV_TC, D)
    nct = D // tc
    half = CONV_WIDTH // 2

    def body(d_ref, ga_ref, gb_ref, w_ref, dga_ref, dgb_ref, dw_ref, upad_ref, dpad_ref, dwacc_ref):
        _conv_fill_u(ga_ref, gb_ref, upad_ref, S)
        zero = jnp.zeros((CONV_HALO, tc), F32)
        dpad_ref[0:CONV_HALO, :] = zero
        dpad_ref[S + CONV_HALO:S + 2 * CONV_HALO, :] = zero

        def fill(r, carry):
            r0 = pl.multiple_of(r * CONV_ROWS, CONV_ROWS)
            dpad_ref[pl.ds(r0 + CONV_HALO, CONV_ROWS), :] = d_ref[pl.ds(r0, CONV_ROWS), :]
            return carry
        lax.fori_loop(0, S // CONV_ROWS, fill, 0)
        dwacc_ref[...] = jnp.zeros_like(dwacc_ref)
        w = w_ref[...]

        def step(r, carry):
            r0 = pl.multiple_of(r * CONV_ROWS, CONV_ROWS)
            uwin = upad_ref[pl.ds(r0, CONV_ROWS + 2 * CONV_HALO), :]
            dwin = dpad_ref[pl.ds(r0, CONV_ROWS + 2 * CONV_HALO), :]
            d = dwin[CONV_HALO:CONV_HALO + CONV_ROWS, :]
            du = jnp.zeros((CONV_ROWS, tc), F32)
            for b in range(8):
                d_shifted, u_shifted = _shift_rows(dwin, b), _shift_rows(uwin, b)
                for k in range(CONV_WIDTH):
                    o = CONV_HALO + half - k
                    if o % 8 == b:
                        du = du + w[k:k + 1, :] * d_shifted[o - b:o - b + CONV_ROWS, :]
                    o = CONV_HALO + k - half
                    if o % 8 == b:
                        prod = d * u_shifted[o - b:o - b + CONV_ROWS, :]
                        part = prod[0:8]
                        for q in range(1, CONV_ROWS // 8):
                            part = part + prod[8 * q:8 * q + 8]
                        dwacc_ref[k] += part
            ga = ga_ref[pl.ds(r0, CONV_ROWS), :]
            sg = _sig(gb_ref[pl.ds(r0, CONV_ROWS), :])
            dga_ref[pl.ds(r0, CONV_ROWS), :] = (du * sg).astype(BF16)
            dgb_ref[pl.ds(r0, CONV_ROWS), :] = (du * ga * sg * (1.0 - sg)).astype(BF16)
            return carry
        lax.fori_loop(0, S // CONV_ROWS, step, 0)
        dw_ref[...] = jnp.sum(dwacc_ref[...], axis=1)

    return pl.pallas_call(
        body, name="conv_bwd", grid=(nct,),
        in_specs=[pl.BlockSpec((S, tc), lambda j: (0, j)), pl.BlockSpec((S, tc), lambda j: (0, j)),
                  pl.BlockSpec((S, tc), lambda j: (0, nct + j)), pl.BlockSpec((CONV_WIDTH, tc), lambda j: (0, j))],
        out_specs=[pl.BlockSpec((S, tc), lambda j: (0, j)), pl.BlockSpec((S, tc), lambda j: (0, j)),
                   pl.BlockSpec((32, tc), lambda j: (0, j))],
        out_shape=[jax.ShapeDtypeStruct((S, D), BF16), jax.ShapeDtypeStruct((S, D), BF16),
                   jax.ShapeDtypeStruct((32, D), F32)],
        scratch_shapes=[pltpu.VMEM((S + 2 * CONV_HALO, tc), F32), pltpu.VMEM((S + 2 * CONV_HALO, tc), F32),
                        pltpu.VMEM((32, 8, tc), F32)],
        compiler_params=_params("parallel"),
    )(duc, pglu, pglu, wdw)


def exchange(name, ops):
    n = len(ops)

    def body(*refs):
        xs, outs = refs[:n], refs[n:2 * n]
        send, recv, lsem = refs[2 * n:]
        mx, my, mc = lax.axis_index("x"), lax.axis_index("y"), lax.axis_index("c")
        me = 4 * mx + 2 * my + mc
        local = [pltpu.make_async_copy(op[3](xs[i], me), op[4](outs[i], me), lsem.at[i]) for i, op in enumerate(ops)]
        for cp in local:
            cp.start()
        copies = []
        for k in range(1, N_DEV):
            px = 1 - mx if k & 4 else mx
            py = 1 - my if k & 2 else my
            pc = 1 - mc if k & 1 else mc
            peer = 4 * px + 2 * py + pc
            for i, op in enumerate(ops):
                cp = pltpu.make_async_remote_copy(
                    src_ref=op[3](xs[i], peer), dst_ref=op[4](outs[i], me),
                    send_sem=send.at[i, k - 1], recv_sem=recv.at[i, k - 1],
                    device_id=(px, py, pc), device_id_type=pl.DeviceIdType.MESH)
                cp.start()
                copies.append(cp)
        for cp in copies:
            cp.wait()
        for cp in local:
            cp.wait()

    any_spec = pl.BlockSpec(memory_space=pl.ANY)
    return pl.pallas_call(
        body, name=name,
        in_specs=[any_spec] * n, out_specs=[any_spec] * n,
        out_shape=[jax.ShapeDtypeStruct(op[1], op[2]) for op in ops],
        scratch_shapes=[pltpu.SemaphoreType.DMA((n, N_DEV - 1)), pltpu.SemaphoreType.DMA((n, N_DEV - 1)),
                        pltpu.SemaphoreType.DMA((n,))],
        compiler_params=pltpu.CompilerParams(has_side_effects=True),
    )(*[op[0] for op in ops])


_HBM = pl.BlockSpec(memory_space=pltpu.HBM)
_SEM = pl.BlockSpec(memory_space=pltpu.SEMAPHORE)
_EFFECT = pltpu.SideEffectType.DATAFLOW_SIDE_EFFECTING


def _me_and_peers():
    mx, my, mc = lax.axis_index("x"), lax.axis_index("y"), lax.axis_index("c")
    peers = []
    for k in range(1, N_DEV):
        px = 1 - mx if k & 4 else mx
        py = 1 - my if k & 2 else my
        pc = 1 - mc if k & 1 else mc
        peers.append((k, (px, py, pc), 4 * px + 2 * py + pc))
    return 4 * mx + 2 * my + mc, peers


PLACE_STEPS = 4


def exchange_local(name, ops, me_arr, deps=()):
    n = len(ops)
    in_specs, out_specs = [], []
    for op in ops:
        kind, axis = op[5]
        rows, cols = op[0].shape if kind == "gather" else op[1][1:]
        steps = PLACE_STEPS if rows % (PLACE_STEPS * ROW_CHUNK) == 0 else 1
        tr = rows // steps

        def row(i, steps=steps):
            return i if steps > 1 else 0

        def window(i, me, axis=axis, steps=steps):
            return (me[0] * steps + row(i, steps), 0) if axis == 0 else (row(i, steps), me[0])
        if kind == "gather":
            in_specs.append(pl.BlockSpec((tr, cols), lambda i, me, row=row: (row(i), 0)))
            out_specs.append(pl.BlockSpec((tr, cols), window))
        else:
            in_specs.append(pl.BlockSpec((tr, cols), window))
            out_specs.append(pl.BlockSpec((None, tr, cols), lambda i, me, row=row: (me[0], row(i), 0)))

    def body(me_ref, *refs):
        for i in range(n):
            refs[n + len(deps) + i][...] = refs[i][...]

    return pl.pallas_call(
        body, name=name,
        grid_spec=pltpu.PrefetchScalarGridSpec(
            num_scalar_prefetch=1, grid=(PLACE_STEPS,),
            in_specs=in_specs + [pl.BlockSpec(memory_space=pl.ANY)] * len(deps), out_specs=out_specs),
        out_shape=[jax.ShapeDtypeStruct(op[1], op[2]) for op in ops],
        compiler_params=_params("arbitrary"),
    )(me_arr, *[op[0] for op in ops], *deps)


ALL_PEERS = (1, 2, 3, 4, 5, 6, 7)
SIBLING = 1
SAME_CORE = (2, 4, 6)


def _flipped(pos, k):
    p = (1 - pos[0] if k & 4 else pos[0], 1 - pos[1] if k & 2 else pos[1], 1 - pos[2] if k & 1 else pos[2])
    return p, 4 * p[0] + 2 * p[1] + p[2]


def direct_copies(ops, ks):
    n = len(ops)

    def build(refs, send, recv):
        xs, lands = refs[:n], refs[n:2 * n]
        pos = (lax.axis_index("x"), lax.axis_index("y"), lax.axis_index("c"))
        _, me = _flipped(pos, 0)
        copies = []
        for j, k in enumerate(ks):
            peer_id, peer = _flipped(pos, k)
            for i, op in enumerate(ops):
                s = i * len(ks) + j
                copies.append(pltpu.make_async_remote_copy(
                    src_ref=op[3](xs[i], peer), dst_ref=op[4](lands[i], me), send_sem=send.at[s], recv_sem=recv.at[s],
                    device_id=peer_id, device_id_type=pl.DeviceIdType.MESH))
        return copies
    return n * len(ks), build


def forward_copies(ops, ks):
    n = len(ops)

    def build(refs, send, recv):
        lands = refs[:n]
        pos = (lax.axis_index("x"), lax.axis_index("y"), lax.axis_index("c"))
        sibling_id, _ = _flipped(pos, SIBLING)
        copies = []
        for j, k in enumerate(ks):
            _, origin = _flipped(pos, k)
            for i, op in enumerate(ops):
                s = i * len(ks) + j
                region = op[4](lands[i], origin)
                copies.append(pltpu.make_async_remote_copy(
                    src_ref=region, dst_ref=region, send_sem=send.at[s], recv_sem=recv.at[s],
                    device_id=sibling_id, device_id_type=pl.DeviceIdType.MESH))
        return copies
    return n * len(ks), build


def exchange_start(name, arrays, plan):
    n_sems, build = plan
    n = len(arrays)

    def body(*refs):
        for cp in build(refs[:n], refs[n], refs[n + 1]):
            cp.start()
        refs[-1][...] = jnp.zeros_like(refs[-1])

    args = [pltpu.with_memory_space_constraint(a, pltpu.HBM) for a in arrays]
    sems = pltpu.SemaphoreType.DMA((n_sems,))
    outs = pl.pallas_call(
        body, name=name,
        out_shape=(sems, sems, *[pltpu.HBM(a.shape, a.dtype) for a in args], jax.ShapeDtypeStruct((8, LANE), F32)),
        in_specs=[_HBM] * n,
        out_specs=(_SEM, _SEM, *[_HBM] * n, pl.BlockSpec(memory_space=pltpu.VMEM)),
        input_output_aliases={i: 2 + i for i in range(n)},
        compiler_params=pltpu.CompilerParams(has_side_effects=_EFFECT),
    )(*args)
    return (build, outs[0], outs[1], outs[2:2 + n]), outs[-1]


def exchange_wait(name, handle, after):
    build, send, recv, thru = handle
    n = len(thru)

    def body(*refs):
        for cp in build(refs[:n], refs[n], refs[n + 1]):
            cp.wait_send()
            cp.wait_recv()

    return pl.pallas_call(
        body, name=name,
        out_shape=[pltpu.HBM(a.shape, a.dtype) for a in thru],
        in_specs=[_HBM] * n + [_SEM, _SEM, pl.BlockSpec(memory_space=pl.ANY)],
        out_specs=[_HBM] * n,
        input_output_aliases={i: i for i in range(n)},
        compiler_params=pltpu.CompilerParams(has_side_effects=_EFFECT),
    )(*thru, send, recv, after)


def _whole(ref, q):
    return ref


def _slot(ref, q):
    return ref.at[q]


def op_gather_stack(x):
    return (x, (N_DEV,) + x.shape, x.dtype, _whole, _slot)


def op_gather_axis(x, axis):
    size = x.shape[axis]
    shape = x.shape[:axis] + (N_DEV * size,) + x.shape[axis + 1:]

    def dst(ref, q):
        idx = [slice(None)] * len(shape)
        idx[axis] = pl.ds(pl.multiple_of(q * size, size), size)
        return ref.at[tuple(idx)]
    return (x, shape, x.dtype, _whole, dst, ("gather", axis))


def op_scatter_axis(x, axis):
    size = x.shape[axis] // N_DEV
    shape = x.shape[:axis] + (size,) + x.shape[axis + 1:]

    def src(ref, q):
        idx = [slice(None)] * len(shape)
        idx[axis] = pl.ds(pl.multiple_of(q * size, size), size)
        return ref.at[tuple(idx)]
    return (x, (N_DEV,) + shape, x.dtype, src, _slot, ("scatter", axis))


def _adamw(w, g, m, v):
    m = ADAM_B1 * m + (1.0 - ADAM_B1) * g
    v = ADAM_B2 * v + (1.0 - ADAM_B2) * (g * g)
    m_hat = m / (1.0 - ADAM_B1 ** ADAM_STEP)
    v_hat = v / (1.0 - ADAM_B2 ** ADAM_STEP)
    delta = -ADAM_LR * (m_hat / (jnp.sqrt(v_hat) + ADAM_EPS) + ADAM_WD * w)
    return delta, m, v


def adamw_update(name, w, m, v, *, layer=None, landed=None, grad=None, into=None):
    R, W = w.shape[-2:]
    n_g = N_DEV if landed is not None else 1

    def fn(rows, vecs, c, i):
        g = rows[0].astype(F32)
        for q in range(1, n_g):
            g = g + rows[q].astype(F32)
        wv, mv, vv = rows[n_g:]
        delta, m2, v2 = _adamw(wv, g, mv, vv)
        return [[g], [delta], [m2], [v2]], []

    g_ins = [_ri(landed, lead=q) for q in range(N_DEV)] if landed is not None else [_ri(grad)]
    cw = LANE if W % LANE == 0 else W
    return rowwise(name, fn, R, g_ins + [_ri(w, lead=layer), _ri(m, lead=layer), _ri(v, lead=layer)], [],
                   [(F32, 1)] * 4, W=W, tr=128, cw=cw, rc=32,
                   out_layer=None if layer is None else (layer, w.shape[0]), into=into)


def kernel(x, c, w_ada, b_ada, w_in, sink, w_dw, conv_ln_g, conv_ln_b, w_oa, w_ob, w_out, ln1_g, ln1_b, w_gu, w_down, ln2_g, ln2_b, loss_target, m_w_ada, m_b_ada, m_w_in, m_sink, m_w_dw, m_conv_ln_g, m_conv_ln_b, m_w_oa, m_w_ob, m_w_out, m_ln1_g, m_ln1_b, m_w_gu, m_w_down, m_ln2_g, m_ln2_b, v_w_ada, v_b_ada, v_w_in, v_sink, v_w_dw, v_conv_ln_g, v_conv_ln_b, v_w_oa, v_w_ob, v_w_out, v_ln1_g, v_ln1_b, v_w_gu, v_w_down, v_ln2_g, v_ln2_b):
    L = w_ada.shape[0]
    S, D = x.shape[1], x.shape[2]
    Dkv = D // GQA_GROUP
    Dqkv = D + 2 * Dkv
    DFF = w_down.shape[1] * N_DEV
    nq, nkv, nb = D // HEAD_DIM, Dkv // HEAD_DIM, S // BLOCK
    alpha = (2.0 * L) ** 0.25
    me = 4 * lax.axis_index("x") + 2 * lax.axis_index("y") + lax.axis_index("c")
    x0 = x.reshape(S, D)
    target = loss_target.reshape(S, D)
    t_cos, t_lo, t_hi = rope_tables(S)

    c_act = jax.nn.silu(c)
    c_all = exchange("gather_c", [op_gather_stack(c_act)])[0].reshape(N_DEV, D)
    c_pad = jnp.concatenate([c_all, jnp.zeros_like(c_all)], axis=0).astype(BF16)
    ada_cols = w_ada.shape[2]
    mod_part = jnp.stack([matmul("mod_mm", c_pad, w_ada[l], "nn", F32)[:N_DEV] for l in range(L)], axis=1)
    mod_land = exchange("scatter_mod", [op_scatter_axis(mod_part, 0)])[0]
    mod = jnp.transpose(mod_land.reshape(N_DEV, L, ada_cols), (1, 0, 2)).reshape(L, N_MOD * D) + b_ada
    mods = [[mod[l:l + 1, j * D:(j + 1) * D] for j in range(N_MOD)] for l in range(L)]

    me_arr = me.astype(jnp.int32).reshape(1)

    def begin(tag, ops, dep, ks=ALL_PEERS):
        lands = exchange_local(tag + "_local", ops, me_arr, [dep])
        return exchange_start(tag + "_start", [op[0] for op in ops] + list(lands), direct_copies(ops, ks))

    def gather_begin(tag, ops, dep):
        return begin(tag, ops, dep, (SIBLING,) + SAME_CORE)

    def gather_forward(tag, ops, handle, after):
        lands = exchange_wait(tag + "_wait", handle, after)[len(ops):]
        return exchange_start(tag + "_fwd_start", list(lands), forward_copies(ops, SAME_CORE))

    def gather_end(tag, handle, after):
        return exchange_wait(tag + "_fwd_wait", handle, after)

    def gather_first(l):
        return [op_gather_axis(w_in[l].astype(BF16), 1), op_gather_axis(w_dw[l], 1)]

    def gather_rest(l):
        return [op_gather_axis(w_gu[l].astype(BF16), 1), op_gather_axis(w_oa[l].astype(BF16), 0),
                op_gather_axis(w_ob[l].astype(BF16), 0), op_gather_axis(w_out[l].astype(BF16), 0),
                op_gather_axis(w_down[l].astype(BF16), 0)]

    W_in, W_gu, W_oa, W_ob, W_out, W_down, W_dw = ([None] * L for _ in range(7))
    first_ops = gather_first(0)
    ag_first, _ = gather_begin("ag0a", first_ops, mod_land)

    def vec(a, l):
        return a[l:l + 1]

    def f_mod(rows, vecs, c_, i_):
        (xv,), (sc, sh) = rows, vecs
        return [[xv * (1.0 + sc) + sh]], []

    cwl = min(LN_CHUNK, D)
    ln_chunks = range(0, D, cwl)

    def f_convln(load, vec_, store, add_sum, i_):
        mu, rstd = _ln_stats(lambda c0: load(0, c0, cwl), D, cwl)
        for c0 in ln_chunks:
            nrm = (load(0, c0, cwl) - mu) * rstd * vec_(0, c0, cwl) + vec_(1, c0, cwl)
            store(0, 0, c0, nrm * _sig(nrm))

    def f_merge(rows, vecs, c_, i_):
        g_a, g_b, y_a, y_b = rows
        return [[_sig(g_a) * y_a + _sig(g_b) * y_b]], []

    def f_ln(load, vec_, store, add_sum, i_):
        def t_of(c0):
            return alpha * load(0, c0, cwl) + (1.0 + vec_(0, c0, cwl)) * load(1, c0, cwl)
        mu, rstd = _ln_stats(t_of, D, cwl)
        for c0 in ln_chunks:
            y = (t_of(c0) - mu) * rstd * vec_(1, c0, cwl) + vec_(2, c0, cwl)
            store(0, 0, c0, y)
            store(1, 0, c0, y * (1.0 + vec_(3, c0, cwl)) + vec_(4, c0, cwl))

    saved = []
    xl = x0
    h = rowwise("modulate", f_mod, S, [_ri(x0)], [mods[0][1], mods[0][0]], [(BF16, 1)], W=D)[0]
    ag_first, _ = gather_forward("ag0a", first_ops, ag_first, h)
    W_in[0], W_dw[0] = gather_end("ag0a", ag_first, h)
    ag_next, next_ops = None, None

    def after_token(token):
        return [token] if token is not None else []

    def rope_heads(tile, extras, j):
        tc_, tl, th = extras
        heads = []
        for hh in range(tile.shape[1] // HEAD_DIM):
            xh = tile[:, hh * HEAD_DIM:(hh + 1) * HEAD_DIM]
            is_qk = j * (tile.shape[1] // HEAD_DIM) + hh < nq + nkv
            heads.append(jnp.where(is_qk, _rope(xh, tc_, tl, th, 1.0), xh))
        return jnp.concatenate(heads, axis=1)

    for l in range(L):
        sh_a, sc_a, gt_a, sh_f, sc_f, gt_f = mods[l]
        if l == 0:
            rest_ops = gather_rest(0)
            ag_rest, token = gather_begin("ag0b", rest_ops, W_in[0])
        elif l + 1 < L:
            next_ops = gather_first(l + 1) + gather_rest(l + 1)
            ag_next, token = gather_begin("ag%d" % (l + 1), next_ops, W_in[l])
        else:
            token = None
        qkv = matmul("in_qkv", h, W_in[l], "nn", BF16, n=Dqkv, deps=after_token(token),
                     epilogue=rope_heads, row_extras=[t_cos, t_lo, t_hi])
        p_glu = matmul("in_glu", h, W_in[l], "nn", F32, b_off=Dqkv, n=2 * D)
        p_gate = matmul("in_gate", h, W_in[l], "nn", F32, b_off=Dqkv + 2 * D, n=2 * D)
        sinkcol = jnp.repeat(sink[l].reshape(nkv, GQA_GROUP), BLOCK, axis=1).reshape(nkv, GQA_GROUP * BLOCK, 1)
        att = attn_fwd(qkv, sinkcol, S, D, Dkv)
        uc = conv_fwd(p_glu, W_dw[l], S, D)
        token = None
        if l == 0:
            ag_rest, _ = gather_forward("ag0b", rest_ops, ag_rest, uc)
            W_gu[0], W_oa[0], W_ob[0], W_out[0], W_down[0] = gather_end("ag0b", ag_rest, uc)
            if L > 1:
                next_ops = gather_first(1) + gather_rest(1)
                ag_next, token = gather_begin("ag1", next_ops, W_gu[0])
        y_a = matmul("oa", att, W_oa[l], "nn", F32, deps=after_token(token))
        z = rowwise("conv_ln", f_convln, S, [_ri(uc)], [vec(conv_ln_g, l), vec(conv_ln_b, l)], [(BF16, 1)], W=D,
                    lazy=True, rc=LN_ROWS)[0]
        y_b = matmul("ob", z, W_ob[l], "nn", F32)
        merged = rowwise("merge", f_merge, S, [_ri(p_gate, col=0), _ri(p_gate, col=1), _ri(y_a), _ri(y_b)], [],
                         [(BF16, 1)], W=D, cw=_pick(D, 512))[0]
        r1 = matmul("out", merged, W_out[l], "nn", F32)
        x1, h2 = rowwise("ln1", f_ln, S, [_ri(xl), _ri(r1)], [gt_a, vec(ln1_g, l), vec(ln1_b, l), sc_f, sh_f],
                         [(F32, 1), (BF16, 1)], W=D, lazy=True, rc=LN_ROWS)
        act, gate, up = matmul_swiglu(h2, W_gu[l], DFF)
        token = None
        if l + 1 < L:
            ag_next, token = gather_forward("ag%d" % (l + 1), next_ops, ag_next, act)
        f = matmul("down", act, W_down[l], "nn", F32, deps=after_token(token))
        nsc, nsh = (mods[l + 1][1], mods[l + 1][0]) if l + 1 < L else (sc_a, sh_a)
        x2, h_next = rowwise("ln2", f_ln, S, [_ri(x1), _ri(f)], [gt_f, vec(ln2_g, l), vec(ln2_b, l), nsc, nsh],
                             [(F32, 1), (BF16, 1)], W=D, lazy=True, rc=LN_ROWS)
        saved.append(dict(x=xl, h=h, qkv=qkv, sinkcol=sinkcol, att=att, p_glu=p_glu, p_gate=p_gate, y_a=y_a, y_b=y_b,
                          uc=uc, z=z, merged=merged, r1=r1, x1=x1, h2=h2, gate=gate, up=up, act=act, f=f))
        xl, h = x2, h_next
        if l + 1 < L:
            (W_in[l + 1], W_dw[l + 1], W_gu[l + 1], W_oa[l + 1], W_ob[l + 1], W_out[l + 1],
             W_down[l + 1]) = gather_end("ag%d" % (l + 1), ag_next, x2)

    def f_loss(rows, vecs, c_, i_):
        y, t = rows
        e = y - t
        return [[e * (1.0 / D)]], [e * e]

    dy, err = rowwise("loss", f_loss, S, [_ri(xl), _ri(target)], [], [(F32, 1)], 1, W=D)
    loss = lax.psum(0.5 * jnp.sum(err) / D, AXES)

    def ln_bwd_passes(t_of, dout_of, g_of, r_of, gt_of, store, add_sum):
        mu, rstd = _ln_stats(t_of, D, cwl)
        m1 = _row_mean(lambda c0: dout_of(c0) * g_of(c0), D, cwl)
        m2 = _row_mean(lambda c0: dout_of(c0) * g_of(c0) * ((t_of(c0) - mu) * rstd), D, cwl)
        for c0 in ln_chunks:
            dout, xhat = dout_of(c0), (t_of(c0) - mu) * rstd
            dt = rstd * (dout * g_of(c0) - m1 - xhat * m2)
            store(0, 0, c0, (1.0 + gt_of(c0)) * dt)
            store(1, 0, c0, alpha * dt)
            add_sum(0, c0, dout * xhat)
            add_sum(1, c0, dout)
            add_sum(2, c0, dt * r_of(c0))

    def f_ln_bwd_last(load, vec_, store, add_sum, i_):
        ln_bwd_passes(lambda c0: alpha * load(1, c0, cwl) + (1.0 + vec_(0, c0, cwl)) * load(2, c0, cwl),
                      lambda c0: load(0, c0, cwl), lambda c0: vec_(1, c0, cwl), lambda c0: load(2, c0, cwl),
                      lambda c0: vec_(0, c0, cwl), store, add_sum)

    def f_ln_bwd(load, vec_, store, add_sum, i_):
        ln_bwd_passes(lambda c0: alpha * load(3, c0, cwl) + (1.0 + vec_(1, c0, cwl)) * load(4, c0, cwl),
                      lambda c0: load(0, c0, cwl) + load(1, c0, cwl) * (1.0 + vec_(0, c0, cwl)),
                      lambda c0: vec_(2, c0, cwl), lambda c0: load(4, c0, cwl), lambda c0: vec_(1, c0, cwl),
                      store, add_sum)
        for c0 in ln_chunks:
            dh = load(1, c0, cwl)
            add_sum(3, c0, dh * load(2, c0, cwl))
            add_sum(4, c0, dh)

    def f_swiglu_bwd(rows, vecs, c_, i_):
        da, gate, up = rows[0], rows[1].astype(F32), rows[2].astype(F32)
        sg = _sig(gate)
        return [[da * up * sg * (1.0 + gate * (1.0 - sg)), da * (gate * sg)]], []

    def f_gate_bwd(rows, vecs, c_, i_):
        dm, g_a, g_b, y_a, y_b = rows
        sa, sb = _sig(g_a), _sig(g_b)
        return [[dm * sa], [dm * sb], [dm * y_a * sa * (1.0 - sa), dm * y_b * sb * (1.0 - sb)]], []

    def f_convln_bwd(load, vec_, store, add_sum, i_):
        mu, rstd = _ln_stats(lambda c0: load(1, c0, cwl), D, cwl)

        def parts(c0):
            xhat = (load(1, c0, cwl) - mu) * rstd
            nrm = xhat * vec_(0, c0, cwl) + vec_(1, c0, cwl)
            sg = _sig(nrm)
            return load(0, c0, cwl) * sg * (1.0 + nrm * (1.0 - sg)), xhat
        m1 = _row_mean(lambda c0: parts(c0)[0] * vec_(0, c0, cwl), D, cwl)

        def dyg_xhat(c0):
            dn, xhat = parts(c0)
            return dn * vec_(0, c0, cwl) * xhat
        m2 = _row_mean(dyg_xhat, D, cwl)
        for c0 in ln_chunks:
            dn, xhat = parts(c0)
            store(0, 0, c0, rstd * (dn * vec_(0, c0, cwl) - m1 - xhat * m2))
            add_sum(0, c0, dn * xhat)
            add_sum(1, c0, dn)

    def f_kv_combine(rows, vecs, c_, i_):
        k_lo, k_mid, k_hi, v_lo, v_mid, v_hi, tc_, tl, th = rows
        lo, hi = i_ > 0, i_ < nb - 1
        dk = jnp.where(lo, k_lo, 0.0) + k_mid + jnp.where(hi, k_hi, 0.0)
        dv = jnp.where(lo, v_lo, 0.0) + v_mid + jnp.where(hi, v_hi, 0.0)
        return [[_rope(dk, tc_, tl, th, -1.0)], [dv]], []

    def f_mod_bwd(rows, vecs, c_, i_):
        (dres, dh, xv), (sc,) = rows, vecs
        return [[dres + dh * (1.0 + sc)]], [dh * xv, dh]

    def flat(a):
        return a.reshape(-1, a.shape[-1])

    big_names = ("w_in", "w_gu", "w_oa", "w_ob", "w_out", "w_down")
    big_w = dict(w_in=w_in, w_gu=w_gu, w_oa=w_oa, w_ob=w_ob, w_out=w_out, w_down=w_down)
    big_m = dict(w_in=m_w_in, w_gu=m_w_gu, w_oa=m_w_oa, w_ob=m_w_ob, w_out=m_w_out, w_down=m_w_down)
    big_v = dict(w_in=v_w_in, w_gu=v_w_gu, w_oa=v_w_oa, w_ob=v_w_ob, w_out=v_w_out, w_down=v_w_down)
    big_res = {nm: None for nm in big_names}
    dmod = [None] * L
    small = dict(sink=[None] * L, conv_ln_g=[None] * L, conv_ln_b=[None] * L, ln1_g=[None] * L, ln1_b=[None] * L,
                 ln2_g=[None] * L, ln2_b=[None] * L)
    dwdw = [None] * L

    first, second, last = ("w_down", "w_gu"), ("w_out", "w_oa", "w_ob"), ("w_in",)

    def finish(names, tag, handle, after, l):
        landed = exchange_wait(tag + "_wait", handle, after)[len(names):]
        for nm, land in zip(names, landed):
            big_res[nm] = adamw_update("adamw_" + nm, big_w[nm], big_m[nm], big_v[nm], layer=l, landed=land,
                                       into=big_res[nm])
        return landed[0]

    dres, dh_next = dy, None
    rs_last, token = None, None
    for l in reversed(range(L)):
        sv = saved[l]
        sh_a, sc_a, gt_a, sh_f, sc_f, gt_f = mods[l]
        if dh_next is None:
            df, dres, d_g2, d_b2, d_gtf = rowwise(
                "ln2_bwd_last", f_ln_bwd_last, S, [_ri(dres), _ri(sv["x1"]), _ri(sv["f"])], [gt_f, vec(ln2_g, l)],
                [(BF16, 1), (F32, 1)], 3, W=D, lazy=True, rc=LN_ROWS)
            d_sca_next = d_sha_next = None
        else:
            df, dres, d_g2, d_b2, d_gtf, d_sca_next, d_sha_next = rowwise(
                "ln2_bwd", f_ln_bwd, S, [_ri(dres), _ri(dh_next), _ri(saved[l + 1]["x"]), _ri(sv["x1"]), _ri(sv["f"])],
                [mods[l + 1][1], gt_f, vec(ln2_g, l)], [(BF16, 1), (F32, 1)], 5, W=D, lazy=True, rc=LN_ROWS)
            dmod[l + 1][1], dmod[l + 1][0] = d_sca_next, d_sha_next
        dmod[l] = [None] * N_MOD
        dmod[l][5] = d_gtf
        small["ln2_g"][l], small["ln2_b"][l] = d_g2, d_b2
        g_down = matmul("d_w_down", sv["act"], df, "tn", BF16, deps=after_token(token))
        dact = matmul("d_act", df, W_down[l], "nt", F32)
        dgu = rowwise("swiglu_bwd", f_swiglu_bwd, S, [_ri(dact), _ri(sv["gate"]), _ri(sv["up"])], [],
                      [(BF16, 2)], W=DFF, tr=128, cw=_pick(DFF, 512))[0]
        g_gu = matmul("d_w_gu", sv["h2"], dgu, "tn", BF16)
        rs_first, token = begin("rsa%d" % l, [op_scatter_axis(g_down, 0), op_scatter_axis(g_gu, 1)], g_gu)
        dh2 = matmul("d_h2", dgu, W_gu[l], "nt", F32, deps=[token])
        dr1, dres, d_g1, d_b1, d_gta, d_scf, d_shf = rowwise(
            "ln1_bwd", f_ln_bwd, S, [_ri(dres), _ri(dh2), _ri(sv["x1"]), _ri(sv["x"]), _ri(sv["r1"])],
            [sc_f, gt_a, vec(ln1_g, l)], [(BF16, 1), (F32, 1)], 5, W=D, lazy=True, rc=LN_ROWS)
        dmod[l][2], dmod[l][4], dmod[l][3] = d_gta, d_scf, d_shf
        small["ln1_g"][l], small["ln1_b"][l] = d_g1, d_b1
        g_out = matmul("d_w_out", sv["merged"], dr1, "tn", BF16)
        dmerged = matmul("d_merged", dr1, W_out[l], "nt", F32)
        if rs_last is not None:
            finish(last, "rsc%d" % (l + 1), rs_last, dmerged, l + 1)
        dy_a, dy_b, dp_gate = rowwise(
            "gate_bwd", f_gate_bwd, S,
            [_ri(dmerged), _ri(sv["p_gate"], col=0), _ri(sv["p_gate"], col=1), _ri(sv["y_a"]), _ri(sv["y_b"])], [],
            [(BF16, 1), (BF16, 1), (BF16, 2)], W=D, cw=_pick(D, 512))
        g_oa = matmul("d_w_oa", sv["att"], dy_a, "tn", BF16)
        datt = matmul("d_att", dy_a, W_oa[l], "nt", BF16)
        g_ob = matmul("d_w_ob", sv["z"], dy_b, "tn", BF16)
        dz = matmul("d_z", dy_b, W_ob[l], "nt", F32)
        duc, d_cg, d_cb = rowwise("conv_ln_bwd", f_convln_bwd, S, [_ri(dz), _ri(sv["uc"])],
                                  [vec(conv_ln_g, l), vec(conv_ln_b, l)], [(F32, 1)], 2, W=D, lazy=True, rc=LN_ROWS)
        small["conv_ln_g"][l], small["conv_ln_b"][l] = d_cg, d_cb
        dga, dgb, dwdw[l] = conv_bwd(duc, sv["p_glu"], W_dw[l], S, D)
        before = finish(first, "rsa%d" % l, rs_first, dga, l)
        rs_second, token = begin("rsb%d" % l, [op_scatter_axis(g_out, 0), op_scatter_axis(g_oa, 0),
                                               op_scatter_axis(g_ob, 0)], before)
        dq_r, dkp, dvp, dsink = attn_bwd(sv["qkv"], sv["sinkcol"], datt, (t_cos, t_lo, t_hi), S, D, Dkv)
        small["sink"][l] = dsink[:, :GQA_GROUP, 0].reshape(1, nq)
        dk_r, dv_r = rowwise(
            "kv_combine", f_kv_combine, S,
            [_ri(dkp, lead=2, shift=-1), _ri(dkp, lead=1), _ri(dkp, lead=0, shift=1),
             _ri(dvp, lead=2, shift=-1), _ri(dvp, lead=1), _ri(dvp, lead=0, shift=1),
             _ri(t_cos, whole=True), _ri(t_lo, whole=True), _ri(t_hi, whole=True)],
            [], [(BF16, 1), (BF16, 1)], W=Dkv, tr=BLOCK, cw=HEAD_DIM, rc=32)
        dp = jnp.concatenate([dq_r, dk_r, dv_r, dga, dgb, dp_gate], axis=1)
        g_in = matmul("d_w_in", sv["h"], dp, "tn", BF16, deps=[token])
        rs_last, token = begin("rsc%d" % l, [op_scatter_axis(g_in, 1)], g_in)
        dh_next = matmul("d_h", dp, W_in[l], "nt", F32, deps=[token])
        finish(second, "rsb%d" % l, rs_second, dh_next, l)
        token = None

    grad_x, d_sca0, d_sha0 = rowwise("mod_bwd", f_mod_bwd, S, [_ri(dres), _ri(dh_next), _ri(x0)], [mods[0][1]],
                                     [(F32, 1)], 2, W=D)
    dmod[0][1], dmod[0][0] = d_sca0, d_sha0
    finish(last, "rsc0", rs_last, grad_x, 0)

    small_names = ("b_ada", "sink", "conv_ln_g", "conv_ln_b", "ln1_g", "ln1_b", "ln2_g", "ln2_b")
    small_w = dict(b_ada=b_ada, sink=sink, conv_ln_g=conv_ln_g, conv_ln_b=conv_ln_b, ln1_g=ln1_g, ln1_b=ln1_b,
                   ln2_g=ln2_g, ln2_b=ln2_b)
    small_m = dict(b_ada=m_b_ada, sink=m_sink, conv_ln_g=m_conv_ln_g, conv_ln_b=m_conv_ln_b, ln1_g=m_ln1_g,
                   ln1_b=m_ln1_b, ln2_g=m_ln2_g, ln2_b=m_ln2_b)
    small_v = dict(b_ada=v_b_ada, sink=v_sink, conv_ln_g=v_conv_ln_g, conv_ln_b=v_conv_ln_b, ln1_g=v_ln1_g,
                   ln1_b=v_ln1_b, ln2_g=v_ln2_g, ln2_b=v_ln2_b)
    small_g = dict(small)
    small_g["b_ada"] = [jnp.concatenate(dmod[l], axis=1) for l in range(L)]
    sizes = [small_w[nm].size for nm in small_names]
    total = sum(sizes)
    padded = -(-total // (SMALL_W * ROW_CHUNK)) * (SMALL_W * ROW_CHUNK)

    def pack(parts):
        flat_ = jnp.concatenate([p.reshape(-1) for p in parts] + [jnp.zeros((padded - total,), F32)])
        return flat_.reshape(padded // SMALL_W, SMALL_W)

    g_pack = pack([jnp.concatenate(small_g[nm], axis=0) for nm in small_names])
    small_land, dwdw_land = exchange("gather_small", [op_gather_stack(g_pack),
                                                      op_scatter_axis(jnp.stack(dwdw, axis=0), 2)])
    small_out = adamw_update("adamw_small", pack([small_w[nm] for nm in small_names]),
                             pack([small_m[nm] for nm in small_names]), pack([small_v[nm] for nm in small_names]),
                             landed=small_land)

    def unpack(buf):
        flat_, out, o = buf.reshape(-1), {}, 0
        for nm, sz in zip(small_names, sizes):
            out[nm] = flat_[o:o + sz].reshape(small_w[nm].shape)
            o += sz
        return out
    small_res = [unpack(b) for b in small_out]

    dw_cols = w_dw.shape[2]

    def pad_dw(a):
        return jnp.pad(a, ((0, 0), (0, 32 - CONV_WIDTH), (0, 0))).reshape(L * 32, dw_cols)
    dw_out = adamw_update("adamw_w_dw", pad_dw(w_dw), pad_dw(m_w_dw), pad_dw(v_w_dw),
                          landed=dwdw_land.reshape(N_DEV, L * 32, dw_cols))
    dw_res = [a.reshape(L, 32, dw_cols)[:, :CONV_WIDTH] for a in dw_out]

    dmod_all = small_land.reshape(N_DEV, -1)[:, :L * N_MOD * D].reshape(N_DEV, L, N_MOD * D)
    dmod_mine = lax.dynamic_slice_in_dim(dmod_all, me * ada_cols, ada_cols, axis=2)
    dmod_pad = jnp.concatenate([dmod_mine, jnp.zeros_like(dmod_mine)], axis=0).astype(BF16)
    g_ada = jnp.stack([matmul("d_w_ada", c_pad, dmod_pad[:, l], "tn", F32) for l in range(L)], axis=0)
    ada_out = adamw_update("adamw_w_ada", flat(w_ada), flat(m_w_ada), flat(v_w_ada), grad=flat(g_ada))
    ada_res = [a.reshape(w_ada.shape) for a in ada_out]

    order = ("w_ada", "b_ada", "w_in", "sink", "w_dw", "conv_ln_g", "conv_ln_b", "w_oa", "w_ob", "w_out",
             "ln1_g", "ln1_b", "w_gu", "w_down", "ln2_g", "ln2_b")

    def result(nm, j):
        if nm == "w_ada":
            return ada_res[j]
        if nm == "w_dw":
            return dw_res[j]
        if nm in big_res:
            return big_res[nm][j]
        return small_res[j][nm]

    outs = [loss, grad_x.reshape(x.shape)]
    for j in range(4):
        outs += [result(nm, j) for nm in order]
    return tuple(outs)
```

```python
import math

import jax
import jax.numpy as jnp
from jax import lax
from jax.experimental import pallas as pl
from jax.experimental.pallas import tpu as pltpu

F32 = jnp.float32
BF16 = jnp.bfloat16
N_DEV = 8
AXES = ("x", "y", "c")
HEAD_DIM = 128
GQA_GROUP = 4
BLOCK = 128
ROPE_DIM = HEAD_DIM // 4
ROPE_HALF = ROPE_DIM // 2
ROPE_THETA = 500000.0
CONV_WIDTH = 31
CONV_HALO = 16
N_MOD = 6
LN_EPS = 1e-5
NEG_INF = -1e30
ADAM_LR, ADAM_B1, ADAM_B2, ADAM_EPS, ADAM_WD, ADAM_STEP = 0.001, 0.9, 0.999, 1e-08, 0.01, 10
VMEM_LIMIT = 56 * 1024 * 1024
LANE = 128
ROW_CHUNK = 16
SMALL_W = 512


def _params(*sem):
    return pltpu.CompilerParams(dimension_semantics=sem, vmem_limit_bytes=VMEM_LIMIT)


def _pick(dim, pref, mult=LANE):
    if dim <= pref:
        return dim
    best = None
    for t in range(mult, pref + 1, mult):
        if dim % t == 0:
            best = t
    assert best is not None, (dim, pref)
    return best


def matmul(name, a, b, mode, out_dtype, *, b_off=0, n=None, tm=1024, tn=1408, tk=2816, deps=(), epilogue=None,
           row_extras=()):
    if mode == "nn":
        (M, K), N = a.shape, (n or b.shape[1])
    elif mode == "tn":
        (K, M), N = a.shape, b.shape[1]
    else:
        (M, K), N = a.shape, b.shape[0]
    if mode == "tn":
        tn, tk = min(tn, 1024), max(tk, 4096)
    tm = _pick(M, tn, LANE) if mode == "tn" else _pick(M, tm, 16)
    tn, tk = _pick(math.gcd(N, b_off) if b_off else N, tn), _pick(K, tk)
    assert b_off % tn == 0 and N % tn == 0
    boff = b_off // tn
    nk = K // tk
    if mode == "nn":
        a_spec = pl.BlockSpec((tm, tk), lambda i, j, k: (i, k))
        b_spec = pl.BlockSpec((tk, tn), lambda i, j, k: (k, j + boff))
        dims = (((1,), (0,)), ((), ()))
    elif mode == "tn":
        a_spec = pl.BlockSpec((tk, tm), lambda i, j, k: (k, i))
        b_spec = pl.BlockSpec((tk, tn), lambda i, j, k: (k, j))
        dims = (((0,), (0,)), ((), ()))
    else:
        a_spec = pl.BlockSpec((tm, tk), lambda i, j, k: (i, k))
        b_spec = pl.BlockSpec((tn, tk), lambda i, j, k: (j, k))
        dims = (((1,), (1,)), ((), ()))
    n_ex, n_dep = len(row_extras), len(deps)
    assert epilogue is None or nk == 1

    def body(a_ref, b_ref, *rest):
        o_ref = rest[n_ex + n_dep]

        def dot():
            return lax.dot_general(a_ref[...].astype(BF16), b_ref[...].astype(BF16), dims, preferred_element_type=F32)
        if epilogue is not None:
            o_ref[...] = epilogue(dot(), [r[...] for r in rest[:n_ex]], pl.program_id(1)).astype(out_dtype)
        elif nk == 1:
            o_ref[...] = dot().astype(out_dtype)
        else:
            acc_ref = rest[n_ex + n_dep + 1]
            k = pl.program_id(2)

            @pl.when(k == 0)
            def _():
                acc_ref[...] = jnp.zeros_like(acc_ref)
            acc_ref[...] += dot()

            @pl.when(k == nk - 1)
            def _():
                o_ref[...] = acc_ref[...].astype(out_dtype)

    return pl.pallas_call(
        body, name=name, grid=(M // tm, N // tn, nk),
        in_specs=[a_spec, b_spec] + [pl.BlockSpec((tm, e.shape[1]), lambda i, j, k: (i, 0)) for e in row_extras]
        + [pl.BlockSpec(memory_space=pl.ANY)] * n_dep,
        out_specs=pl.BlockSpec((tm, tn), lambda i, j, k: (i, j)),
        out_shape=jax.ShapeDtypeStruct((M, N), out_dtype),
        scratch_shapes=[pltpu.VMEM((tm, tn), F32)] if nk > 1 else [],
        compiler_params=_params("parallel", "parallel", "arbitrary"),
    )(a, b, *row_extras, *deps)


def matmul_swiglu(h, w_gu, dff, *, tm=1024, tn=512):
    M, K = h.shape
    tm, tn = _pick(M, tm, 16), _pick(dff, tn)
    nj = dff // tn

    def body(a_ref, bg_ref, bu_ref, act_ref, gate_ref, up_ref):
        a = a_ref[...]
        gate = jnp.dot(a, bg_ref[...], preferred_element_type=F32)
        up = jnp.dot(a, bu_ref[...], preferred_element_type=F32)
        act_ref[...] = (gate * _sig(gate) * up).astype(BF16)
        gate_ref[...] = gate.astype(BF16)
        up_ref[...] = up.astype(BF16)

    out_spec = pl.BlockSpec((tm, tn), lambda i, j: (i, j))
    return pl.pallas_call(
        body, name="gu_swiglu", grid=(M // tm, nj),
        in_specs=[pl.BlockSpec((tm, K), lambda i, j: (i, 0)), pl.BlockSpec((K, tn), lambda i, j: (0, j)),
                  pl.BlockSpec((K, tn), lambda i, j: (0, nj + j))],
        out_specs=[out_spec] * 3,
        out_shape=[jax.ShapeDtypeStruct((M, dff), BF16)] * 3,
        compiler_params=_params("parallel", "parallel"),
    )(h, w_gu, w_gu)


def rowwise(name, fn, nrows, row_ins, vec_ins, row_outs, n_sums=0, *, W, tr=256, cw=None, rc=ROW_CHUNK,
            out_layer=None, into=None, lazy=False):
    cw = cw or W
    tr = _pick(nrows, tr, ROW_CHUNK)
    rc = min(rc, tr)
    assert W % cw == 0 and nrows % ROW_CHUNK == 0 and tr % rc == 0 and rc % 8 == 0
    nrt = nrows // tr
    n_ri, n_v, n_ro = len(row_ins), len(vec_ins), len(row_outs)

    def row_spec(lead, colblk, shift, whole):
        w = cw if whole else W

        def rmap(i):
            return jnp.clip(i + shift, 0, nrt - 1) if shift else i
        if lead is None:
            return pl.BlockSpec((tr, w), lambda i: (rmap(i), colblk))
        return pl.BlockSpec((None, tr, w), lambda i: (lead, rmap(i), colblk))

    n_prev = len(into) if into is not None else 0
    in_specs = [row_spec(*ri[1:]) for ri in row_ins]
    in_specs += [pl.BlockSpec((1, W), lambda i: (0, 0)) for _ in vec_ins]
    in_specs += [pl.BlockSpec(memory_space=pl.ANY)] * n_prev
    if out_layer is None:
        out_specs = [pl.BlockSpec((tr, p * W), lambda i: (i, 0)) for _, p in row_outs]
        out_shape = [jax.ShapeDtypeStruct((nrows, p * W), dt) for dt, p in row_outs]
    else:
        out_specs = [pl.BlockSpec((None, tr, p * W), lambda i: (out_layer[0], i, 0)) for _, p in row_outs]
        out_shape = [jax.ShapeDtypeStruct((out_layer[1], nrows, p * W), dt) for dt, p in row_outs]
    out_specs += [pl.BlockSpec((1, W), lambda i: (0, 0)) for _ in range(n_sums)]
    out_shape += [jax.ShapeDtypeStruct((1, W), F32) for _ in range(n_sums)]

    def body(*refs):
        rin, vin = refs[:n_ri], refs[n_ri:n_ri + n_v]
        refs = refs[n_prev:]
        rout = refs[n_ri + n_v:n_ri + n_v + n_ro]
        sout = refs[n_ri + n_v + n_ro:n_ri + n_v + n_ro + n_sums]
        acc = refs[n_ri + n_v + n_ro + n_sums:]
        i = pl.program_id(0)
        if n_sums:
            @pl.when(i == 0)
            def _():
                for a in acc:
                    a[...] = jnp.zeros_like(a)
        def fold8(s):
            part = s[0:8]
            for q in range(1, rc // 8):
                part = part + s[8 * q:8 * q + 8]
            return part

        def lazy_step(r, carry):
            r0 = pl.multiple_of(r * rc, rc)

            def load(idx, c0, w):
                return rin[idx][pl.ds(r0, rc), c0:c0 + w]

            def vec(idx, c0, w):
                return vin[idx][:, c0:c0 + w]

            def store(o, piece, c0, val):
                rout[o][pl.ds(r0, rc), piece * W + c0:piece * W + c0 + val.shape[1]] = val.astype(row_outs[o][0])

            def add_sum(k, c0, val):
                acc[k][:, c0:c0 + val.shape[1]] += fold8(val)
            fn(load, vec, store, add_sum, i)
            return carry
        if lazy:
            lax.fori_loop(0, tr // rc, lazy_step, 0, unroll=2)
        for c in range(0 if lazy else W // cw):
            c0 = c * cw
            vecs = [v[:, c0:c0 + cw] for v in vin]

            def step(r, carry, c=c, c0=c0, vecs=vecs):
                r0 = pl.multiple_of(r * rc, rc)
                rows = [ref[pl.ds(r0, rc), :] if ri[4] else ref[pl.ds(r0, rc), c0:c0 + cw]
                        for ref, ri in zip(rin, row_ins)]
                outs, sums = fn(rows, vecs, c, i)
                for oref, (dt, _), pieces in zip(rout, row_outs, outs):
                    for pi, piece in enumerate(pieces):
                        oref[pl.ds(r0, rc), pi * W + c0:pi * W + c0 + cw] = piece.astype(dt)
                for a, s in zip(acc, sums):
                    a[:, c0:c0 + cw] += fold8(s)
                return carry
            lax.fori_loop(0, tr // rc, step, 0)
        if n_sums:
            @pl.when(i == nrt - 1)
            def _():
                for o, a in zip(sout, acc):
                    o[...] = jnp.sum(a[...], axis=0, keepdims=True)

    res = pl.pallas_call(
        body, name=name, grid=(nrt,), in_specs=in_specs, out_specs=out_specs, out_shape=out_shape,
        scratch_shapes=[pltpu.VMEM((8, W), F32) for _ in range(n_sums)],
        input_output_aliases={n_ri + n_v + q: q for q in range(n_prev)},
        compiler_params=_params("arbitrary"),
    )(*[ri[0] for ri in row_ins], *vec_ins, *(into or ()))
    return res


def _ri(arr, lead=None, col=0, shift=0, whole=False):
    return (arr, lead, col, shift, whole)


def _sig(x):
    return jax.nn.sigmoid(x)


LN_CHUNK = 256
LN_ROWS = 64


def _row_mean(make, width, cw):
    acc = None
    for c0 in range(0, width, cw):
        t = make(c0)
        for q in range(cw // LANE):
            part = t[:, q * LANE:(q + 1) * LANE]
            acc = part if acc is None else acc + part
    return jnp.sum(acc, axis=-1, keepdims=True) * (1.0 / width)


def _ln_stats(t_of, width, cw):
    mu = _row_mean(t_of, width, cw)

    def sq(c0):
        d = t_of(c0) - mu
        return d * d
    return mu, lax.rsqrt(_row_mean(sq, width, cw) + LN_EPS)


def _rope(x, cos, s_lo, s_hi, sign):
    up = pltpu.roll(x, HEAD_DIM - ROPE_HALF, 1)
    down = pltpu.roll(x, ROPE_HALF, 1)
    return x * cos + sign * (up * s_lo + down * s_hi)


def rope_tables(S):
    pos = jnp.arange(S, dtype=F32)
    inv_freq = ROPE_THETA ** (-jnp.arange(0, ROPE_DIM, 2, dtype=F32) / ROPE_DIM)
    ang = pos[:, None] * inv_freq[None, :]
    cos, sin = jnp.cos(ang), jnp.sin(ang)
    ones = jnp.ones((S, HEAD_DIM - ROPE_DIM), F32)
    zeros = jnp.zeros((S, HEAD_DIM - ROPE_DIM), F32)
    zh = jnp.zeros((S, ROPE_HALF), F32)
    t_cos = jnp.concatenate([cos, cos, ones], axis=1)
    t_lo = jnp.concatenate([-sin, zh, zeros], axis=1)
    t_hi = jnp.concatenate([zh, sin, zeros], axis=1)
    return t_cos, t_lo, t_hi


ATTN_HEADS_PER_STEP = 4


def _attn_specs(S, D, Dkv):
    nb, nkv, qb = S // BLOCK, Dkv // HEAD_DIM, D // HEAD_DIM
    hps = math.gcd(ATTN_HEADS_PER_STEP, nkv)
    assert qb % hps == 0 and (qb + nkv) % hps == 0
    gw = GQA_GROUP * HEAD_DIM
    q_spec = pl.BlockSpec((BLOCK, hps * gw), lambda h, n: (n, h))

    def band(col0):
        c0 = col0 // hps
        return [pl.BlockSpec((BLOCK, hps * HEAD_DIM), lambda h, n: (jnp.maximum(n - 1, 0), c0 + h)),
                pl.BlockSpec((BLOCK, hps * HEAD_DIM), lambda h, n: (n, c0 + h)),
                pl.BlockSpec((BLOCK, hps * HEAD_DIM), lambda h, n: (jnp.minimum(n + 1, nb - 1), c0 + h))]
    sink_spec = pl.BlockSpec((hps, GQA_GROUP * BLOCK, 1), lambda h, n: (h, 0, 0))
    return nb, nkv, hps, gw, q_spec, band(qb), band(qb + nkv), sink_spec


def _attn_valid(n, S):
    shape = (GQA_GROUP * BLOCK, 3 * BLOCK)
    row = lax.broadcasted_iota(jnp.int32, shape, 0) & (BLOCK - 1)
    col = lax.broadcasted_iota(jnp.int32, shape, 1)
    rel = col - BLOCK - row
    kpos = (n - 1) * BLOCK + col
    return (jnp.abs(rel) <= BLOCK) & (kpos >= 0) & (kpos < S)


def _head(ref, hh, width):
    return ref[:, hh * width:(hh + 1) * width]


def _attn_probs(q, k_blocks, sink, valid):
    qs = jnp.concatenate([q[:, g * HEAD_DIM:(g + 1) * HEAD_DIM] for g in range(GQA_GROUP)], axis=0)
    kb = jnp.concatenate(k_blocks, axis=0)
    s = lax.dot_general(qs, kb, (((1,), (1,)), ((), ())), preferred_element_type=F32) * (HEAD_DIM ** -0.5)
    s = jnp.where(valid, s, NEG_INF)
    m = jnp.maximum(jnp.max(s, axis=-1, keepdims=True), sink)
    p = jnp.exp(s - m)
    e_sink = jnp.exp(sink - m)
    denom = jnp.sum(p, axis=-1, keepdims=True) + e_sink
    return qs, kb, p / denom, e_sink / denom


def attn_fwd(qkv, sinkcol, S, D, Dkv):
    nb, nkv, hps, gw, q_spec, k_specs, v_specs, sink_spec = _attn_specs(S, D, Dkv)

    def body(q_ref, k0, k1, k2, v0, v1, v2, sink_ref, o_ref):
        valid = _attn_valid(pl.program_id(1), S)
        for hh in range(hps):
            _, _, w, _ = _attn_probs(_head(q_ref, hh, gw), [_head(r, hh, HEAD_DIM) for r in (k0, k1, k2)],
                                     sink_ref[hh], valid)
            vb = jnp.concatenate([_head(r, hh, HEAD_DIM) for r in (v0, v1, v2)], axis=0)
            o = jnp.dot(w.astype(BF16), vb, preferred_element_type=F32)
            for g in range(GQA_GROUP):
                c0 = hh * gw + g * HEAD_DIM
                o_ref[:, c0:c0 + HEAD_DIM] = o[g * BLOCK:(g + 1) * BLOCK].astype(BF16)

    return pl.pallas_call(
        body, name="attn_fwd", grid=(nkv // hps, nb),
        in_specs=[q_spec, *k_specs, *v_specs, sink_spec],
        out_specs=pl.BlockSpec((BLOCK, hps * gw), lambda h, n: (n, h)),
        out_shape=jax.ShapeDtypeStruct((S, D), BF16),
        compiler_params=_params("parallel", "arbitrary"),
    )(qkv, qkv, qkv, qkv, qkv, qkv, qkv, sinkcol)


def attn_bwd(qkv, sinkcol, datt, tables, S, D, Dkv):
    nb, nkv, hps, gw, q_spec, k_specs, v_specs, sink_spec = _attn_specs(S, D, Dkv)

    def body(q_ref, k0, k1, k2, v0, v1, v2, sink_ref, do_ref, tc_ref, tl_ref, th_ref, dq_ref, dkp_ref, dvp_ref,
             dsink_ref):
        n = pl.program_id(1)
        valid = _attn_valid(n, S)
        tc_, tl, th = tc_ref[...], tl_ref[...], th_ref[...]

        @pl.when(n == 0)
        def _():
            dsink_ref[...] = jnp.zeros_like(dsink_ref)
        for hh in range(hps):
            qs, kb, w, w_sink = _attn_probs(_head(q_ref, hh, gw), [_head(r, hh, HEAD_DIM) for r in (k0, k1, k2)],
                                            sink_ref[hh], valid)
            vb = jnp.concatenate([_head(r, hh, HEAD_DIM) for r in (v0, v1, v2)], axis=0)
            do = _head(do_ref, hh, gw)
            dos = jnp.concatenate([do[:, g * HEAD_DIM:(g + 1) * HEAD_DIM] for g in range(GQA_GROUP)], axis=0)
            dv = lax.dot_general(w.astype(BF16), dos, (((0,), (0,)), ((), ())), preferred_element_type=F32)
            dw = lax.dot_general(dos, vb, (((1,), (1,)), ((), ())), preferred_element_type=F32)
            delta = jnp.sum(w * dw, axis=-1, keepdims=True)
            ds = (w * (dw - delta) * (HEAD_DIM ** -0.5)).astype(BF16)
            dq = jnp.dot(ds, kb, preferred_element_type=F32)
            dk = lax.dot_general(ds, qs, (((0,), (0,)), ((), ())), preferred_element_type=F32)
            for g in range(GQA_GROUP):
                c0 = hh * gw + g * HEAD_DIM
                dq_ref[:, c0:c0 + HEAD_DIM] = _rope(dq[g * BLOCK:(g + 1) * BLOCK], tc_, tl, th, -1.0).astype(BF16)
            for j in range(3):
                dkp_ref[j, :, hh * HEAD_DIM:(hh + 1) * HEAD_DIM] = dk[j * BLOCK:(j + 1) * BLOCK]
                dvp_ref[j, :, hh * HEAD_DIM:(hh + 1) * HEAD_DIM] = dv[j * BLOCK:(j + 1) * BLOCK]
            t = w_sink * delta
            for g in range(GQA_GROUP):
                dsink_ref[hh, g:g + 1, :] -= jnp.sum(t[g * BLOCK:(g + 1) * BLOCK], axis=0, keepdims=True)

    part_spec = pl.BlockSpec((3, BLOCK, hps * HEAD_DIM), lambda h, n: (0, n, h))
    return pl.pallas_call(
        body, name="attn_bwd", grid=(nkv // hps, nb),
        in_specs=[q_spec, *k_specs, *v_specs, sink_spec, pl.BlockSpec((BLOCK, hps * gw), lambda h, n: (n, h))]
        + [pl.BlockSpec((BLOCK, HEAD_DIM), lambda h, n: (n, 0))] * 3,
        out_specs=[pl.BlockSpec((BLOCK, hps * gw), lambda h, n: (n, h)), part_spec, part_spec,
                   pl.BlockSpec((hps, 8, LANE), lambda h, n: (h, 0, 0))],
        out_shape=[jax.ShapeDtypeStruct((S, D), BF16), jax.ShapeDtypeStruct((3, S, Dkv), F32),
                   jax.ShapeDtypeStruct((3, S, Dkv), F32), jax.ShapeDtypeStruct((nkv, 8, LANE), F32)],
        compiler_params=_params("parallel", "arbitrary"),
    )(qkv, qkv, qkv, qkv, qkv, qkv, qkv, sinkcol, datt, *tables)


CONV_TC = 128
CONV_ROWS = 64


def _shift_rows(win, b):
    return win if b == 0 else pltpu.roll(win, win.shape[0] - b, 0)


def _conv_fill_u(ga_ref, gb_ref, upad_ref, S):
    tc = upad_ref.shape[1]
    zero = jnp.zeros((CONV_HALO, tc), F32)
    upad_ref[0:CONV_HALO, :] = zero
    upad_ref[S + CONV_HALO:S + 2 * CONV_HALO, :] = zero

    def fill(r, carry):
        r0 = pl.multiple_of(r * CONV_ROWS, CONV_ROWS)
        upad_ref[pl.ds(r0 + CONV_HALO, CONV_ROWS), :] = ga_ref[pl.ds(r0, CONV_ROWS), :] * _sig(gb_ref[pl.ds(r0, CONV_ROWS), :])
        return carry
    lax.fori_loop(0, S // CONV_ROWS, fill, 0)


def conv_fwd(pglu, wdw, S, D):
    tc = min(CONV_TC, D)
    nct = D // tc

    def body(ga_ref, gb_ref, w_ref, o_ref, upad_ref):
        _conv_fill_u(ga_ref, gb_ref, upad_ref, S)
        w = w_ref[...]

        def step(r, carry):
            r0 = pl.multiple_of(r * CONV_ROWS, CONV_ROWS)
            win = upad_ref[pl.ds(r0, CONV_ROWS + 2 * CONV_HALO), :]
            acc = jnp.zeros((CONV_ROWS, tc), F32)
            for b in range(8):
                shifted = _shift_rows(win, b)
                for k in range(CONV_WIDTH):
                    o = k + CONV_HALO - CONV_WIDTH // 2
                    if o % 8 == b:
                        acc = acc + w[k:k + 1, :] * shifted[o - b:o - b + CONV_ROWS, :]
            o_ref[pl.ds(r0, CONV_ROWS), :] = acc
            return carry
        lax.fori_loop(0, S // CONV_ROWS, step, 0)

    return pl.pallas_call(
        body, name="conv_fwd", grid=(nct,),
        in_specs=[pl.BlockSpec((S, tc), lambda j: (0, j)), pl.BlockSpec((S, tc), lambda j: (0, nct + j)),
                  pl.BlockSpec((CONV_WIDTH, tc), lambda j: (0, j))],
        out_specs=pl.BlockSpec((S, tc), lambda j: (0, j)),
        out_shape=jax.ShapeDtypeStruct((S, D), F32),
        scratch_shapes=[pltpu.VMEM((S + 2 * CONV_HALO, tc), F32)],
        compiler_params=_params("parallel"),
    )(pglu, pglu, wdw)


def conv_bwd(duc, pglu, wdw, S, D):
    tc = min(CONV_TC, D)
    nct = D // tc
    half = CONV_WIDTH // 2

    def body(d_ref, ga_ref, gb_ref, w_ref, dga_ref, dgb_ref, dw_ref, upad_ref, dpad_ref, dwacc_ref):
        _conv_fill_u(ga_ref, gb_ref, upad_ref, S)
        zero = jnp.zeros((CONV_HALO, tc), F32)
        dpad_ref[0:CONV_HALO, :] = zero
        dpad_ref[S + CONV_HALO:S + 2 * CONV_HALO, :] = zero

        def fill(r, carry):
            r0 = pl.multiple_of(r * CONV_ROWS, CONV_ROWS)
            dpad_ref[pl.ds(r0 + CONV_HALO, CONV_ROWS), :] = d_ref[pl.ds(r0, CONV_ROWS), :]
            return carry
        lax.fori_loop(0, S // CONV_ROWS, fill, 0)
        dwacc_ref[...] = jnp.zeros_like(dwacc_ref)
        w = w_ref[...]

        def step(r, carry):
            r0 = pl.multiple_of(r * CONV_ROWS, CONV_ROWS)
            uwin = upad_ref[pl.ds(r0, CONV_ROWS + 2 * CONV_HALO), :]
            dwin = dpad_ref[pl.ds(r0, CONV_ROWS + 2 * CONV_HALO), :]
            d = dwin[CONV_HALO:CONV_HALO + CONV_ROWS, :]
            du = jnp.zeros((CONV_ROWS, tc), F32)
            for b in range(8):
                d_shifted, u_shifted = _shift_rows(dwin, b), _shift_rows(uwin, b)
                for k in range(CONV_WIDTH):
                    o = CONV_HALO + half - k
                    if o % 8 == b:
                        du = du + w[k:k + 1, :] * d_shifted[o - b:o - b + CONV_ROWS, :]
                    o = CONV_HALO + k - half
                    if o % 8 == b:
                        prod = d * u_shifted[o - b:o - b + CONV_ROWS, :]
                        part = prod[0:8]
                        for q in range(1, CONV_ROWS // 8):
                            part = part + prod[8 * q:8 * q + 8]
                        dwacc_ref[k] += part
            ga = ga_ref[pl.ds(r0, CONV_ROWS), :]
            sg = _sig(gb_ref[pl.ds(r0, CONV_ROWS), :])
            dga_ref[pl.ds(r0, CONV_ROWS), :] = (du * sg).astype(BF16)
            dgb_ref[pl.ds(r0, CONV_ROWS), :] = (du * ga * sg * (1.0 - sg)).astype(BF16)
            return carry
        lax.fori_loop(0, S // CONV_ROWS, step, 0)
        dw_ref[...] = jnp.sum(dwacc_ref[...], axis=1)

    return pl.pallas_call(
        body, name="conv_bwd", grid=(nct,),
        in_specs=[pl.BlockSpec((S, tc), lambda j: (0, j)), pl.BlockSpec((S, tc), lambda j: (0, j)),
                  pl.BlockSpec((S, tc), lambda j: (0, nct + j)), pl.BlockSpec((CONV_WIDTH, tc), lambda j: (0, j))],
        out_specs=[pl.BlockSpec((S, tc), lambda j: (0, j)), pl.BlockSpec((S, tc), lambda j: (0, j)),
                   pl.BlockSpec((32, tc), lambda j: (0, j))],
        out_shape=[jax.ShapeDtypeStruct((S, D), BF16), jax.ShapeDtypeStruct((S, D), BF16),
                   jax.ShapeDtypeStruct((32, D), F32)],
        scratch_shapes=[pltpu.VMEM((S + 2 * CONV_HALO, tc), F32), pltpu.VMEM((S + 2 * CONV_HALO, tc), F32),
                        pltpu.VMEM((32, 8, tc), F32)],
        compiler_params=_params("parallel"),
    )(duc, pglu, pglu, wdw)


def exchange(name, ops):
    n = len(ops)

    def body(*refs):
        xs, outs = refs[:n], refs[n:2 * n]
        send, recv, lsem = refs[2 * n:]
        mx, my, mc = lax.axis_index("x"), lax.axis_index("y"), lax.axis_index("c")
        me = 4 * mx + 2 * my + mc
        local = [pltpu.make_async_copy(op[3](xs[i], me), op[4](outs[i], me), lsem.at[i]) for i, op in enumerate(ops)]
        for cp in local:
            cp.start()
        copies = []
        for k in range(1, N_DEV):
            px = 1 - mx if k & 4 else mx
            py = 1 - my if k & 2 else my
            pc = 1 - mc if k & 1 else mc
            peer = 4 * px + 2 * py + pc
            for i, op in enumerate(ops):
                cp = pltpu.make_async_remote_copy(
                    src_ref=op[3](xs[i], peer), dst_ref=op[4](outs[i], me),
                    send_sem=send.at[i, k - 1], recv_sem=recv.at[i, k - 1],
                    device_id=(px, py, pc), device_id_type=pl.DeviceIdType.MESH)
                cp.start()
                copies.append(cp)
        for cp in copies:
            cp.wait()
        for cp in local:
            cp.wait()

    any_spec = pl.BlockSpec(memory_space=pl.ANY)
    return pl.pallas_call(
        body, name=name,
        in_specs=[any_spec] * n, out_specs=[any_spec] * n,
        out_shape=[jax.ShapeDtypeStruct(op[1], op[2]) for op in ops],
        scratch_shapes=[pltpu.SemaphoreType.DMA((n, N_DEV - 1)), pltpu.SemaphoreType.DMA((n, N_DEV - 1)),
                        pltpu.SemaphoreType.DMA((n,))],
        compiler_params=pltpu.CompilerParams(has_side_effects=True),
    )(*[op[0] for op in ops])


_HBM = pl.BlockSpec(memory_space=pltpu.HBM)
_SEM = pl.BlockSpec(memory_space=pltpu.SEMAPHORE)
_EFFECT = pltpu.SideEffectType.DATAFLOW_SIDE_EFFECTING


def _me_and_peers():
    mx, my, mc = lax.axis_index("x"), lax.axis_index("y"), lax.axis_index("c")
    peers = []
    for k in range(1, N_DEV):
        px = 1 - mx if k & 4 else mx
        py = 1 - my if k & 2 else my
        pc = 1 - mc if k & 1 else mc
        peers.append((k, (px, py, pc), 4 * px + 2 * py + pc))
    return 4 * mx + 2 * my + mc, peers


PLACE_STEPS = 4


def exchange_local(name, ops, me_arr, deps=()):
    n = len(ops)
    in_specs, out_specs = [], []
    for op in ops:
        kind, axis = op[5]
        rows, cols = op[0].shape if kind == "gather" else op[1][1:]
        steps = PLACE_STEPS if rows % (PLACE_STEPS * ROW_CHUNK) == 0 else 1
        tr = rows // steps

        def row(i, steps=steps):
            return i if steps > 1 else 0

        def window(i, me, axis=axis, steps=steps):
            return (me[0] * steps + row(i, steps), 0) if axis == 0 else (row(i, steps), me[0])
        if kind == "gather":
            in_specs.append(pl.BlockSpec((tr, cols), lambda i, me, row=row: (row(i), 0)))
            out_specs.append(pl.BlockSpec((tr, cols), window))
        else:
            in_specs.append(pl.BlockSpec((tr, cols), window))
            out_specs.append(pl.BlockSpec((None, tr, cols), lambda i, me, row=row: (me[0], row(i), 0)))

    def body(me_ref, *refs):
        for i in range(n):
            refs[n + len(deps) + i][...] = refs[i][...]

    return pl.pallas_call(
        body, name=name,
        grid_spec=pltpu.PrefetchScalarGridSpec(
            num_scalar_prefetch=1, grid=(PLACE_STEPS,),
            in_specs=in_specs + [pl.BlockSpec(memory_space=pl.ANY)] * len(deps), out_specs=out_specs),
        out_shape=[jax.ShapeDtypeStruct(op[1], op[2]) for op in ops],
        compiler_params=_params("arbitrary"),
    )(me_arr, *[op[0] for op in ops], *deps)


ALL_PEERS = (1, 2, 3, 4, 5, 6, 7)
SIBLING = 1
SAME_CORE = (2, 4, 6)


def _flipped(pos, k):
    p = (1 - pos[0] if k & 4 else pos[0], 1 - pos[1] if k & 2 else pos[1], 1 - pos[2] if k & 1 else pos[2])
    return p, 4 * p[0] + 2 * p[1] + p[2]


def direct_copies(ops, ks):
    n = len(ops)

    def build(refs, send, recv):
        xs, lands = refs[:n], refs[n:2 * n]
        pos = (lax.axis_index("x"), lax.axis_index("y"), lax.axis_index("c"))
        _, me = _flipped(pos, 0)
        copies = []
        for j, k in enumerate(ks):
            peer_id, peer = _flipped(pos, k)
            for i, op in enumerate(ops):
                s = i * len(ks) + j
                copies.append(pltpu.make_async_remote_copy(
                    src_ref=op[3](xs[i], peer), dst_ref=op[4](lands[i], me), send_sem=send.at[s], recv_sem=recv.at[s],
                    device_id=peer_id, device_id_type=pl.DeviceIdType.MESH))
        return copies
    return n * len(ks), build


def forward_copies(ops, ks):
    n = len(ops)

    def build(refs, send, recv):
        lands = refs[:n]
        pos = (lax.axis_index("x"), lax.axis_index("y"), lax.axis_index("c"))
        sibling_id, _ = _flipped(pos, SIBLING)
        copies = []
        for j, k in enumerate(ks):
            _, origin = _flipped(pos, k)
            for i, op in enumerate(ops):
                s = i * len(ks) + j
                region = op[4](lands[i], origin)
                copies.append(pltpu.make_async_remote_copy(
                    src_ref=region, dst_ref=region, send_sem=send.at[s], recv_sem=recv.at[s],
                    device_id=sibling_id, device_id_type=pl.DeviceIdType.MESH))
        return copies
    return n * len(ks), build


def exchange_start(name, arrays, plan):
    n_sems, build = plan
    n = len(arrays)

    def body(*refs):
        for cp in build(refs[:n], refs[n], refs[n + 1]):
            cp.start()
        refs[-1][...] = jnp.zeros_like(refs[-1])

    args = [pltpu.with_memory_space_constraint(a, pltpu.HBM) for a in arrays]
    sems = pltpu.SemaphoreType.DMA((n_sems,))
    outs = pl.pallas_call(
        body, name=name,
        out_shape=(sems, sems, *[pltpu.HBM(a.shape, a.dtype) for a in args], jax.ShapeDtypeStruct((8, LANE), F32)),
        in_specs=[_HBM] * n,
        out_specs=(_SEM, _SEM, *[_HBM] * n, pl.BlockSpec(memory_space=pltpu.VMEM)),
        input_output_aliases={i: 2 + i for i in range(n)},
        compiler_params=pltpu.CompilerParams(has_side_effects=_EFFECT),
    )(*args)
    return (build, outs[0], outs[1], outs[2:2 + n]), outs[-1]


def exchange_wait(name, handle, after):
    build, send, recv, thru = handle
    n = len(thru)

    def body(*refs):
        for cp in build(refs[:n], refs[n], refs[n + 1]):
            cp.wait_send()
            cp.wait_recv()

    return pl.pallas_call(
        body, name=name,
        out_shape=[pltpu.HBM(a.shape, a.dtype) for a in thru],
        in_specs=[_HBM] * n + [_SEM, _SEM, pl.BlockSpec(memory_space=pl.ANY)],
        out_specs=[_HBM] * n,
        input_output_aliases={i: i for i in range(n)},
        compiler_params=pltpu.CompilerParams(has_side_effects=_EFFECT),
    )(*thru, send, recv, after)


def _whole(ref, q):
    return ref


def _slot(ref, q):
    return ref.at[q]


def op_gather_stack(x):
    return (x, (N_DEV,) + x.shape, x.dtype, _whole, _slot)


def op_gather_axis(x, axis):
    size = x.shape[axis]
    shape = x.shape[:axis] + (N_DEV * size,) + x.shape[axis + 1:]

    def dst(ref, q):
        idx = [slice(None)] * len(shape)
        idx[axis] = pl.ds(pl.multiple_of(q * size, size), size)
        return ref.at[tuple(idx)]
    return (x, shape, x.dtype, _whole, dst, ("gather", axis))


def op_scatter_axis(x, axis):
    size = x.shape[axis] // N_DEV
    shape = x.shape[:axis] + (size,) + x.shape[axis + 1:]

    def src(ref, q):
        idx = [slice(None)] * len(shape)
        idx[axis] = pl.ds(pl.multiple_of(q * size, size), size)
        return ref.at[tuple(idx)]
    return (x, (N_DEV,) + shape, x.dtype, src, _slot, ("scatter", axis))


def _adamw(w, g, m, v):
    m = ADAM_B1 * m + (1.0 - ADAM_B1) * g
    v = ADAM_B2 * v + (1.0 - ADAM_B2) * (g * g)
    m_hat = m / (1.0 - ADAM_B1 ** ADAM_STEP)
    v_hat = v / (1.0 - ADAM_B2 ** ADAM_STEP)
    delta = -ADAM_LR * (m_hat / (jnp.sqrt(v_hat) + ADAM_EPS) + ADAM_WD * w)
    return delta, m, v


def adamw_update(name, w, m, v, *, layer=None, landed=None, grad=None, into=None):
    R, W = w.shape[-2:]
    n_g = N_DEV if landed is not None else 1

    def fn(rows, vecs, c, i):
        g = rows[0].astype(F32)
        for q in range(1, n_g):
            g = g + rows[q].astype(F32)
        wv, mv, vv = rows[n_g:]
        delta, m2, v2 = _adamw(wv, g, mv, vv)
        return [[g], [delta], [m2], [v2]], []

    g_ins = [_ri(landed, lead=q) for q in range(N_DEV)] if landed is not None else [_ri(grad)]
    cw = LANE if W % LANE == 0 else W
    return rowwise(name, fn, R, g_ins + [_ri(w, lead=layer), _ri(m, lead=layer), _ri(v, lead=layer)], [],
                   [(F32, 1)] * 4, W=W, tr=128, cw=cw, rc=32,
                   out_layer=None if layer is None else (layer, w.shape[0]), into=into)


def kernel(x, c, w_ada, b_ada, w_in, sink, w_dw, conv_ln_g, conv_ln_b, w_oa, w_ob, w_out, ln1_g, ln1_b, w_gu, w_down, ln2_g, ln2_b, loss_target, m_w_ada, m_b_ada, m_w_in, m_sink, m_w_dw, m_conv_ln_g, m_conv_ln_b, m_w_oa, m_w_ob, m_w_out, m_ln1_g, m_ln1_b, m_w_gu, m_w_down, m_ln2_g, m_ln2_b, v_w_ada, v_b_ada, v_w_in, v_sink, v_w_dw, v_conv_ln_g, v_conv_ln_b, v_w_oa, v_w_ob, v_w_out, v_ln1_g, v_ln1_b, v_w_gu, v_w_down, v_ln2_g, v_ln2_b):
    L = w_ada.shape[0]
    S, D = x.shape[1], x.shape[2]
    Dkv = D // GQA_GROUP
    Dqkv = D + 2 * Dkv
    DFF = w_down.shape[1] * N_DEV
    nq, nkv, nb = D // HEAD_DIM, Dkv // HEAD_DIM, S // BLOCK
    alpha = (2.0 * L) ** 0.25
    me = 4 * lax.axis_index("x") + 2 * lax.axis_index("y") + lax.axis_index("c")
    x0 = x.reshape(S, D)
    target = loss_target.reshape(S, D)
    t_cos, t_lo, t_hi = rope_tables(S)

    c_act = jax.nn.silu(c)
    c_all = exchange("gather_c", [op_gather_stack(c_act)])[0].reshape(N_DEV, D)
    c_pad = jnp.concatenate([c_all, jnp.zeros_like(c_all)], axis=0).astype(BF16)
    ada_cols = w_ada.shape[2]
    mod_part = jnp.stack([matmul("mod_mm", c_pad, w_ada[l], "nn", F32)[:N_DEV] for l in range(L)], axis=1)
    mod_land = exchange("scatter_mod", [op_scatter_axis(mod_part, 0)])[0]
    mod = jnp.transpose(mod_land.reshape(N_DEV, L, ada_cols), (1, 0, 2)).reshape(L, N_MOD * D) + b_ada
    mods = [[mod[l:l + 1, j * D:(j + 1) * D] for j in range(N_MOD)] for l in range(L)]

    me_arr = me.astype(jnp.int32).reshape(1)

    def begin(tag, ops, dep, ks=ALL_PEERS):
        lands = exchange_local(tag + "_local", ops, me_arr, [dep])
        return exchange_start(tag + "_start", [op[0] for op in ops] + list(lands), direct_copies(ops, ks))

    def gather_begin(tag, ops, dep):
        return begin(tag, ops, dep, (SIBLING,) + SAME_CORE)

    def gather_forward(tag, ops, handle, after):
        lands = exchange_wait(tag + "_wait", handle, after)[len(ops):]
        return exchange_start(tag + "_fwd_start", list(lands), forward_copies(ops, SAME_CORE))

    def gather_end(tag, handle, after):
        return exchange_wait(tag + "_fwd_wait", handle, after)

    def gather_first(l):
        return [op_gather_axis(w_in[l].astype(BF16), 1), op_gather_axis(w_dw[l], 1)]

    def gather_rest(l):
        return [op_gather_axis(w_gu[l].astype(BF16), 1), op_gather_axis(w_oa[l].astype(BF16), 0),
                op_gather_axis(w_ob[l].astype(BF16), 0), op_gather_axis(w_out[l].astype(BF16), 0),
                op_gather_axis(w_down[l].astype(BF16), 0)]

    W_in, W_gu, W_oa, W_ob, W_out, W_down, W_dw = ([None] * L for _ in range(7))
    first_ops = gather_first(0)
    ag_first, _ = gather_begin("ag0a", first_ops, mod_land)

    def vec(a, l):
        return a[l:l + 1]

    def f_mod(rows, vecs, c_, i_):
        (xv,), (sc, sh) = rows, vecs
        return [[xv * (1.0 + sc) + sh]], []

    cwl = min(LN_CHUNK, D)
    ln_chunks = range(0, D, cwl)

    def f_convln(load, vec_, store, add_sum, i_):
        mu, rstd = _ln_stats(lambda c0: load(0, c0, cwl), D, cwl)
        for c0 in ln_chunks:
            nrm = (load(0, c0, cwl) - mu) * rstd * vec_(0, c0, cwl) + vec_(1, c0, cwl)
            store(0, 0, c0, nrm * _sig(nrm))

    def f_merge(rows, vecs, c_, i_):
        g_a, g_b, y_a, y_b = [r.astype(F32) for r in rows]
        return [[_sig(g_a) * y_a + _sig(g_b) * y_b]], []

    def f_ln(load, vec_, store, add_sum, i_):
        def t_of(c0):
            return alpha * load(0, c0, cwl) + (1.0 + vec_(0, c0, cwl)) * load(1, c0, cwl)
        mu, rstd = _ln_stats(t_of, D, cwl)
        for c0 in ln_chunks:
            y = (t_of(c0) - mu) * rstd * vec_(1, c0, cwl) + vec_(2, c0, cwl)
            store(0, 0, c0, y)
            store(1, 0, c0, y * (1.0 + vec_(3, c0, cwl)) + vec_(4, c0, cwl))

    saved = []
    xl = x0
    h = rowwise("modulate", f_mod, S, [_ri(x0)], [mods[0][1], mods[0][0]], [(BF16, 1)], W=D)[0]
    ag_first, _ = gather_forward("ag0a", first_ops, ag_first, h)
    W_in[0], W_dw[0] = gather_end("ag0a", ag_first, h)
    ag_next, next_ops = None, None

    def after_token(token):
        return [token] if token is not None else []

    def rope_heads(tile, extras, j):
        tc_, tl, th = extras
        heads = []
        for hh in range(tile.shape[1] // HEAD_DIM):
            xh = tile[:, hh * HEAD_DIM:(hh + 1) * HEAD_DIM]
            is_qk = j * (tile.shape[1] // HEAD_DIM) + hh < nq + nkv
            heads.append(jnp.where(is_qk, _rope(xh, tc_, tl, th, 1.0), xh))
        return jnp.concatenate(heads, axis=1)

    for l in range(L):
        sh_a, sc_a, gt_a, sh_f, sc_f, gt_f = mods[l]
        if l == 0:
            rest_ops = gather_rest(0)
            ag_rest, token = gather_begin("ag0b", rest_ops, W_in[0])
        elif l + 1 < L:
            next_ops = gather_first(l + 1) + gather_rest(l + 1)
            ag_next, token = gather_begin("ag%d" % (l + 1), next_ops, W_in[l])
        else:
            token = None
        qkv = matmul("in_qkv", h, W_in[l], "nn", BF16, n=Dqkv, deps=after_token(token),
                     epilogue=rope_heads, row_extras=[t_cos, t_lo, t_hi])
        p_glu = matmul("in_glu", h, W_in[l], "nn", F32, b_off=Dqkv, n=2 * D)
        p_gate = matmul("in_gate", h, W_in[l], "nn", BF16, b_off=Dqkv + 2 * D, n=2 * D)
        sinkcol = jnp.repeat(sink[l].reshape(nkv, GQA_GROUP), BLOCK, axis=1).reshape(nkv, GQA_GROUP * BLOCK, 1)
        att = attn_fwd(qkv, sinkcol, S, D, Dkv)
        uc = conv_fwd(p_glu, W_dw[l], S, D)
        token = None
        if l == 0:
            ag_rest, _ = gather_forward("ag0b", rest_ops, ag_rest, uc)
            W_gu[0], W_oa[0], W_ob[0], W_out[0], W_down[0] = gather_end("ag0b", ag_rest, uc)
            if L > 1:
                next_ops = gather_first(1) + gather_rest(1)
                ag_next, token = gather_begin("ag1", next_ops, W_gu[0])
        y_a = matmul("oa", att, W_oa[l], "nn", BF16, deps=after_token(token))
        z = rowwise("conv_ln", f_convln, S, [_ri(uc)], [vec(conv_ln_g, l), vec(conv_ln_b, l)], [(BF16, 1)], W=D,
                    lazy=True, rc=LN_ROWS)[0]
        y_b = matmul("ob", z, W_ob[l], "nn", BF16)
        merged = rowwise("merge", f_merge, S, [_ri(p_gate, col=0), _ri(p_gate, col=1), _ri(y_a), _ri(y_b)], [],
                         [(BF16, 1)], W=D, cw=_pick(D, 512))[0]
        r1 = matmul("out", merged, W_out[l], "nn", BF16)
        x1, h2 = rowwise("ln1", f_ln, S, [_ri(xl), _ri(r1)], [gt_a, vec(ln1_g, l), vec(ln1_b, l), sc_f, sh_f],
                         [(F32, 1), (BF16, 1)], W=D, lazy=True, rc=LN_ROWS)
        act, gate, up = matmul_swiglu(h2, W_gu[l], DFF)
        token = None
        if l + 1 < L:
            ag_next, token = gather_forward("ag%d" % (l + 1), next_ops, ag_next, act)
        f = matmul("down", act, W_down[l], "nn", BF16, deps=after_token(token))
        nsc, nsh = (mods[l + 1][1], mods[l + 1][0]) if l + 1 < L else (sc_a, sh_a)
        x2, h_next = rowwise("ln2", f_ln, S, [_ri(x1), _ri(f)], [gt_f, vec(ln2_g, l), vec(ln2_b, l), nsc, nsh],
                             [(F32, 1), (BF16, 1)], W=D, lazy=True, rc=LN_ROWS)
        saved.append(dict(x=xl, h=h, qkv=qkv, sinkcol=sinkcol, att=att, p_glu=p_glu, p_gate=p_gate, y_a=y_a, y_b=y_b,
                          uc=uc, z=z, merged=merged, r1=r1, x1=x1, h2=h2, gate=gate, up=up, act=act, f=f))
        xl, h = x2, h_next
        if l + 1 < L:
            (W_in[l + 1], W_dw[l + 1], W_gu[l + 1], W_oa[l + 1], W_ob[l + 1], W_out[l + 1],
             W_down[l + 1]) = gather_end("ag%d" % (l + 1), ag_next, x2)

    def f_loss(rows, vecs, c_, i_):
        y, t = rows
        e = y - t
        return [[e * (1.0 / D)]], [e * e]

    dy, err = rowwise("loss", f_loss, S, [_ri(xl), _ri(target)], [], [(F32, 1)], 1, W=D)
    loss = lax.psum(0.5 * jnp.sum(err) / D, AXES)

    def ln_bwd_passes(t_of, dout_of, g_of, r_of, gt_of, store, add_sum):
        mu, rstd = _ln_stats(t_of, D, cwl)
        m1 = _row_mean(lambda c0: dout_of(c0) * g_of(c0), D, cwl)
        m2 = _row_mean(lambda c0: dout_of(c0) * g_of(c0) * ((t_of(c0) - mu) * rstd), D, cwl)
        for c0 in ln_chunks:
            dout, xhat = dout_of(c0), (t_of(c0) - mu) * rstd
            dt = rstd * (dout * g_of(c0) - m1 - xhat * m2)
            store(0, 0, c0, (1.0 + gt_of(c0)) * dt)
            store(1, 0, c0, alpha * dt)
            add_sum(0, c0, dout * xhat)
            add_sum(1, c0, dout)
            add_sum(2, c0, dt * r_of(c0))

    def f_ln_bwd_last(load, vec_, store, add_sum, i_):
        ln_bwd_passes(lambda c0: alpha * load(1, c0, cwl) + (1.0 + vec_(0, c0, cwl)) * load(2, c0, cwl),
                      lambda c0: load(0, c0, cwl), lambda c0: vec_(1, c0, cwl), lambda c0: load(2, c0, cwl),
                      lambda c0: vec_(0, c0, cwl), store, add_sum)

    def f_ln_bwd(load, vec_, store, add_sum, i_):
        ln_bwd_passes(lambda c0: alpha * load(3, c0, cwl) + (1.0 + vec_(1, c0, cwl)) * load(4, c0, cwl),
                      lambda c0: load(0, c0, cwl) + load(1, c0, cwl).astype(F32) * (1.0 + vec_(0, c0, cwl)),
                      lambda c0: vec_(2, c0, cwl), lambda c0: load(4, c0, cwl), lambda c0: vec_(1, c0, cwl),
                      store, add_sum)
        for c0 in ln_chunks:
            dh = load(1, c0, cwl).astype(F32)
            add_sum(3, c0, dh * load(2, c0, cwl))
            add_sum(4, c0, dh)

    def f_swiglu_bwd(rows, vecs, c_, i_):
        da, gate, up = rows[0].astype(F32), rows[1].astype(F32), rows[2].astype(F32)
        sg = _sig(gate)
        return [[da * up * sg * (1.0 + gate * (1.0 - sg)), da * (gate * sg)]], []

    def f_gate_bwd(rows, vecs, c_, i_):
        dm, g_a, g_b, y_a, y_b = [r.astype(F32) for r in rows]
        sa, sb = _sig(g_a), _sig(g_b)
        return [[dm * sa], [dm * sb], [dm * y_a * sa * (1.0 - sa), dm * y_b * sb * (1.0 - sb)]], []

    def f_convln_bwd(load, vec_, store, add_sum, i_):
        mu, rstd = _ln_stats(lambda c0: load(1, c0, cwl), D, cwl)

        def parts(c0):
            xhat = (load(1, c0, cwl) - mu) * rstd
            nrm = xhat * vec_(0, c0, cwl) + vec_(1, c0, cwl)
            sg = _sig(nrm)
            return load(0, c0, cwl).astype(F32) * sg * (1.0 + nrm * (1.0 - sg)), xhat
        m1 = _row_mean(lambda c0: parts(c0)[0] * vec_(0, c0, cwl), D, cwl)

        def dyg_xhat(c0):
            dn, xhat = parts(c0)
            return dn * vec_(0, c0, cwl) * xhat
        m2 = _row_mean(dyg_xhat, D, cwl)
        for c0 in ln_chunks:
            dn, xhat = parts(c0)
            store(0, 0, c0, rstd * (dn * vec_(0, c0, cwl) - m1 - xhat * m2))
            add_sum(0, c0, dn * xhat)
            add_sum(1, c0, dn)

    def f_kv_combine(rows, vecs, c_, i_):
        k_lo, k_mid, k_hi, v_lo, v_mid, v_hi, tc_, tl, th = rows
        lo, hi = i_ > 0, i_ < nb - 1
        dk = jnp.where(lo, k_lo, 0.0) + k_mid + jnp.where(hi, k_hi, 0.0)
        dv = jnp.where(lo, v_lo, 0.0) + v_mid + jnp.where(hi, v_hi, 0.0)
        return [[_rope(dk, tc_, tl, th, -1.0)], [dv]], []

    def f_mod_bwd(rows, vecs, c_, i_):
        (dres, dh, xv), (sc,) = rows, vecs
        dh = dh.astype(F32)
        return [[dres + dh * (1.0 + sc)]], [dh * xv, dh]

    def flat(a):
        return a.reshape(-1, a.shape[-1])

    big_names = ("w_in", "w_gu", "w_oa", "w_ob", "w_out", "w_down")
    big_w = dict(w_in=w_in, w_gu=w_gu, w_oa=w_oa, w_ob=w_ob, w_out=w_out, w_down=w_down)
    big_m = dict(w_in=m_w_in, w_gu=m_w_gu, w_oa=m_w_oa, w_ob=m_w_ob, w_out=m_w_out, w_down=m_w_down)
    big_v = dict(w_in=v_w_in, w_gu=v_w_gu, w_oa=v_w_oa, w_ob=v_w_ob, w_out=v_w_out, w_down=v_w_down)
    big_res = {nm: None for nm in big_names}
    dmod = [None] * L
    small = dict(sink=[None] * L, conv_ln_g=[None] * L, conv_ln_b=[None] * L, ln1_g=[None] * L, ln1_b=[None] * L,
                 ln2_g=[None] * L, ln2_b=[None] * L)
    dwdw = [None] * L

    first, second, last = ("w_down", "w_gu"), ("w_out", "w_oa", "w_ob"), ("w_in",)

    def finish(names, tag, handle, after, l):
        landed = exchange_wait(tag + "_wait", handle, after)[len(names):]
        for nm, land in zip(names, landed):
            big_res[nm] = adamw_update("adamw_" + nm, big_w[nm], big_m[nm], big_v[nm], layer=l, landed=land,
                                       into=big_res[nm])
        return landed[0]

    dres, dh_next = dy, None
    rs_last, token = None, None
    for l in reversed(range(L)):
        sv = saved[l]
        sh_a, sc_a, gt_a, sh_f, sc_f, gt_f = mods[l]
        if dh_next is None:
            df, dres, d_g2, d_b2, d_gtf = rowwise(
                "ln2_bwd_last", f_ln_bwd_last, S, [_ri(dres), _ri(sv["x1"]), _ri(sv["f"])], [gt_f, vec(ln2_g, l)],
                [(BF16, 1), (F32, 1)], 3, W=D, lazy=True, rc=LN_ROWS)
            d_sca_next = d_sha_next = None
        else:
            df, dres, d_g2, d_b2, d_gtf, d_sca_next, d_sha_next = rowwise(
                "ln2_bwd", f_ln_bwd, S, [_ri(dres), _ri(dh_next), _ri(saved[l + 1]["x"]), _ri(sv["x1"]), _ri(sv["f"])],
                [mods[l + 1][1], gt_f, vec(ln2_g, l)], [(BF16, 1), (F32, 1)], 5, W=D, lazy=True, rc=LN_ROWS)
            dmod[l + 1][1], dmod[l + 1][0] = d_sca_next, d_sha_next
        dmod[l] = [None] * N_MOD
        dmod[l][5] = d_gtf
        small["ln2_g"][l], small["ln2_b"][l] = d_g2, d_b2
        g_down = matmul("d_w_down", sv["act"], df, "tn", BF16, deps=after_token(token))
        dact = matmul("d_act", df, W_down[l], "nt", BF16)
        dgu = rowwise("swiglu_bwd", f_swiglu_bwd, S, [_ri(dact), _ri(sv["gate"]), _ri(sv["up"])], [],
                      [(BF16, 2)], W=DFF, tr=128, cw=_pick(DFF, 512))[0]
        g_gu = matmul("d_w_gu", sv["h2"], dgu, "tn", BF16)
        rs_first, token = begin("rsa%d" % l, [op_scatter_axis(g_down, 0), op_scatter_axis(g_gu, 1)], g_gu)
        dh2 = matmul("d_h2", dgu, W_gu[l], "nt", BF16, deps=[token])
        dr1, dres, d_g1, d_b1, d_gta, d_scf, d_shf = rowwise(
            "ln1_bwd", f_ln_bwd, S, [_ri(dres), _ri(dh2), _ri(sv["x1"]), _ri(sv["x"]), _ri(sv["r1"])],
            [sc_f, gt_a, vec(ln1_g, l)], [(BF16, 1), (F32, 1)], 5, W=D, lazy=True, rc=LN_ROWS)
        dmod[l][2], dmod[l][4], dmod[l][3] = d_gta, d_scf, d_shf
        small["ln1_g"][l], small["ln1_b"][l] = d_g1, d_b1
        g_out = matmul("d_w_out", sv["merged"], dr1, "tn", BF16)
        dmerged = matmul("d_merged", dr1, W_out[l], "nt", BF16)
        if rs_last is not None:
            finish(last, "rsc%d" % (l + 1), rs_last, dmerged, l + 1)
        dy_a, dy_b, dp_gate = rowwise(
            "gate_bwd", f_gate_bwd, S,
            [_ri(dmerged), _ri(sv["p_gate"], col=0), _ri(sv["p_gate"], col=1), _ri(sv["y_a"]), _ri(sv["y_b"])], [],
            [(BF16, 1), (BF16, 1), (BF16, 2)], W=D, cw=_pick(D, 512))
        g_oa = matmul("d_w_oa", sv["att"], dy_a, "tn", BF16)
        datt = matmul("d_att", dy_a, W_oa[l], "nt", BF16)
        g_ob = matmul("d_w_ob", sv["z"], dy_b, "tn", BF16)
        dz = matmul("d_z", dy_b, W_ob[l], "nt", BF16)
        duc, d_cg, d_cb = rowwise("conv_ln_bwd", f_convln_bwd, S, [_ri(dz), _ri(sv["uc"])],
                                  [vec(conv_ln_g, l), vec(conv_ln_b, l)], [(F32, 1)], 2, W=D, lazy=True, rc=LN_ROWS)
        small["conv_ln_g"][l], small["conv_ln_b"][l] = d_cg, d_cb
        dga, dgb, dwdw[l] = conv_bwd(duc, sv["p_glu"], W_dw[l], S, D)
        before = finish(first, "rsa%d" % l, rs_first, dga, l)
        rs_second, token = begin("rsb%d" % l, [op_scatter_axis(g_out, 0), op_scatter_axis(g_oa, 0),
                                               op_scatter_axis(g_ob, 0)], before)
        dq_r, dkp, dvp, dsink = attn_bwd(sv["qkv"], sv["sinkcol"], datt, (t_cos, t_lo, t_hi), S, D, Dkv)
        small["sink"][l] = dsink[:, :GQA_GROUP, 0].reshape(1, nq)
        dk_r, dv_r = rowwise(
            "kv_combine", f_kv_combine, S,
            [_ri(dkp, lead=2, shift=-1), _ri(dkp, lead=1), _ri(dkp, lead=0, shift=1),
             _ri(dvp, lead=2, shift=-1), _ri(dvp, lead=1), _ri(dvp, lead=0, shift=1),
             _ri(t_cos, whole=True), _ri(t_lo, whole=True), _ri(t_hi, whole=True)],
            [], [(BF16, 1), (BF16, 1)], W=Dkv, tr=BLOCK, cw=HEAD_DIM, rc=32)
        dp = jnp.concatenate([dq_r, dk_r, dv_r, dga, dgb, dp_gate], axis=1)
        g_in = matmul("d_w_in", sv["h"], dp, "tn", BF16, deps=[token])
        rs_last, token = begin("rsc%d" % l, [op_scatter_axis(g_in, 1)], g_in)
        dh_next = matmul("d_h", dp, W_in[l], "nt", BF16, deps=[token])
        finish(second, "rsb%d" % l, rs_second, dh_next, l)
        token = None

    grad_x, d_sca0, d_sha0 = rowwise("mod_bwd", f_mod_bwd, S, [_ri(dres), _ri(dh_next), _ri(x0)], [mods[0][1]],
                                     [(F32, 1)], 2, W=D)
    dmod[0][1], dmod[0][0] = d_sca0, d_sha0
    finish(last, "rsc0", rs_last, grad_x, 0)

    small_names = ("b_ada", "sink", "conv_ln_g", "conv_ln_b", "ln1_g", "ln1_b", "ln2_g", "ln2_b")
    small_w = dict(b_ada=b_ada, sink=sink, conv_ln_g=conv_ln_g, conv_ln_b=conv_ln_b, ln1_g=ln1_g, ln1_b=ln1_b,
                   ln2_g=ln2_g, ln2_b=ln2_b)
    small_m = dict(b_ada=m_b_ada, sink=m_sink, conv_ln_g=m_conv_ln_g, conv_ln_b=m_conv_ln_b, ln1_g=m_ln1_g,
                   ln1_b=m_ln1_b, ln2_g=m_ln2_g, ln2_b=m_ln2_b)
    small_v = dict(b_ada=v_b_ada, sink=v_sink, conv_ln_g=v_conv_ln_g, conv_ln_b=v_conv_ln_b, ln1_g=v_ln1_g,
                   ln1_b=v_ln1_b, ln2_g=v_ln2_g, ln2_b=v_ln2_b)
    small_g = dict(small)
    small_g["b_ada"] = [jnp.concatenate(dmod[l], axis=1) for l in range(L)]
    sizes = [small_w[nm].size for nm in small_names]
    total = sum(sizes)
    padded = -(-total // (SMALL_W * ROW_CHUNK)) * (SMALL_W * ROW_CHUNK)

    def pack(parts):
        flat_ = jnp.concatenate([p.reshape(-1) for p in parts] + [jnp.zeros((padded - total,), F32)])
        return flat_.reshape(padded // SMALL_W, SMALL_W)

    g_pack = pack([jnp.concatenate(small_g[nm], axis=0) for nm in small_names])
    small_land, dwdw_land = exchange("gather_small", [op_gather_stack(g_pack),
                                                      op_scatter_axis(jnp.stack(dwdw, axis=0), 2)])
    small_out = adamw_update("adamw_small", pack([small_w[nm] for nm in small_names]),
                             pack([small_m[nm] for nm in small_names]), pack([small_v[nm] for nm in small_names]),
                             landed=small_land)

    def unpack(buf):
        flat_, out, o = buf.reshape(-1), {}, 0
        for nm, sz in zip(small_names, sizes):
            out[nm] = flat_[o:o + sz].reshape(small_w[nm].shape)
            o += sz
        return out
    small_res = [unpack(b) for b in small_out]

    dw_cols = w_dw.shape[2]

    def pad_dw(a):
        return jnp.pad(a, ((0, 0), (0, 32 - CONV_WIDTH), (0, 0))).reshape(L * 32, dw_cols)
    dw_out = adamw_update("adamw_w_dw", pad_dw(w_dw), pad_dw(m_w_dw), pad_dw(v_w_dw),
                          landed=dwdw_land.reshape(N_DEV, L * 32, dw_cols))
    dw_res = [a.reshape(L, 32, dw_cols)[:, :CONV_WIDTH] for a in dw_out]

    dmod_all = small_land.reshape(N_DEV, -1)[:, :L * N_MOD * D].reshape(N_DEV, L, N_MOD * D)
    dmod_mine = lax.dynamic_slice_in_dim(dmod_all, me * ada_cols, ada_cols, axis=2)
    dmod_pad = jnp.concatenate([dmod_mine, jnp.zeros_like(dmod_mine)], axis=0).astype(BF16)
    g_ada = jnp.stack([matmul("d_w_ada", c_pad, dmod_pad[:, l], "tn", F32) for l in range(L)], axis=0)
    ada_out = adamw_update("adamw_w_ada", flat(w_ada), flat(m_w_ada), flat(v_w_ada), grad=flat(g_ada))
    ada_res = [a.reshape(w_ada.shape) for a in ada_out]

    order = ("w_ada", "b_ada", "w_in", "sink", "w_dw", "conv_ln_g", "conv_ln_b", "w_oa", "w_ob", "w_out",
             "ln1_g", "ln1_b", "w_gu", "w_down", "ln2_g", "ln2_b")

    def result(nm, j):
        if nm == "w_ada":
            return ada_res[j]
        if nm == "w_dw":
            return dw_res[j]
        if nm in big_res:
            return big_res[nm][j]
        return small_res[j][nm]

    outs = [loss, grad_x.reshape(x.shape)]
    for j in range(4):
        outs += [result(nm, j) for nm in order]
    return tuple(outs)
```

```python
import math

import jax
import jax.numpy as jnp
from jax import lax
from jax.experimental import pallas as pl
from jax.experimental.pallas import tpu as pltpu

F32 = jnp.float32
BF16 = jnp.bfloat16
N_DEV = 8
AXES = ("x", "y", "c")
HEAD_DIM = 128
GQA_GROUP = 4
BLOCK = 128
ROPE_DIM = HEAD_DIM // 4
ROPE_HALF = ROPE_DIM // 2
ROPE_THETA = 500000.0
CONV_WIDTH = 31
CONV_HALO = 16
N_MOD = 6
LN_EPS = 1e-5
NEG_INF = -1e30
ADAM_LR, ADAM_B1, ADAM_B2, ADAM_EPS, ADAM_WD, ADAM_STEP = 0.001, 0.9, 0.999, 1e-08, 0.01, 10
VMEM_LIMIT = 56 * 1024 * 1024
LANE = 128
ROW_CHUNK = 16
SMALL_W = 512


def _params(*sem):
    return pltpu.CompilerParams(dimension_semantics=sem, vmem_limit_bytes=VMEM_LIMIT)


def _pick(dim, pref, mult=LANE):
    if dim <= pref:
        return dim
    best = None
    for t in range(mult, pref + 1, mult):
        if dim % t == 0:
            best = t
    assert best is not None, (dim, pref)
    return best


def matmul(name, a, b, mode, out_dtype, *, b_off=0, n=None, tm=1024, tn=1408, tk=2816, deps=(), epilogue=None,
           row_extras=()):
    if mode == "nn":
        (M, K), N = a.shape, (n or b.shape[1])
    elif mode == "tn":
        (K, M), N = a.shape, b.shape[1]
    else:
        (M, K), N = a.shape, b.shape[0]
    if mode == "tn":
        tn, tk = min(tn, 1024), max(tk, 4096)
    tm = _pick(M, tn, LANE) if mode == "tn" else _pick(M, tm, 16)
    tn, tk = _pick(math.gcd(N, b_off) if b_off else N, tn), _pick(K, tk)
    assert b_off % tn == 0 and N % tn == 0
    boff = b_off // tn
    nk = K // tk
    if mode == "nn":
        a_spec = pl.BlockSpec((tm, tk), lambda i, j, k: (i, k))
        b_spec = pl.BlockSpec((tk, tn), lambda i, j, k: (k, j + boff))
        dims = (((1,), (0,)), ((), ()))
    elif mode == "tn":
        a_spec = pl.BlockSpec((tk, tm), lambda i, j, k: (k, i))
        b_spec = pl.BlockSpec((tk, tn), lambda i, j, k: (k, j))
        dims = (((0,), (0,)), ((), ()))
    else:
        a_spec = pl.BlockSpec((tm, tk), lambda i, j, k: (i, k))
        b_spec = pl.BlockSpec((tn, tk), lambda i, j, k: (j, k))
        dims = (((1,), (1,)), ((), ()))
    n_ex, n_dep = len(row_extras), len(deps)
    assert epilogue is None or nk == 1

    def body(a_ref, b_ref, *rest):
        o_ref = rest[n_ex + n_dep]

        def dot():
            return lax.dot_general(a_ref[...].astype(BF16), b_ref[...].astype(BF16), dims, preferred_element_type=F32)
        if epilogue is not None:
            o_ref[...] = epilogue(dot(), [r[...] for r in rest[:n_ex]], pl.program_id(1)).astype(out_dtype)
        elif nk == 1:
            o_ref[...] = dot().astype(out_dtype)
        else:
            acc_ref = rest[n_ex + n_dep + 1]
            k = pl.program_id(2)

            @pl.when(k == 0)
            def _():
                acc_ref[...] = jnp.zeros_like(acc_ref)
            acc_ref[...] += dot()

            @pl.when(k == nk - 1)
            def _():
                o_ref[...] = acc_ref[...].astype(out_dtype)

    return pl.pallas_call(
        body, name=name, grid=(M // tm, N // tn, nk),
        in_specs=[a_spec, b_spec] + [pl.BlockSpec((tm, e.shape[1]), lambda i, j, k: (i, 0)) for e in row_extras]
        + [pl.BlockSpec(memory_space=pl.ANY)] * n_dep,
        out_specs=pl.BlockSpec((tm, tn), lambda i, j, k: (i, j)),
        out_shape=jax.ShapeDtypeStruct((M, N), out_dtype),
        scratch_shapes=[pltpu.VMEM((tm, tn), F32)] if nk > 1 else [],
        compiler_params=_params("parallel", "parallel", "arbitrary"),
    )(a, b, *row_extras, *deps)


def matmul_swiglu(h, w_gu, dff, *, tm=1024, tn=512):
    M, K = h.shape
    tm, tn = _pick(M, tm, 16), _pick(dff, tn)
    nj = dff // tn

    def body(a_ref, bg_ref, bu_ref, act_ref, gate_ref, up_ref):
        a = a_ref[...]
        gate = jnp.dot(a, bg_ref[...], preferred_element_type=F32)
        up = jnp.dot(a, bu_ref[...], preferred_element_type=F32)
        act_ref[...] = (gate * _sig(gate) * up).astype(BF16)
        gate_ref[...] = gate.astype(BF16)
        up_ref[...] = up.astype(BF16)

    out_spec = pl.BlockSpec((tm, tn), lambda i, j: (i, j))
    return pl.pallas_call(
        body, name="gu_swiglu", grid=(M // tm, nj),
        in_specs=[pl.BlockSpec((tm, K), lambda i, j: (i, 0)), pl.BlockSpec((K, tn), lambda i, j: (0, j)),
                  pl.BlockSpec((K, tn), lambda i, j: (0, nj + j))],
        out_specs=[out_spec] * 3,
        out_shape=[jax.ShapeDtypeStruct((M, dff), BF16)] * 3,
        compiler_params=_params("parallel", "parallel"),
    )(h, w_gu, w_gu)


def rowwise(name, fn, nrows, row_ins, vec_ins, row_outs, n_sums=0, *, W, tr=256, cw=None, rc=ROW_CHUNK,
            out_layer=None, into=None, lazy=False, deps=()):
    cw = cw or W
    tr = _pick(nrows, tr, ROW_CHUNK)
    rc = min(rc, tr)
    assert W % cw == 0 and nrows % ROW_CHUNK == 0 and tr % rc == 0 and rc % 8 == 0
    nrt = nrows // tr
    n_ri, n_v, n_ro = len(row_ins), len(vec_ins), len(row_outs)

    def row_spec(lead, colblk, shift, whole):
        w = cw if whole else W

        def rmap(i):
            return jnp.clip(i + shift, 0, nrt - 1) if shift else i
        if lead is None:
            return pl.BlockSpec((tr, w), lambda i: (rmap(i), colblk))
        return pl.BlockSpec((None, tr, w), lambda i: (lead, rmap(i), colblk))

    n_prev = len(deps) + (len(into) if into is not None else 0)
    in_specs = [row_spec(*ri[1:]) for ri in row_ins]
    in_specs += [pl.BlockSpec((1, W), lambda i: (0, 0)) for _ in vec_ins]
    in_specs += [pl.BlockSpec(memory_space=pl.ANY)] * n_prev
    if out_layer is None:
        out_specs = [pl.BlockSpec((tr, p * W), lambda i: (i, 0)) for _, p in row_outs]
        out_shape = [jax.ShapeDtypeStruct((nrows, p * W), dt) for dt, p in row_outs]
    else:
        out_specs = [pl.BlockSpec((None, tr, p * W), lambda i: (out_layer[0], i, 0)) for _, p in row_outs]
        out_shape = [jax.ShapeDtypeStruct((out_layer[1], nrows, p * W), dt) for dt, p in row_outs]
    out_specs += [pl.BlockSpec((1, W), lambda i: (0, 0)) for _ in range(n_sums)]
    out_shape += [jax.ShapeDtypeStruct((1, W), F32) for _ in range(n_sums)]

    def body(*refs):
        rin, vin = refs[:n_ri], refs[n_ri:n_ri + n_v]
        refs = refs[n_prev:]
        rout = refs[n_ri + n_v:n_ri + n_v + n_ro]
        sout = refs[n_ri + n_v + n_ro:n_ri + n_v + n_ro + n_sums]
        acc = refs[n_ri + n_v + n_ro + n_sums:]
        i = pl.program_id(0)
        if n_sums:
            @pl.when(i == 0)
            def _():
                for a in acc:
                    a[...] = jnp.zeros_like(a)
        def fold8(s):
            part = s[0:8]
            for q in range(1, rc // 8):
                part = part + s[8 * q:8 * q + 8]
            return part

        def lazy_step(r, carry):
            r0 = pl.multiple_of(r * rc, rc)

            def load(idx, c0, w):
                return rin[idx][pl.ds(r0, rc), c0:c0 + w]

            def vec(idx, c0, w):
                return vin[idx][:, c0:c0 + w]

            def store(o, piece, c0, val):
                rout[o][pl.ds(r0, rc), piece * W + c0:piece * W + c0 + val.shape[1]] = val.astype(row_outs[o][0])

            def add_sum(k, c0, val):
                acc[k][:, c0:c0 + val.shape[1]] += fold8(val)
            fn(load, vec, store, add_sum, i)
            return carry
        if lazy:
            lax.fori_loop(0, tr // rc, lazy_step, 0, unroll=2)
        for c in range(0 if lazy else W // cw):
            c0 = c * cw
            vecs = [v[:, c0:c0 + cw] for v in vin]

            def step(r, carry, c=c, c0=c0, vecs=vecs):
                r0 = pl.multiple_of(r * rc, rc)
                rows = [ref[pl.ds(r0, rc), :] if ri[4] else ref[pl.ds(r0, rc), c0:c0 + cw]
                        for ref, ri in zip(rin, row_ins)]
                outs, sums = fn(rows, vecs, c, i)
                for oref, (dt, _), pieces in zip(rout, row_outs, outs):
                    for pi, piece in enumerate(pieces):
                        oref[pl.ds(r0, rc), pi * W + c0:pi * W + c0 + cw] = piece.astype(dt)
                for a, s in zip(acc, sums):
                    a[:, c0:c0 + cw] += fold8(s)
                return carry
            lax.fori_loop(0, tr // rc, step, 0)
        if n_sums:
            @pl.when(i == nrt - 1)
            def _():
                for o, a in zip(sout, acc):
                    o[...] = jnp.sum(a[...], axis=0, keepdims=True)

    res = pl.pallas_call(
        body, name=name, grid=(nrt,), in_specs=in_specs, out_specs=out_specs, out_shape=out_shape,
        scratch_shapes=[pltpu.VMEM((8, W), F32) for _ in range(n_sums)],
        input_output_aliases={n_ri + n_v + len(deps) + q: q for q in range(n_prev - len(deps))},
        compiler_params=_params("arbitrary"),
    )(*[ri[0] for ri in row_ins], *vec_ins, *deps, *(into or ()))
    return res


def _ri(arr, lead=None, col=0, shift=0, whole=False):
    return (arr, lead, col, shift, whole)


def _sig(x):
    return jax.nn.sigmoid(x)


LN_CHUNK = 256
LN_ROWS = 64


def _row_mean(make, width, cw):
    acc = None
    for c0 in range(0, width, cw):
        t = make(c0)
        for q in range(cw // LANE):
            part = t[:, q * LANE:(q + 1) * LANE]
            acc = part if acc is None else acc + part
    return jnp.sum(acc, axis=-1, keepdims=True) * (1.0 / width)


def _ln_stats(t_of, width, cw):
    mu = _row_mean(t_of, width, cw)

    def sq(c0):
        d = t_of(c0) - mu
        return d * d
    return mu, lax.rsqrt(_row_mean(sq, width, cw) + LN_EPS)


def _rope(x, cos, s_lo, s_hi, sign):
    up = pltpu.roll(x, HEAD_DIM - ROPE_HALF, 1)
    down = pltpu.roll(x, ROPE_HALF, 1)
    return x * cos + sign * (up * s_lo + down * s_hi)


def rope_tables(S):
    pos = jnp.arange(S, dtype=F32)
    inv_freq = ROPE_THETA ** (-jnp.arange(0, ROPE_DIM, 2, dtype=F32) / ROPE_DIM)
    ang = pos[:, None] * inv_freq[None, :]
    cos, sin = jnp.cos(ang), jnp.sin(ang)
    ones = jnp.ones((S, HEAD_DIM - ROPE_DIM), F32)
    zeros = jnp.zeros((S, HEAD_DIM - ROPE_DIM), F32)
    zh = jnp.zeros((S, ROPE_HALF), F32)
    t_cos = jnp.concatenate([cos, cos, ones], axis=1)
    t_lo = jnp.concatenate([-sin, zh, zeros], axis=1)
    t_hi = jnp.concatenate([zh, sin, zeros], axis=1)
    return t_cos, t_lo, t_hi


ATTN_HEADS_PER_STEP = 4


def _attn_specs(S, D, Dkv):
    nb, nkv, qb = S // BLOCK, Dkv // HEAD_DIM, D // HEAD_DIM
    hps = math.gcd(ATTN_HEADS_PER_STEP, nkv)
    assert qb % hps == 0 and (qb + nkv) % hps == 0
    gw = GQA_GROUP * HEAD_DIM
    q_spec = pl.BlockSpec((BLOCK, hps * gw), lambda h, n: (n, h))

    def band(col0):
        c0 = col0 // hps
        return [pl.BlockSpec((BLOCK, hps * HEAD_DIM), lambda h, n: (jnp.maximum(n - 1, 0), c0 + h)),
                pl.BlockSpec((BLOCK, hps * HEAD_DIM), lambda h, n: (n, c0 + h)),
                pl.BlockSpec((BLOCK, hps * HEAD_DIM), lambda h, n: (jnp.minimum(n + 1, nb - 1), c0 + h))]
    sink_spec = pl.BlockSpec((hps, GQA_GROUP * BLOCK, 1), lambda h, n: (h, 0, 0))
    return nb, nkv, hps, gw, q_spec, band(qb), band(qb + nkv), sink_spec


def _attn_valid(n, S):
    shape = (GQA_GROUP * BLOCK, 3 * BLOCK)
    row = lax.broadcasted_iota(jnp.int32, shape, 0) & (BLOCK - 1)
    col = lax.broadcasted_iota(jnp.int32, shape, 1)
    rel = col - BLOCK - row
    kpos = (n - 1) * BLOCK + col
    return (jnp.abs(rel) <= BLOCK) & (kpos >= 0) & (kpos < S)


def _head(ref, hh, width):
    return ref[:, hh * width:(hh + 1) * width]


def _attn_probs(q, k_blocks, sink, valid):
    qs = jnp.concatenate([q[:, g * HEAD_DIM:(g + 1) * HEAD_DIM] for g in range(GQA_GROUP)], axis=0)
    kb = jnp.concatenate(k_blocks, axis=0)
    s = lax.dot_general(qs, kb, (((1,), (1,)), ((), ())), preferred_element_type=F32) * (HEAD_DIM ** -0.5)
    s = jnp.where(valid, s, NEG_INF)
    m = jnp.maximum(jnp.max(s, axis=-1, keepdims=True), sink)
    p = jnp.exp(s - m)
    e_sink = jnp.exp(sink - m)
    denom = jnp.sum(p, axis=-1, keepdims=True) + e_sink
    return qs, kb, p / denom, e_sink / denom


def attn_fwd(qkv, sinkcol, S, D, Dkv):
    nb, nkv, hps, gw, q_spec, k_specs, v_specs, sink_spec = _attn_specs(S, D, Dkv)

    def body(q_ref, k0, k1, k2, v0, v1, v2, sink_ref, o_ref):
        valid = _attn_valid(pl.program_id(1), S)
        for hh in range(hps):
            _, _, w, _ = _attn_probs(_head(q_ref, hh, gw), [_head(r, hh, HEAD_DIM) for r in (k0, k1, k2)],
                                     sink_ref[hh], valid)
            vb = jnp.concatenate([_head(r, hh, HEAD_DIM) for r in (v0, v1, v2)], axis=0)
            o = jnp.dot(w.astype(BF16), vb, preferred_element_type=F32)
            for g in range(GQA_GROUP):
                c0 = hh * gw + g * HEAD_DIM
                o_ref[:, c0:c0 + HEAD_DIM] = o[g * BLOCK:(g + 1) * BLOCK].astype(BF16)

    return pl.pallas_call(
        body, name="attn_fwd", grid=(nkv // hps, nb),
        in_specs=[q_spec, *k_specs, *v_specs, sink_spec],
        out_specs=pl.BlockSpec((BLOCK, hps * gw), lambda h, n: (n, h)),
        out_shape=jax.ShapeDtypeStruct((S, D), BF16),
        compiler_params=_params("parallel", "arbitrary"),
    )(qkv, qkv, qkv, qkv, qkv, qkv, qkv, sinkcol)


def attn_bwd(qkv, sinkcol, datt, tables, S, D, Dkv):
    nb, nkv, hps, gw, q_spec, k_specs, v_specs, sink_spec = _attn_specs(S, D, Dkv)

    def body(q_ref, k0, k1, k2, v0, v1, v2, sink_ref, do_ref, tc_ref, tl_ref, th_ref, dq_ref, dkp_ref, dvp_ref,
             dsink_ref):
        n = pl.program_id(1)
        valid = _attn_valid(n, S)
        tc_, tl, th = tc_ref[...], tl_ref[...], th_ref[...]

        @pl.when(n == 0)
        def _():
            dsink_ref[...] = jnp.zeros_like(dsink_ref)
        for hh in range(hps):
            qs, kb, w, w_sink = _attn_probs(_head(q_ref, hh, gw), [_head(r, hh, HEAD_DIM) for r in (k0, k1, k2)],
                                            sink_ref[hh], valid)
            vb = jnp.concatenate([_head(r, hh, HEAD_DIM) for r in (v0, v1, v2)], axis=0)
            do = _head(do_ref, hh, gw)
            dos = jnp.concatenate([do[:, g * HEAD_DIM:(g + 1) * HEAD_DIM] for g in range(GQA_GROUP)], axis=0)
            dv = lax.dot_general(w.astype(BF16), dos, (((0,), (0,)), ((), ())), preferred_element_type=F32)
            dw = lax.dot_general(dos, vb, (((1,), (1,)), ((), ())), preferred_element_type=F32)
            delta = jnp.sum(w * dw, axis=-1, keepdims=True)
            ds = (w * (dw - delta) * (HEAD_DIM ** -0.5)).astype(BF16)
            dq = jnp.dot(ds, kb, preferred_element_type=F32)
            dk = lax.dot_general(ds, qs, (((0,), (0,)), ((), ())), preferred_element_type=F32)
            for g in range(GQA_GROUP):
                c0 = hh * gw + g * HEAD_DIM
                dq_ref[:, c0:c0 + HEAD_DIM] = _rope(dq[g * BLOCK:(g + 1) * BLOCK], tc_, tl, th, -1.0).astype(BF16)
            for j in range(3):
                dkp_ref[j, :, hh * HEAD_DIM:(hh + 1) * HEAD_DIM] = dk[j * BLOCK:(j + 1) * BLOCK]
                dvp_ref[j, :, hh * HEAD_DIM:(hh + 1) * HEAD_DIM] = dv[j * BLOCK:(j + 1) * BLOCK]
            t = w_sink * delta
            for g in range(GQA_GROUP):
                dsink_ref[hh, g:g + 1, :] -= jnp.sum(t[g * BLOCK:(g + 1) * BLOCK], axis=0, keepdims=True)

    part_spec = pl.BlockSpec((3, BLOCK, hps * HEAD_DIM), lambda h, n: (0, n, h))
    return pl.pallas_call(
        body, name="attn_bwd", grid=(nkv // hps, nb),
        in_specs=[q_spec, *k_specs, *v_specs, sink_spec, pl.BlockSpec((BLOCK, hps * gw), lambda h, n: (n, h))]
        + [pl.BlockSpec((BLOCK, HEAD_DIM), lambda h, n: (n, 0))] * 3,
        out_specs=[pl.BlockSpec((BLOCK, hps * gw), lambda h, n: (n, h)), part_spec, part_spec,
                   pl.BlockSpec((hps, 8, LANE), lambda h, n: (h, 0, 0))],
        out_shape=[jax.ShapeDtypeStruct((S, D), BF16), jax.ShapeDtypeStruct((3, S, Dkv), F32),
                   jax.ShapeDtypeStruct((3, S, Dkv), F32), jax.ShapeDtypeStruct((nkv, 8, LANE), F32)],
        compiler_params=_params("parallel", "arbitrary"),
    )(qkv, qkv, qkv, qkv, qkv, qkv, qkv, sinkcol, datt, *tables)


CONV_TC = 128
CONV_ROWS = 64


def _shift_rows(win, b):
    return win if b == 0 else pltpu.roll(win, win.shape[0] - b, 0)


def _conv_fill_u(ga_ref, gb_ref, upad_ref, S):
    tc = upad_ref.shape[1]
    zero = jnp.zeros((CONV_HALO, tc), F32)
    upad_ref[0:CONV_HALO, :] = zero
    upad_ref[S + CONV_HALO:S + 2 * CONV_HALO, :] = zero

    def fill(r, carry):
        r0 = pl.multiple_of(r * CONV_ROWS, CONV_ROWS)
        upad_ref[pl.ds(r0 + CONV_HALO, CONV_ROWS), :] = ga_ref[pl.ds(r0, CONV_ROWS), :] * _sig(gb_ref[pl.ds(r0, CONV_ROWS), :])
        return carry
    lax.fori_loop(0, S // CONV_ROWS, fill, 0)


def conv_fwd(pglu, wdw, S, D):
    tc = min(CONV_TC, D)
    nct = D // tc

    def body(ga_ref, gb_ref, w_ref, o_ref, upad_ref):
        _conv_fill_u(ga_ref, gb_ref, upad_ref, S)
        w = w_ref[...]

        def step(r, carry):
            r0 = pl.multiple_of(r * CONV_ROWS, CONV_ROWS)
            win = upad_ref[pl.ds(r0, CONV_ROWS + 2 * CONV_HALO), :]
            acc = jnp.zeros((CONV_ROWS, tc), F32)
            for b in range(8):
                shifted = _shift_rows(win, b)
                for k in range(CONV_WIDTH):
                    o = k + CONV_HALO - CONV_WIDTH // 2
                    if o % 8 == b:
                        acc = acc + w[k:k + 1, :] * shifted[o - b:o - b + CONV_ROWS, :]
            o_ref[pl.ds(r0, CONV_ROWS), :] = acc
            return carry
        lax.fori_loop(0, S // CONV_ROWS, step, 0)

    return pl.pallas_call(
        body, name="conv_fwd", grid=(nct,),
        in_specs=[pl.BlockSpec((S, tc), lambda j: (0, j)), pl.BlockSpec((S, tc), lambda j: (0, nct + j)),
                  pl.BlockSpec((CONV_WIDTH, tc), lambda j: (0, j))],
        out_specs=pl.BlockSpec((S, tc), lambda j: (0, j)),
        out_shape=jax.ShapeDtypeStruct((S, D), F32),
        scratch_shapes=[pltpu.VMEM((S + 2 * CONV_HALO, tc), F32)],
        compiler_params=_params("parallel"),
    )(pglu, pglu, wdw)


def conv_bwd(duc, pglu, wdw, S, D):
    tc = min(CONV_TC, D)
    nct = D // tc
    half = CONV_WIDTH // 2

    def body(d_ref, ga_ref, gb_ref, w_ref, dga_ref, dgb_ref, dw_ref, upad_ref, dpad_ref, dwacc_ref):
        _conv_fill_u(ga_ref, gb_ref, upad_ref, S)
        zero = jnp.zeros((CONV_HALO, tc), F32)
        dpad_ref[0:CONV_HALO, :] = zero
        dpad_ref[S + CONV_HALO:S + 2 * CONV_HALO, :] = zero

        def fill(r, carry):
            r0 = pl.multiple_of(r * CONV_ROWS, CONV_ROWS)
            dpad_ref[pl.ds(r0 + CONV_HALO, CONV_ROWS), :] = d_ref[pl.ds(r0, CONV_ROWS), :]
            return carry
        lax.fori_loop(0, S // CONV_ROWS, fill, 0)
        dwacc_ref[...] = jnp.zeros_like(dwacc_ref)
        w = w_ref[...]

        def step(r, carry):
            r0 = pl.multiple_of(r * CONV_ROWS, CONV_ROWS)
            uwin = upad_ref[pl.ds(r0, CONV_ROWS + 2 * CONV_HALO), :]
            dwin = dpad_ref[pl.ds(r0, CONV_ROWS + 2 * CONV_HALO), :]
            d = dwin[CONV_HALO:CONV_HALO + CONV_ROWS, :]
            du = jnp.zeros((CONV_ROWS, tc), F32)
            for b in range(8):
                d_shifted, u_shifted = _shift_rows(dwin, b), _shift_rows(uwin, b)
                for k in range(CONV_WIDTH):
                    o = CONV_HALO + half - k
                    if o % 8 == b:
                        du = du + w[k:k + 1, :] * d_shifted[o - b:o - b + CONV_ROWS, :]
                    o = CONV_HALO + k - half
                    if o % 8 == b:
                        prod = d * u_shifted[o - b:o - b + CONV_ROWS, :]
                        part = prod[0:8]
                        for q in range(1, CONV_ROWS // 8):
                            part = part + prod[8 * q:8 * q + 8]
                        dwacc_ref[k] += part
            ga = ga_ref[pl.ds(r0, CONV_ROWS), :]
            sg = _sig(gb_ref[pl.ds(r0, CONV_ROWS), :])
            dga_ref[pl.ds(r0, CONV_ROWS), :] = (du * sg).astype(BF16)
            dgb_ref[pl.ds(r0, CONV_ROWS), :] = (du * ga * sg * (1.0 - sg)).astype(BF16)
            return carry
        lax.fori_loop(0, S // CONV_ROWS, step, 0)
        dw_ref[...] = jnp.sum(dwacc_ref[...], axis=1)

    return pl.pallas_call(
        body, name="conv_bwd", grid=(nct,),
        in_specs=[pl.BlockSpec((S, tc), lambda j: (0, j)), pl.BlockSpec((S, tc), lambda j: (0, j)),
                  pl.BlockSpec((S, tc), lambda j: (0, nct + j)), pl.BlockSpec((CONV_WIDTH, tc), lambda j: (0, j))],
        out_specs=[pl.BlockSpec((S, tc), lambda j: (0, j)), pl.BlockSpec((S, tc), lambda j: (0, j)),
                   pl.BlockSpec((32, tc), lambda j: (0, j))],
        out_shape=[jax.ShapeDtypeStruct((S, D), BF16), jax.ShapeDtypeStruct((S, D), BF16),
                   jax.ShapeDtypeStruct((32, D), F32)],
        scratch_shapes=[pltpu.VMEM((S + 2 * CONV_HALO, tc), F32), pltpu.VMEM((S + 2 * CONV_HALO, tc), F32),
                        pltpu.VMEM((32, 8, tc), F32)],
        compiler_params=_params("parallel"),
    )(duc, pglu, pglu, wdw)


def exchange(name, ops, deps=()):
    n = len(ops)

    def body(*refs):
        xs, outs = refs[:n], refs[n + len(deps):2 * n + len(deps)]
        send, recv, lsem = refs[2 * n + len(deps):]
        mx, my, mc = lax.axis_index("x"), lax.axis_index("y"), lax.axis_index("c")
        me = 4 * mx + 2 * my + mc
        local = [pltpu.make_async_copy(op[3](xs[i], me), op[4](outs[i], me), lsem.at[i]) for i, op in enumerate(ops)]
        for cp in local:
            cp.start()
        copies = []
        for k in range(1, N_DEV):
            px = 1 - mx if k & 4 else mx
            py = 1 - my if k & 2 else my
            pc = 1 - mc if k & 1 else mc
            peer = 4 * px + 2 * py + pc
            for i, op in enumerate(ops):
                cp = pltpu.make_async_remote_copy(
                    src_ref=op[3](xs[i], peer), dst_ref=op[4](outs[i], me),
                    send_sem=send.at[i, k - 1], recv_sem=recv.at[i, k - 1],
                    device_id=(px, py, pc), device_id_type=pl.DeviceIdType.MESH)
                cp.start()
                copies.append(cp)
        for cp in copies:
            cp.wait()
        for cp in local:
            cp.wait()

    any_spec = pl.BlockSpec(memory_space=pl.ANY)
    return pl.pallas_call(
        body, name=name,
        in_specs=[any_spec] * (n + len(deps)), out_specs=[any_spec] * n,
        out_shape=[jax.ShapeDtypeStruct(op[1], op[2]) for op in ops],
        scratch_shapes=[pltpu.SemaphoreType.DMA((n, N_DEV - 1)), pltpu.SemaphoreType.DMA((n, N_DEV - 1)),
                        pltpu.SemaphoreType.DMA((n,))],
        compiler_params=pltpu.CompilerParams(has_side_effects=True),
    )(*[op[0] for op in ops], *deps)


_HBM = pl.BlockSpec(memory_space=pltpu.HBM)
_SEM = pl.BlockSpec(memory_space=pltpu.SEMAPHORE)
_EFFECT = pltpu.SideEffectType.DATAFLOW_SIDE_EFFECTING


def _me_and_peers():
    mx, my, mc = lax.axis_index("x"), lax.axis_index("y"), lax.axis_index("c")
    peers = []
    for k in range(1, N_DEV):
        px = 1 - mx if k & 4 else mx
        py = 1 - my if k & 2 else my
        pc = 1 - mc if k & 1 else mc
        peers.append((k, (px, py, pc), 4 * px + 2 * py + pc))
    return 4 * mx + 2 * my + mc, peers


PLACE_STEPS = 4


def exchange_local(name, ops, me_arr, deps=()):
    n = len(ops)
    in_specs, out_specs = [], []
    for op in ops:
        kind, axis = op[5]
        rows, cols = op[0].shape if kind == "gather" else op[1][1:]
        steps = PLACE_STEPS if rows % (PLACE_STEPS * ROW_CHUNK) == 0 else 1
        tr = rows // steps

        def row(i, steps=steps):
            return i if steps > 1 else 0

        def window(i, me, axis=axis, steps=steps):
            return (me[0] * steps + row(i, steps), 0) if axis == 0 else (row(i, steps), me[0])
        if kind == "gather":
            in_specs.append(pl.BlockSpec((tr, cols), lambda i, me, row=row: (row(i), 0)))
            out_specs.append(pl.BlockSpec((tr, cols), window))
        else:
            in_specs.append(pl.BlockSpec((tr, cols), window))
            out_specs.append(pl.BlockSpec((None, tr, cols), lambda i, me, row=row: (me[0], row(i), 0)))

    def body(me_ref, *refs):
        for i in range(n):
            refs[n + len(deps) + i][...] = refs[i][...]

    return pl.pallas_call(
        body, name=name,
        grid_spec=pltpu.PrefetchScalarGridSpec(
            num_scalar_prefetch=1, grid=(PLACE_STEPS,),
            in_specs=in_specs + [pl.BlockSpec(memory_space=pl.ANY)] * len(deps), out_specs=out_specs),
        out_shape=[jax.ShapeDtypeStruct(op[1], op[2]) for op in ops],
        compiler_params=_params("arbitrary"),
    )(me_arr, *[op[0] for op in ops], *deps)


ALL_PEERS = (1, 2, 3, 4, 5, 6, 7)
SIBLING = 1
SAME_CORE = (2, 4, 6)


def _flipped(pos, k):
    p = (1 - pos[0] if k & 4 else pos[0], 1 - pos[1] if k & 2 else pos[1], 1 - pos[2] if k & 1 else pos[2])
    return p, 4 * p[0] + 2 * p[1] + p[2]


def direct_copies(ops, ks):
    n = len(ops)

    def build(refs, send, recv):
        xs, lands = refs[:n], refs[n:2 * n]
        pos = (lax.axis_index("x"), lax.axis_index("y"), lax.axis_index("c"))
        _, me = _flipped(pos, 0)
        copies = []
        for j, k in enumerate(ks):
            peer_id, peer = _flipped(pos, k)
            for i, op in enumerate(ops):
                s = i * len(ks) + j
                copies.append(pltpu.make_async_remote_copy(
                    src_ref=op[3](xs[i], peer), dst_ref=op[4](lands[i], me), send_sem=send.at[s], recv_sem=recv.at[s],
                    device_id=peer_id, device_id_type=pl.DeviceIdType.MESH))
        return copies
    return n * len(ks), build


def forward_copies(ops, ks):
    n = len(ops)

    def build(refs, send, recv):
        lands = refs[:n]
        pos = (lax.axis_index("x"), lax.axis_index("y"), lax.axis_index("c"))
        sibling_id, _ = _flipped(pos, SIBLING)
        copies = []
        for j, k in enumerate(ks):
            _, origin = _flipped(pos, k)
            for i, op in enumerate(ops):
                s = i * len(ks) + j
                region = op[4](lands[i], origin)
                copies.append(pltpu.make_async_remote_copy(
                    src_ref=region, dst_ref=region, send_sem=send.at[s], recv_sem=recv.at[s],
                    device_id=sibling_id, device_id_type=pl.DeviceIdType.MESH))
        return copies
    return n * len(ks), build


def exchange_start(name, arrays, plan):
    n_sems, build = plan
    n = len(arrays)

    def body(*refs):
        for cp in build(refs[:n], refs[n], refs[n + 1]):
            cp.start()
        refs[-1][...] = jnp.zeros_like(refs[-1])

    args = [pltpu.with_memory_space_constraint(a, pltpu.HBM) for a in arrays]
    sems = pltpu.SemaphoreType.DMA((n_sems,))
    outs = pl.pallas_call(
        body, name=name,
        out_shape=(sems, sems, *[pltpu.HBM(a.shape, a.dtype) for a in args], jax.ShapeDtypeStruct((8, LANE), F32)),
        in_specs=[_HBM] * n,
        out_specs=(_SEM, _SEM, *[_HBM] * n, pl.BlockSpec(memory_space=pltpu.VMEM)),
        input_output_aliases={i: 2 + i for i in range(n)},
        compiler_params=pltpu.CompilerParams(has_side_effects=_EFFECT),
    )(*args)
    return (build, outs[0], outs[1], outs[2:2 + n]), outs[-1]


def exchange_wait(name, handle, after):
    build, send, recv, thru = handle
    n = len(thru)

    def body(*refs):
        for cp in build(refs[:n], refs[n], refs[n + 1]):
            cp.wait_send()
            cp.wait_recv()

    return pl.pallas_call(
        body, name=name,
        out_shape=[pltpu.HBM(a.shape, a.dtype) for a in thru],
        in_specs=[_HBM] * n + [_SEM, _SEM, pl.BlockSpec(memory_space=pl.ANY)],
        out_specs=[_HBM] * n,
        input_output_aliases={i: i for i in range(n)},
        compiler_params=pltpu.CompilerParams(has_side_effects=_EFFECT),
    )(*thru, send, recv, after)


def _whole(ref, q):
    return ref


def _slot(ref, q):
    return ref.at[q]


def op_gather_stack(x):
    return (x, (N_DEV,) + x.shape, x.dtype, _whole, _slot)


def op_gather_axis(x, axis):
    size = x.shape[axis]
    shape = x.shape[:axis] + (N_DEV * size,) + x.shape[axis + 1:]

    def dst(ref, q):
        idx = [slice(None)] * len(shape)
        idx[axis] = pl.ds(pl.multiple_of(q * size, size), size)
        return ref.at[tuple(idx)]
    return (x, shape, x.dtype, _whole, dst, ("gather", axis))


def op_scatter_axis(x, axis):
    size = x.shape[axis] // N_DEV
    shape = x.shape[:axis] + (size,) + x.shape[axis + 1:]

    def src(ref, q):
        idx = [slice(None)] * len(shape)
        idx[axis] = pl.ds(pl.multiple_of(q * size, size), size)
        return ref.at[tuple(idx)]
    return (x, (N_DEV,) + shape, x.dtype, src, _slot, ("scatter", axis))


def _adamw(w, g, m, v):
    m = ADAM_B1 * m + (1.0 - ADAM_B1) * g
    v = ADAM_B2 * v + (1.0 - ADAM_B2) * (g * g)
    m_hat = m / (1.0 - ADAM_B1 ** ADAM_STEP)
    v_hat = v / (1.0 - ADAM_B2 ** ADAM_STEP)
    delta = -ADAM_LR * (m_hat / (jnp.sqrt(v_hat) + ADAM_EPS) + ADAM_WD * w)
    return delta, m, v


def adamw_update(name, w, m, v, *, layer=None, landed=None, grad=None, into=None, deps=()):
    R, W = w.shape[-2:]
    n_g = N_DEV if landed is not None else 1

    def fn(rows, vecs, c, i):
        g = rows[0].astype(F32)
        for q in range(1, n_g):
            g = g + rows[q].astype(F32)
        wv, mv, vv = rows[n_g:]
        delta, m2, v2 = _adamw(wv, g, mv, vv)
        return [[g], [delta], [m2], [v2]], []

    g_ins = [_ri(landed, lead=q) for q in range(N_DEV)] if landed is not None else [_ri(grad)]
    cw = LANE if W % LANE == 0 else W
    return rowwise(name, fn, R, g_ins + [_ri(w, lead=layer), _ri(m, lead=layer), _ri(v, lead=layer)], [],
                   [(F32, 1)] * 4, W=W, tr=128, cw=cw, rc=32,
                   out_layer=None if layer is None else (layer, w.shape[0]), into=into, deps=deps)


def kernel(x, c, w_ada, b_ada, w_in, sink, w_dw, conv_ln_g, conv_ln_b, w_oa, w_ob, w_out, ln1_g, ln1_b, w_gu, w_down, ln2_g, ln2_b, loss_target, m_w_ada, m_b_ada, m_w_in, m_sink, m_w_dw, m_conv_ln_g, m_conv_ln_b, m_w_oa, m_w_ob, m_w_out, m_ln1_g, m_ln1_b, m_w_gu, m_w_down, m_ln2_g, m_ln2_b, v_w_ada, v_b_ada, v_w_in, v_sink, v_w_dw, v_conv_ln_g, v_conv_ln_b, v_w_oa, v_w_ob, v_w_out, v_ln1_g, v_ln1_b, v_w_gu, v_w_down, v_ln2_g, v_ln2_b):
    L = w_ada.shape[0]
    S, D = x.shape[1], x.shape[2]
    Dkv = D // GQA_GROUP
    Dqkv = D + 2 * Dkv
    DFF = w_down.shape[1] * N_DEV
    nq, nkv, nb = D // HEAD_DIM, Dkv // HEAD_DIM, S // BLOCK
    alpha = (2.0 * L) ** 0.25
    me = 4 * lax.axis_index("x") + 2 * lax.axis_index("y") + lax.axis_index("c")
    x0 = x.reshape(S, D)
    target = loss_target.reshape(S, D)
    t_cos, t_lo, t_hi = rope_tables(S)

    c_act = jax.nn.silu(c)
    c_all = exchange("gather_c", [op_gather_stack(c_act)])[0].reshape(N_DEV, D)
    c_pad = jnp.concatenate([c_all, jnp.zeros_like(c_all)], axis=0).astype(BF16)
    ada_cols = w_ada.shape[2]
    mod_part = jnp.stack([matmul("mod_mm", c_pad, w_ada[l], "nn", F32)[:N_DEV] for l in range(L)], axis=1)
    mod_land = exchange("scatter_mod", [op_scatter_axis(mod_part, 0)])[0]
    mod = jnp.transpose(mod_land.reshape(N_DEV, L, ada_cols), (1, 0, 2)).reshape(L, N_MOD * D) + b_ada
    mods = [[mod[l:l + 1, j * D:(j + 1) * D] for j in range(N_MOD)] for l in range(L)]

    me_arr = me.astype(jnp.int32).reshape(1)

    def begin(tag, ops, dep, ks=ALL_PEERS):
        lands = exchange_local(tag + "_local", ops, me_arr, [dep])
        return exchange_start(tag + "_start", [op[0] for op in ops] + list(lands), direct_copies(ops, ks))

    def gather_begin(tag, ops, dep):
        return begin(tag, ops, dep, (SIBLING,) + SAME_CORE)

    def gather_forward(tag, ops, handle, after):
        lands = exchange_wait(tag + "_wait", handle, after)[len(ops):]
        return exchange_start(tag + "_fwd_start", list(lands), forward_copies(ops, SAME_CORE))

    def gather_end(tag, handle, after):
        return exchange_wait(tag + "_fwd_wait", handle, after)

    def gather_first(l):
        return [op_gather_axis(w_in[l].astype(BF16), 1), op_gather_axis(w_dw[l], 1)]

    def gather_rest(l):
        return [op_gather_axis(w_gu[l].astype(BF16), 1), op_gather_axis(w_oa[l].astype(BF16), 0),
                op_gather_axis(w_ob[l].astype(BF16), 0), op_gather_axis(w_out[l].astype(BF16), 0),
                op_gather_axis(w_down[l].astype(BF16), 0)]

    W_in, W_gu, W_oa, W_ob, W_out, W_down, W_dw = ([None] * L for _ in range(7))
    first_ops = gather_first(0)
    ag_first, _ = gather_begin("ag0a", first_ops, mod_land)

    def vec(a, l):
        return a[l:l + 1]

    def f_mod(rows, vecs, c_, i_):
        (xv,), (sc, sh) = rows, vecs
        return [[xv * (1.0 + sc) + sh]], []

    cwl = min(LN_CHUNK, D)
    ln_chunks = range(0, D, cwl)

    def f_convln(load, vec_, store, add_sum, i_):
        mu, rstd = _ln_stats(lambda c0: load(0, c0, cwl), D, cwl)
        for c0 in ln_chunks:
            nrm = (load(0, c0, cwl) - mu) * rstd * vec_(0, c0, cwl) + vec_(1, c0, cwl)
            store(0, 0, c0, nrm * _sig(nrm))

    def f_merge(rows, vecs, c_, i_):
        g_a, g_b, y_a, y_b = [r.astype(F32) for r in rows]
        return [[_sig(g_a) * y_a + _sig(g_b) * y_b]], []

    def f_ln(load, vec_, store, add_sum, i_):
        def t_of(c0):
            return alpha * load(0, c0, cwl) + (1.0 + vec_(0, c0, cwl)) * load(1, c0, cwl)
        mu, rstd = _ln_stats(t_of, D, cwl)
        for c0 in ln_chunks:
            y = (t_of(c0) - mu) * rstd * vec_(1, c0, cwl) + vec_(2, c0, cwl)
            store(0, 0, c0, y)
            store(1, 0, c0, y * (1.0 + vec_(3, c0, cwl)) + vec_(4, c0, cwl))

    saved = []
    xl = x0
    h = rowwise("modulate", f_mod, S, [_ri(x0)], [mods[0][1], mods[0][0]], [(BF16, 1)], W=D)[0]
    ag_first, _ = gather_forward("ag0a", first_ops, ag_first, h)
    W_in[0], W_dw[0] = gather_end("ag0a", ag_first, h)
    ag_next, next_ops = None, None

    def after_token(token):
        return [token] if token is not None else []

    def rope_heads(tile, extras, j):
        tc_, tl, th = extras
        heads = []
        for hh in range(tile.shape[1] // HEAD_DIM):
            xh = tile[:, hh * HEAD_DIM:(hh + 1) * HEAD_DIM]
            is_qk = j * (tile.shape[1] // HEAD_DIM) + hh < nq + nkv
            heads.append(jnp.where(is_qk, _rope(xh, tc_, tl, th, 1.0), xh))
        return jnp.concatenate(heads, axis=1)

    for l in range(L):
        sh_a, sc_a, gt_a, sh_f, sc_f, gt_f = mods[l]
        if l == 0:
            rest_ops = gather_rest(0)
            ag_rest, token = gather_begin("ag0b", rest_ops, W_in[0])
        elif l + 1 < L:
            next_ops = gather_first(l + 1) + gather_rest(l + 1)
            ag_next, token = gather_begin("ag%d" % (l + 1), next_ops, W_in[l])
        else:
            token = None
        qkv = matmul("in_qkv", h, W_in[l], "nn", BF16, n=Dqkv, deps=after_token(token),
                     epilogue=rope_heads, row_extras=[t_cos, t_lo, t_hi])
        p_glu = matmul("in_glu", h, W_in[l], "nn", F32, b_off=Dqkv, n=2 * D)
        p_gate = matmul("in_gate", h, W_in[l], "nn", BF16, b_off=Dqkv + 2 * D, n=2 * D)
        sinkcol = jnp.repeat(sink[l].reshape(nkv, GQA_GROUP), BLOCK, axis=1).reshape(nkv, GQA_GROUP * BLOCK, 1)
        att = attn_fwd(qkv, sinkcol, S, D, Dkv)
        uc = conv_fwd(p_glu, W_dw[l], S, D)
        token = None
        if l == 0:
            ag_rest, _ = gather_forward("ag0b", rest_ops, ag_rest, uc)
            W_gu[0], W_oa[0], W_ob[0], W_out[0], W_down[0] = gather_end("ag0b", ag_rest, uc)
            if L > 1:
                next_ops = gather_first(1) + gather_rest(1)
                ag_next, token = gather_begin("ag1", next_ops, W_gu[0])
        y_a = matmul("oa", att, W_oa[l], "nn", BF16, deps=after_token(token))
        z = rowwise("conv_ln", f_convln, S, [_ri(uc)], [vec(conv_ln_g, l), vec(conv_ln_b, l)], [(BF16, 1)], W=D,
                    lazy=True, rc=LN_ROWS)[0]
        y_b = matmul("ob", z, W_ob[l], "nn", BF16)
        merged = rowwise("merge", f_merge, S, [_ri(p_gate, col=0), _ri(p_gate, col=1), _ri(y_a), _ri(y_b)], [],
                         [(BF16, 1)], W=D, cw=_pick(D, 512))[0]
        r1 = matmul("out", merged, W_out[l], "nn", BF16)
        x1, h2 = rowwise("ln1", f_ln, S, [_ri(xl), _ri(r1)], [gt_a, vec(ln1_g, l), vec(ln1_b, l), sc_f, sh_f],
                         [(F32, 1), (BF16, 1)], W=D, lazy=True, rc=LN_ROWS)
        act, gate, up = matmul_swiglu(h2, W_gu[l], DFF)
        token = None
        if l + 1 < L:
            ag_next, token = gather_forward("ag%d" % (l + 1), next_ops, ag_next, act)
        f = matmul("down", act, W_down[l], "nn", BF16, deps=after_token(token))
        nsc, nsh = (mods[l + 1][1], mods[l + 1][0]) if l + 1 < L else (sc_a, sh_a)
        x2, h_next = rowwise("ln2", f_ln, S, [_ri(x1), _ri(f)], [gt_f, vec(ln2_g, l), vec(ln2_b, l), nsc, nsh],
                             [(F32, 1), (BF16, 1)], W=D, lazy=True, rc=LN_ROWS)
        saved.append(dict(x=xl, h=h, qkv=qkv, sinkcol=sinkcol, att=att, p_glu=p_glu, p_gate=p_gate, y_a=y_a, y_b=y_b,
                          uc=uc, z=z, merged=merged, r1=r1, x1=x1, h2=h2, gate=gate, up=up, act=act, f=f))
        xl, h = x2, h_next
        if l + 1 < L:
            (W_in[l + 1], W_dw[l + 1], W_gu[l + 1], W_oa[l + 1], W_ob[l + 1], W_out[l + 1],
             W_down[l + 1]) = gather_end("ag%d" % (l + 1), ag_next, x2)

    def f_loss(rows, vecs, c_, i_):
        y, t = rows
        e = y - t
        return [[e * (1.0 / D)]], [e * e]

    dy, err = rowwise("loss", f_loss, S, [_ri(xl), _ri(target)], [], [(F32, 1)], 1, W=D)
    loss = lax.psum(0.5 * jnp.sum(err) / D, AXES)

    def ln_bwd_passes(t_of, dout_of, g_of, r_of, gt_of, store, add_sum):
        mu, rstd = _ln_stats(t_of, D, cwl)
        m1 = _row_mean(lambda c0: dout_of(c0) * g_of(c0), D, cwl)
        m2 = _row_mean(lambda c0: dout_of(c0) * g_of(c0) * ((t_of(c0) - mu) * rstd), D, cwl)
        for c0 in ln_chunks:
            dout, xhat = dout_of(c0), (t_of(c0) - mu) * rstd
            dt = rstd * (dout * g_of(c0) - m1 - xhat * m2)
            store(0, 0, c0, (1.0 + gt_of(c0)) * dt)
            store(1, 0, c0, alpha * dt)
            add_sum(0, c0, dout * xhat)
            add_sum(1, c0, dout)
            add_sum(2, c0, dt * r_of(c0))

    def f_ln_bwd_last(load, vec_, store, add_sum, i_):
        ln_bwd_passes(lambda c0: alpha * load(1, c0, cwl) + (1.0 + vec_(0, c0, cwl)) * load(2, c0, cwl),
                      lambda c0: load(0, c0, cwl), lambda c0: vec_(1, c0, cwl), lambda c0: load(2, c0, cwl),
                      lambda c0: vec_(0, c0, cwl), store, add_sum)

    def f_ln_bwd(load, vec_, store, add_sum, i_):
        ln_bwd_passes(lambda c0: alpha * load(3, c0, cwl) + (1.0 + vec_(1, c0, cwl)) * load(4, c0, cwl),
                      lambda c0: load(0, c0, cwl) + load(1, c0, cwl).astype(F32) * (1.0 + vec_(0, c0, cwl)),
                      lambda c0: vec_(2, c0, cwl), lambda c0: load(4, c0, cwl), lambda c0: vec_(1, c0, cwl),
                      store, add_sum)
        for c0 in ln_chunks:
            dh = load(1, c0, cwl).astype(F32)
            add_sum(3, c0, dh * load(2, c0, cwl))
            add_sum(4, c0, dh)

    def f_swiglu_bwd(rows, vecs, c_, i_):
        da, gate, up = rows[0].astype(F32), rows[1].astype(F32), rows[2].astype(F32)
        sg = _sig(gate)
        return [[da * up * sg * (1.0 + gate * (1.0 - sg)), da * (gate * sg)]], []

    def f_gate_bwd(rows, vecs, c_, i_):
        dm, g_a, g_b, y_a, y_b = [r.astype(F32) for r in rows]
        sa, sb = _sig(g_a), _sig(g_b)
        return [[dm * sa], [dm * sb], [dm * y_a * sa * (1.0 - sa), dm * y_b * sb * (1.0 - sb)]], []

    def f_convln_bwd(load, vec_, store, add_sum, i_):
        mu, rstd = _ln_stats(lambda c0: load(1, c0, cwl), D, cwl)

        def parts(c0):
            xhat = (load(1, c0, cwl) - mu) * rstd
            nrm = xhat * vec_(0, c0, cwl) + vec_(1, c0, cwl)
            sg = _sig(nrm)
            return load(0, c0, cwl).astype(F32) * sg * (1.0 + nrm * (1.0 - sg)), xhat
        m1 = _row_mean(lambda c0: parts(c0)[0] * vec_(0, c0, cwl), D, cwl)

        def dyg_xhat(c0):
            dn, xhat = parts(c0)
            return dn * vec_(0, c0, cwl) * xhat
        m2 = _row_mean(dyg_xhat, D, cwl)
        for c0 in ln_chunks:
            dn, xhat = parts(c0)
            store(0, 0, c0, rstd * (dn * vec_(0, c0, cwl) - m1 - xhat * m2))
            add_sum(0, c0, dn * xhat)
            add_sum(1, c0, dn)

    def f_kv_combine(rows, vecs, c_, i_):
        k_lo, k_mid, k_hi, v_lo, v_mid, v_hi, tc_, tl, th = rows
        lo, hi = i_ > 0, i_ < nb - 1
        dk = jnp.where(lo, k_lo, 0.0) + k_mid + jnp.where(hi, k_hi, 0.0)
        dv = jnp.where(lo, v_lo, 0.0) + v_mid + jnp.where(hi, v_hi, 0.0)
        return [[_rope(dk, tc_, tl, th, -1.0)], [dv]], []

    def f_mod_bwd(rows, vecs, c_, i_):
        (dres, dh, xv), (sc,) = rows, vecs
        dh = dh.astype(F32)
        return [[dres + dh * (1.0 + sc)]], [dh * xv, dh]

    def flat(a):
        return a.reshape(-1, a.shape[-1])

    big_names = ("w_in", "w_gu", "w_oa", "w_ob", "w_out", "w_down")
    big_w = dict(w_in=w_in, w_gu=w_gu, w_oa=w_oa, w_ob=w_ob, w_out=w_out, w_down=w_down)
    big_m = dict(w_in=m_w_in, w_gu=m_w_gu, w_oa=m_w_oa, w_ob=m_w_ob, w_out=m_w_out, w_down=m_w_down)
    big_v = dict(w_in=v_w_in, w_gu=v_w_gu, w_oa=v_w_oa, w_ob=v_w_ob, w_out=v_w_out, w_down=v_w_down)
    big_res = {nm: None for nm in big_names}
    dmod = [None] * L
    small = dict(sink=[None] * L, conv_ln_g=[None] * L, conv_ln_b=[None] * L, ln1_g=[None] * L, ln1_b=[None] * L,
                 ln2_g=[None] * L, ln2_b=[None] * L)
    dwdw = [None] * L

    first, second, last = ("w_down", "w_gu"), ("w_out", "w_oa", "w_ob"), ("w_in",)

    deferred = []

    def run_adamw(nm, l, land, deps=()):
        big_res[nm] = adamw_update("adamw_" + nm, big_w[nm], big_m[nm], big_v[nm], layer=l, landed=land,
                                   into=big_res[nm], deps=deps)

    def finish(names, tag, handle, after, l, defer=True):
        landed = exchange_wait(tag + "_wait", handle, after)[len(names):]
        for nm, land in zip(names, landed):
            if defer:
                deferred.append((nm, l, land))
            else:
                run_adamw(nm, l, land)
        return landed[0]

    dres, dh_next = dy, None
    rs_last, token = None, None
    for l in reversed(range(L)):
        sv = saved[l]
        sh_a, sc_a, gt_a, sh_f, sc_f, gt_f = mods[l]
        if dh_next is None:
            df, dres, d_g2, d_b2, d_gtf = rowwise(
                "ln2_bwd_last", f_ln_bwd_last, S, [_ri(dres), _ri(sv["x1"]), _ri(sv["f"])], [gt_f, vec(ln2_g, l)],
                [(BF16, 1), (F32, 1)], 3, W=D, lazy=True, rc=LN_ROWS)
            d_sca_next = d_sha_next = None
        else:
            df, dres, d_g2, d_b2, d_gtf, d_sca_next, d_sha_next = rowwise(
                "ln2_bwd", f_ln_bwd, S, [_ri(dres), _ri(dh_next), _ri(saved[l + 1]["x"]), _ri(sv["x1"]), _ri(sv["f"])],
                [mods[l + 1][1], gt_f, vec(ln2_g, l)], [(BF16, 1), (F32, 1)], 5, W=D, lazy=True, rc=LN_ROWS)
            dmod[l + 1][1], dmod[l + 1][0] = d_sca_next, d_sha_next
        dmod[l] = [None] * N_MOD
        dmod[l][5] = d_gtf
        small["ln2_g"][l], small["ln2_b"][l] = d_g2, d_b2
        g_down = matmul("d_w_down", sv["act"], df, "tn", BF16, deps=after_token(token))
        dact = matmul("d_act", df, W_down[l], "nt", BF16)
        dgu = rowwise("swiglu_bwd", f_swiglu_bwd, S, [_ri(dact), _ri(sv["gate"]), _ri(sv["up"])], [],
                      [(BF16, 2)], W=DFF, tr=128, cw=_pick(DFF, 512))[0]
        g_gu = matmul("d_w_gu", sv["h2"], dgu, "tn", BF16)
        rs_first, token = begin("rsa%d" % l, [op_scatter_axis(g_down, 0), op_scatter_axis(g_gu, 1)], g_gu)
        dh2 = matmul("d_h2", dgu, W_gu[l], "nt", BF16, deps=[token])
        dr1, dres, d_g1, d_b1, d_gta, d_scf, d_shf = rowwise(
            "ln1_bwd", f_ln_bwd, S, [_ri(dres), _ri(dh2), _ri(sv["x1"]), _ri(sv["x"]), _ri(sv["r1"])],
            [sc_f, gt_a, vec(ln1_g, l)], [(BF16, 1), (F32, 1)], 5, W=D, lazy=True, rc=LN_ROWS)
        dmod[l][2], dmod[l][4], dmod[l][3] = d_gta, d_scf, d_shf
        small["ln1_g"][l], small["ln1_b"][l] = d_g1, d_b1
        g_out = matmul("d_w_out", sv["merged"], dr1, "tn", BF16)
        dmerged = matmul("d_merged", dr1, W_out[l], "nt", BF16)
        if rs_last is not None:
            finish(last, "rsc%d" % (l + 1), rs_last, dmerged, l + 1)
        dy_a, dy_b, dp_gate = rowwise(
            "gate_bwd", f_gate_bwd, S,
            [_ri(dmerged), _ri(sv["p_gate"], col=0), _ri(sv["p_gate"], col=1), _ri(sv["y_a"]), _ri(sv["y_b"])], [],
            [(BF16, 1), (BF16, 1), (BF16, 2)], W=D, cw=_pick(D, 512))
        g_oa = matmul("d_w_oa", sv["att"], dy_a, "tn", BF16)
        datt = matmul("d_att", dy_a, W_oa[l], "nt", BF16)
        g_ob = matmul("d_w_ob", sv["z"], dy_b, "tn", BF16)
        dz = matmul("d_z", dy_b, W_ob[l], "nt", BF16)
        duc, d_cg, d_cb = rowwise("conv_ln_bwd", f_convln_bwd, S, [_ri(dz), _ri(sv["uc"])],
                                  [vec(conv_ln_g, l), vec(conv_ln_b, l)], [(F32, 1)], 2, W=D, lazy=True, rc=LN_ROWS)
        small["conv_ln_g"][l], small["conv_ln_b"][l] = d_cg, d_cb
        dga, dgb, dwdw[l] = conv_bwd(duc, sv["p_glu"], W_dw[l], S, D)
        before = finish(first, "rsa%d" % l, rs_first, dga, l)
        rs_second, token = begin("rsb%d" % l, [op_scatter_axis(g_out, 0), op_scatter_axis(g_oa, 0),
                                               op_scatter_axis(g_ob, 0)], before)
        dq_r, dkp, dvp, dsink = attn_bwd(sv["qkv"], sv["sinkcol"], datt, (t_cos, t_lo, t_hi), S, D, Dkv)
        small["sink"][l] = dsink[:, :GQA_GROUP, 0].reshape(1, nq)
        dk_r, dv_r = rowwise(
            "kv_combine", f_kv_combine, S,
            [_ri(dkp, lead=2, shift=-1), _ri(dkp, lead=1), _ri(dkp, lead=0, shift=1),
             _ri(dvp, lead=2, shift=-1), _ri(dvp, lead=1), _ri(dvp, lead=0, shift=1),
             _ri(t_cos, whole=True), _ri(t_lo, whole=True), _ri(t_hi, whole=True)],
            [], [(BF16, 1), (BF16, 1)], W=Dkv, tr=BLOCK, cw=HEAD_DIM, rc=32)
        dp = jnp.concatenate([dq_r, dk_r, dv_r, dga, dgb, dp_gate], axis=1)
        g_in = matmul("d_w_in", sv["h"], dp, "tn", BF16, deps=[token])
        rs_last, token = begin("rsc%d" % l, [op_scatter_axis(g_in, 1)], g_in)
        dh_next = matmul("d_h", dp, W_in[l], "nt", BF16, deps=[token])
        finish(second, "rsb%d" % l, rs_second, dh_next, l)
        token = None

    grad_x, d_sca0, d_sha0 = rowwise("mod_bwd", f_mod_bwd, S, [_ri(dres), _ri(dh_next), _ri(x0)], [mods[0][1]],
                                     [(F32, 1)], 2, W=D)
    dmod[0][1], dmod[0][0] = d_sca0, d_sha0
    for nm, l, land in deferred:
        run_adamw(nm, l, land, deps=[grad_x])
    finish(last, "rsc0", rs_last, big_res["w_gu"][0], 0, defer=False)

    small_names = ("b_ada", "sink", "conv_ln_g", "conv_ln_b", "ln1_g", "ln1_b", "ln2_g", "ln2_b")
    small_w = dict(b_ada=b_ada, sink=sink, conv_ln_g=conv_ln_g, conv_ln_b=conv_ln_b, ln1_g=ln1_g, ln1_b=ln1_b,
                   ln2_g=ln2_g, ln2_b=ln2_b)
    small_m = dict(b_ada=m_b_ada, sink=m_sink, conv_ln_g=m_conv_ln_g, conv_ln_b=m_conv_ln_b, ln1_g=m_ln1_g,
                   ln1_b=m_ln1_b, ln2_g=m_ln2_g, ln2_b=m_ln2_b)
    small_v = dict(b_ada=v_b_ada, sink=v_sink, conv_ln_g=v_conv_ln_g, conv_ln_b=v_conv_ln_b, ln1_g=v_ln1_g,
                   ln1_b=v_ln1_b, ln2_g=v_ln2_g, ln2_b=v_ln2_b)
    small_g = dict(small)
    small_g["b_ada"] = [jnp.concatenate(dmod[l], axis=1) for l in range(L)]
    sizes = [small_w[nm].size for nm in small_names]
    total = sum(sizes)
    padded = -(-total // (SMALL_W * ROW_CHUNK)) * (SMALL_W * ROW_CHUNK)

    def pack(parts):
        flat_ = jnp.concatenate([p.reshape(-1) for p in parts] + [jnp.zeros((padded - total,), F32)])
        return flat_.reshape(padded // SMALL_W, SMALL_W)

    g_pack = pack([jnp.concatenate(small_g[nm], axis=0) for nm in small_names])
    small_land, dwdw_land = exchange("gather_small", [op_gather_stack(g_pack),
                                                      op_scatter_axis(jnp.stack(dwdw, axis=0), 2)],
                                     deps=[big_res["w_in"][0]])
    small_out = adamw_update("adamw_small", pack([small_w[nm] for nm in small_names]),
                             pack([small_m[nm] for nm in small_names]), pack([small_v[nm] for nm in small_names]),
                             landed=small_land)

    def unpack(buf):
        flat_, out, o = buf.reshape(-1), {}, 0
        for nm, sz in zip(small_names, sizes):
            out[nm] = flat_[o:o + sz].reshape(small_w[nm].shape)
            o += sz
        return out
    small_res = [unpack(b) for b in small_out]

    dw_cols = w_dw.shape[2]

    def pad_dw(a):
        return jnp.pad(a, ((0, 0), (0, 32 - CONV_WIDTH), (0, 0))).reshape(L * 32, dw_cols)
    dw_out = adamw_update("adamw_w_dw", pad_dw(w_dw), pad_dw(m_w_dw), pad_dw(v_w_dw),
                          landed=dwdw_land.reshape(N_DEV, L * 32, dw_cols))
    dw_res = [a.reshape(L, 32, dw_cols)[:, :CONV_WIDTH] for a in dw_out]

    dmod_all = small_land.reshape(N_DEV, -1)[:, :L * N_MOD * D].reshape(N_DEV, L, N_MOD * D)
    dmod_mine = lax.dynamic_slice_in_dim(dmod_all, me * ada_cols, ada_cols, axis=2)
    dmod_pad = jnp.concatenate([dmod_mine, jnp.zeros_like(dmod_mine)], axis=0).astype(BF16)
    g_ada = jnp.stack([matmul("d_w_ada", c_pad, dmod_pad[:, l], "tn", F32) for l in range(L)], axis=0)
    ada_out = adamw_update("adamw_w_ada", flat(w_ada), flat(m_w_ada), flat(v_w_ada), grad=flat(g_ada))
    ada_res = [a.reshape(w_ada.shape) for a in ada_out]

    order = ("w_ada", "b_ada", "w_in", "sink", "w_dw", "conv_ln_g", "conv_ln_b", "w_oa", "w_ob", "w_out",
             "ln1_g", "ln1_b", "w_gu", "w_down", "ln2_g", "ln2_b")

    def result(nm, j):
        if nm == "w_ada":
            return ada_res[j]
        if nm == "w_dw":
            return dw_res[j]
        if nm in big_res:
            return big_res[nm][j]
        return small_res[j][nm]

    outs = [loss, grad_x.reshape(x.shape)]
    for j in range(4):
        outs += [result(nm, j) for nm in order]
    return tuple(outs)
```

```python
import math

import jax
import jax.numpy as jnp
from jax import lax
from jax.experimental import pallas as pl
from jax.experimental.pallas import tpu as pltpu

F32 = jnp.float32
BF16 = jnp.bfloat16
N_DEV = 8
AXES = ("x", "y", "c")
HEAD_DIM = 128
GQA_GROUP = 4
BLOCK = 128
ROPE_DIM = HEAD_DIM // 4
ROPE_HALF = ROPE_DIM // 2
ROPE_THETA = 500000.0
CONV_WIDTH = 31
CONV_HALO = 16
N_MOD = 6
LN_EPS = 1e-5
NEG_INF = -1e30
ADAM_LR, ADAM_B1, ADAM_B2, ADAM_EPS, ADAM_WD, ADAM_STEP = 0.001, 0.9, 0.999, 1e-08, 0.01, 10
VMEM_LIMIT = 56 * 1024 * 1024
LANE = 128
ROW_CHUNK = 16
SMALL_W = 512


def _params(*sem):
    return pltpu.CompilerParams(dimension_semantics=sem, vmem_limit_bytes=VMEM_LIMIT)


def _pick(dim, pref, mult=LANE):
    if dim <= pref:
        return dim
    best = None
    for t in range(mult, pref + 1, mult):
        if dim % t == 0:
            best = t
    assert best is not None, (dim, pref)
    return best


def matmul(name, a, b, mode, out_dtype, *, b_off=0, n=None, tm=1024, tn=1408, tk=2816, deps=(), epilogue=None,
           row_extras=()):
    if mode == "nn":
        (M, K), N = a.shape, (n or b.shape[1])
    elif mode == "tn":
        (K, M), N = a.shape, b.shape[1]
    else:
        (M, K), N = a.shape, b.shape[0]
    if mode == "tn":
        tn, tk = min(tn, 1024), max(tk, 4096)
    tm = _pick(M, tn, LANE) if mode == "tn" else _pick(M, tm, 16)
    tn, tk = _pick(math.gcd(N, b_off) if b_off else N, tn), _pick(K, tk)
    assert b_off % tn == 0 and N % tn == 0
    boff = b_off // tn
    nk = K // tk
    if mode == "nn":
        a_spec = pl.BlockSpec((tm, tk), lambda i, j, k: (i, k))
        b_spec = pl.BlockSpec((tk, tn), lambda i, j, k: (k, j + boff))
        dims = (((1,), (0,)), ((), ()))
    elif mode == "tn":
        a_spec = pl.BlockSpec((tk, tm), lambda i, j, k: (k, i))
        b_spec = pl.BlockSpec((tk, tn), lambda i, j, k: (k, j))
        dims = (((0,), (0,)), ((), ()))
    else:
        a_spec = pl.BlockSpec((tm, tk), lambda i, j, k: (i, k))
        b_spec = pl.BlockSpec((tn, tk), lambda i, j, k: (j, k))
        dims = (((1,), (1,)), ((), ()))
    n_ex, n_dep = len(row_extras), len(deps)
    assert epilogue is None or nk == 1

    def body(a_ref, b_ref, *rest):
        o_ref = rest[n_ex + n_dep]

        def dot():
            return lax.dot_general(a_ref[...].astype(BF16), b_ref[...].astype(BF16), dims, preferred_element_type=F32)
        if epilogue is not None:
            o_ref[...] = epilogue(dot(), [r[...] for r in rest[:n_ex]], pl.program_id(1)).astype(out_dtype)
        elif nk == 1:
            o_ref[...] = dot().astype(out_dtype)
        else:
            acc_ref = rest[n_ex + n_dep + 1]
            k = pl.program_id(2)

            @pl.when(k == 0)
            def _():
                acc_ref[...] = jnp.zeros_like(acc_ref)
            acc_ref[...] += dot()

            @pl.when(k == nk - 1)
            def _():
                o_ref[...] = acc_ref[...].astype(out_dtype)

    return pl.pallas_call(
        body, name=name, grid=(M // tm, N // tn, nk),
        in_specs=[a_spec, b_spec] + [pl.BlockSpec((tm, e.shape[1]), lambda i, j, k: (i, 0)) for e in row_extras]
        + [pl.BlockSpec(memory_space=pl.ANY)] * n_dep,
        out_specs=pl.BlockSpec((tm, tn), lambda i, j, k: (i, j)),
        out_shape=jax.ShapeDtypeStruct((M, N), out_dtype),
        scratch_shapes=[pltpu.VMEM((tm, tn), F32)] if nk > 1 else [],
        compiler_params=_params("parallel", "parallel", "arbitrary"),
    )(a, b, *row_extras, *deps)


def matmul_swiglu(h, w_gu, dff, *, tm=1024, tn=512):
    M, K = h.shape
    tm, tn = _pick(M, tm, 16), _pick(dff, tn)
    nj = dff // tn

    def body(a_ref, bg_ref, bu_ref, act_ref, gate_ref, up_ref):
        a = a_ref[...]
        gate = jnp.dot(a, bg_ref[...], preferred_element_type=F32)
        up = jnp.dot(a, bu_ref[...], preferred_element_type=F32)
        act_ref[...] = (gate * _sig(gate) * up).astype(BF16)
        gate_ref[...] = gate.astype(BF16)
        up_ref[...] = up.astype(BF16)

    out_spec = pl.BlockSpec((tm, tn), lambda i, j: (i, j))
    return pl.pallas_call(
        body, name="gu_swiglu", grid=(M // tm, nj),
        in_specs=[pl.BlockSpec((tm, K), lambda i, j: (i, 0)), pl.BlockSpec((K, tn), lambda i, j: (0, j)),
                  pl.BlockSpec((K, tn), lambda i, j: (0, nj + j))],
        out_specs=[out_spec] * 3,
        out_shape=[jax.ShapeDtypeStruct((M, dff), BF16)] * 3,
        compiler_params=_params("parallel", "parallel"),
    )(h, w_gu, w_gu)


def rowwise(name, fn, nrows, row_ins, vec_ins, row_outs, n_sums=0, *, W, tr=256, cw=None, rc=ROW_CHUNK,
            out_layer=None, into=None, lazy=False, deps=()):
    cw = cw or W
    tr = _pick(nrows, tr, ROW_CHUNK)
    rc = min(rc, tr)
    assert W % cw == 0 and nrows % ROW_CHUNK == 0 and tr % rc == 0 and rc % 8 == 0
    nrt = nrows // tr
    n_ri, n_v, n_ro = len(row_ins), len(vec_ins), len(row_outs)

    def row_spec(lead, colblk, shift, whole):
        w = cw if whole else W

        def rmap(i):
            return jnp.clip(i + shift, 0, nrt - 1) if shift else i
        if lead is None:
            return pl.BlockSpec((tr, w), lambda i: (rmap(i), colblk))
        return pl.BlockSpec((None, tr, w), lambda i: (lead, rmap(i), colblk))

    n_prev = len(deps) + (len(into) if into is not None else 0)
    in_specs = [row_spec(*ri[1:]) for ri in row_ins]
    in_specs += [pl.BlockSpec((1, W), lambda i: (0, 0)) for _ in vec_ins]
    in_specs += [pl.BlockSpec(memory_space=pl.ANY)] * n_prev
    if out_layer is None:
        out_specs = [pl.BlockSpec((tr, p * W), lambda i: (i, 0)) for _, p in row_outs]
        out_shape = [jax.ShapeDtypeStruct((nrows, p * W), dt) for dt, p in row_outs]
    else:
        out_specs = [pl.BlockSpec((None, tr, p * W), lambda i: (out_layer[0], i, 0)) for _, p in row_outs]
        out_shape = [jax.ShapeDtypeStruct((out_layer[1], nrows, p * W), dt) for dt, p in row_outs]
    out_specs += [pl.BlockSpec((1, W), lambda i: (0, 0)) for _ in range(n_sums)]
    out_shape += [jax.ShapeDtypeStruct((1, W), F32) for _ in range(n_sums)]

    def body(*refs):
        rin, vin = refs[:n_ri], refs[n_ri:n_ri + n_v]
        refs = refs[n_prev:]
        rout = refs[n_ri + n_v:n_ri + n_v + n_ro]
        sout = refs[n_ri + n_v + n_ro:n_ri + n_v + n_ro + n_sums]
        acc = refs[n_ri + n_v + n_ro + n_sums:]
        i = pl.program_id(0)
        if n_sums:
            @pl.when(i == 0)
            def _():
                for a in acc:
                    a[...] = jnp.zeros_like(a)
        def fold8(s):
            part = s[0:8]
            for q in range(1, rc // 8):
                part = part + s[8 * q:8 * q + 8]
            return part

        def lazy_step(r, carry):
            r0 = pl.multiple_of(r * rc, rc)

            def load(idx, c0, w):
                return rin[idx][pl.ds(r0, rc), c0:c0 + w]

            def vec(idx, c0, w):
                return vin[idx][:, c0:c0 + w]

            def store(o, piece, c0, val):
                rout[o][pl.ds(r0, rc), piece * W + c0:piece * W + c0 + val.shape[1]] = val.astype(row_outs[o][0])

            def add_sum(k, c0, val):
                acc[k][:, c0:c0 + val.shape[1]] += fold8(val)
            fn(load, vec, store, add_sum, i)
            return carry
        if lazy:
            lax.fori_loop(0, tr // rc, lazy_step, 0, unroll=2)
        for c in range(0 if lazy else W // cw):
            c0 = c * cw
            vecs = [v[:, c0:c0 + cw] for v in vin]

            def step(r, carry, c=c, c0=c0, vecs=vecs):
                r0 = pl.multiple_of(r * rc, rc)
                rows = [ref[pl.ds(r0, rc), :] if ri[4] else ref[pl.ds(r0, rc), c0:c0 + cw]
                        for ref, ri in zip(rin, row_ins)]
                outs, sums = fn(rows, vecs, c, i)
                for oref, (dt, _), pieces in zip(rout, row_outs, outs):
                    for pi, piece in enumerate(pieces):
                        oref[pl.ds(r0, rc), pi * W + c0:pi * W + c0 + cw] = piece.astype(dt)
                for a, s in zip(acc, sums):
                    a[:, c0:c0 + cw] += fold8(s)
                return carry
            lax.fori_loop(0, tr // rc, step, 0)
        if n_sums:
            @pl.when(i == nrt - 1)
            def _():
                for o, a in zip(sout, acc):
                    o[...] = jnp.sum(a[...], axis=0, keepdims=True)

    res = pl.pallas_call(
        body, name=name, grid=(nrt,), in_specs=in_specs, out_specs=out_specs, out_shape=out_shape,
        scratch_shapes=[pltpu.VMEM((8, W), F32) for _ in range(n_sums)],
        input_output_aliases={n_ri + n_v + len(deps) + q: q for q in range(n_prev - len(deps))},
        compiler_params=_params("arbitrary"),
    )(*[ri[0] for ri in row_ins], *vec_ins, *deps, *(into or ()))
    return res


def _ri(arr, lead=None, col=0, shift=0, whole=False):
    return (arr, lead, col, shift, whole)


def _sig(x):
    return jax.nn.sigmoid(x)


LN_CHUNK = 256
LN_ROWS = 64


def _row_mean(make, width, cw):
    acc = None
    for c0 in range(0, width, cw):
        t = make(c0)
        for q in range(cw // LANE):
            part = t[:, q * LANE:(q + 1) * LANE]
            acc = part if acc is None else acc + part
    return jnp.sum(acc, axis=-1, keepdims=True) * (1.0 / width)


def _ln_stats(t_of, width, cw):
    mu = _row_mean(t_of, width, cw)

    def sq(c0):
        d = t_of(c0) - mu
        return d * d
    return mu, lax.rsqrt(_row_mean(sq, width, cw) + LN_EPS)


def _rope(x, cos, s_lo, s_hi, sign):
    up = pltpu.roll(x, HEAD_DIM - ROPE_HALF, 1)
    down = pltpu.roll(x, ROPE_HALF, 1)
    return x * cos + sign * (up * s_lo + down * s_hi)


def rope_tables(S):
    pos = jnp.arange(S, dtype=F32)
    inv_freq = ROPE_THETA ** (-jnp.arange(0, ROPE_DIM, 2, dtype=F32) / ROPE_DIM)
    ang = pos[:, None] * inv_freq[None, :]
    cos, sin = jnp.cos(ang), jnp.sin(ang)
    ones = jnp.ones((S, HEAD_DIM - ROPE_DIM), F32)
    zeros = jnp.zeros((S, HEAD_DIM - ROPE_DIM), F32)
    zh = jnp.zeros((S, ROPE_HALF), F32)
    t_cos = jnp.concatenate([cos, cos, ones], axis=1)
    t_lo = jnp.concatenate([-sin, zh, zeros], axis=1)
    t_hi = jnp.concatenate([zh, sin, zeros], axis=1)
    return t_cos, t_lo, t_hi


ATTN_HEADS_PER_STEP = 4


def _attn_specs(S, D, Dkv):
    nb, nkv, qb = S // BLOCK, Dkv // HEAD_DIM, D // HEAD_DIM
    hps = math.gcd(ATTN_HEADS_PER_STEP, nkv)
    assert qb % hps == 0 and (qb + nkv) % hps == 0
    gw = GQA_GROUP * HEAD_DIM
    q_spec = pl.BlockSpec((BLOCK, hps * gw), lambda h, n: (n, h))

    def band(col0):
        c0 = col0 // hps
        return [pl.BlockSpec((BLOCK, hps * HEAD_DIM), lambda h, n: (jnp.maximum(n - 1, 0), c0 + h)),
                pl.BlockSpec((BLOCK, hps * HEAD_DIM), lambda h, n: (n, c0 + h)),
                pl.BlockSpec((BLOCK, hps * HEAD_DIM), lambda h, n: (jnp.minimum(n + 1, nb - 1), c0 + h))]
    sink_spec = pl.BlockSpec((hps, GQA_GROUP * BLOCK, 1), lambda h, n: (h, 0, 0))
    return nb, nkv, hps, gw, q_spec, band(qb), band(qb + nkv), sink_spec


def _attn_valid(n, S):
    shape = (GQA_GROUP * BLOCK, 3 * BLOCK)
    row = lax.broadcasted_iota(jnp.int32, shape, 0) & (BLOCK - 1)
    col = lax.broadcasted_iota(jnp.int32, shape, 1)
    rel = col - BLOCK - row
    kpos = (n - 1) * BLOCK + col
    return (jnp.abs(rel) <= BLOCK) & (kpos >= 0) & (kpos < S)


def _head(ref, hh, width):
    return ref[:, hh * width:(hh + 1) * width]


def _attn_probs(q, k_blocks, sink, valid):
    qs = jnp.concatenate([q[:, g * HEAD_DIM:(g + 1) * HEAD_DIM] for g in range(GQA_GROUP)], axis=0)
    kb = jnp.concatenate(k_blocks, axis=0)
    s = lax.dot_general(qs, kb, (((1,), (1,)), ((), ())), preferred_element_type=F32) * (HEAD_DIM ** -0.5)
    s = jnp.where(valid, s, NEG_INF)
    m = jnp.maximum(jnp.max(s, axis=-1, keepdims=True), sink)
    p = jnp.exp(s - m)
    e_sink = jnp.exp(sink - m)
    denom = jnp.sum(p, axis=-1, keepdims=True) + e_sink
    return qs, kb, p / denom, e_sink / denom


def attn_fwd(qkv, sinkcol, S, D, Dkv):
    nb, nkv, hps, gw, q_spec, k_specs, v_specs, sink_spec = _attn_specs(S, D, Dkv)

    def body(q_ref, k0, k1, k2, v0, v1, v2, sink_ref, o_ref):
        valid = _attn_valid(pl.program_id(1), S)
        for hh in range(hps):
            _, _, w, _ = _attn_probs(_head(q_ref, hh, gw), [_head(r, hh, HEAD_DIM) for r in (k0, k1, k2)],
                                     sink_ref[hh], valid)
            vb = jnp.concatenate([_head(r, hh, HEAD_DIM) for r in (v0, v1, v2)], axis=0)
            o = jnp.dot(w.astype(BF16), vb, preferred_element_type=F32)
            for g in range(GQA_GROUP):
                c0 = hh * gw + g * HEAD_DIM
                o_ref[:, c0:c0 + HEAD_DIM] = o[g * BLOCK:(g + 1) * BLOCK].astype(BF16)

    return pl.pallas_call(
        body, name="attn_fwd", grid=(nkv // hps, nb),
        in_specs=[q_spec, *k_specs, *v_specs, sink_spec],
        out_specs=pl.BlockSpec((BLOCK, hps * gw), lambda h, n: (n, h)),
        out_shape=jax.ShapeDtypeStruct((S, D), BF16),
        compiler_params=_params("parallel", "arbitrary"),
    )(qkv, qkv, qkv, qkv, qkv, qkv, qkv, sinkcol)


def attn_bwd(qkv, sinkcol, datt, tables, S, D, Dkv):
    nb, nkv, hps, gw, q_spec, k_specs, v_specs, sink_spec = _attn_specs(S, D, Dkv)

    def body(q_ref, k0, k1, k2, v0, v1, v2, sink_ref, do_ref, tc_ref, tl_ref, th_ref, dq_ref, dkp_ref, dvp_ref,
             dsink_ref):
        n = pl.program_id(1)
        valid = _attn_valid(n, S)
        tc_, tl, th = tc_ref[...], tl_ref[...], th_ref[...]

        @pl.when(n == 0)
        def _():
            dsink_ref[...] = jnp.zeros_like(dsink_ref)
        for hh in range(hps):
            qs, kb, w, w_sink = _attn_probs(_head(q_ref, hh, gw), [_head(r, hh, HEAD_DIM) for r in (k0, k1, k2)],
                                            sink_ref[hh], valid)
            vb = jnp.concatenate([_head(r, hh, HEAD_DIM) for r in (v0, v1, v2)], axis=0)
            do = _head(do_ref, hh, gw)
            dos = jnp.concatenate([do[:, g * HEAD_DIM:(g + 1) * HEAD_DIM] for g in range(GQA_GROUP)], axis=0)
            dv = lax.dot_general(w.astype(BF16), dos, (((0,), (0,)), ((), ())), preferred_element_type=F32)
            dw = lax.dot_general(dos, vb, (((1,), (1,)), ((), ())), preferred_element_type=F32)
            delta = jnp.sum(w * dw, axis=-1, keepdims=True)
            ds = (w * (dw - delta) * (HEAD_DIM ** -0.5)).astype(BF16)
            dq = jnp.dot(ds, kb, preferred_element_type=F32)
            dk = lax.dot_general(ds, qs, (((0,), (0,)), ((), ())), preferred_element_type=F32)
            for g in range(GQA_GROUP):
                c0 = hh * gw + g * HEAD_DIM
                dq_ref[:, c0:c0 + HEAD_DIM] = _rope(dq[g * BLOCK:(g + 1) * BLOCK], tc_, tl, th, -1.0).astype(BF16)
            for j in range(3):
                dkp_ref[j, :, hh * HEAD_DIM:(hh + 1) * HEAD_DIM] = dk[j * BLOCK:(j + 1) * BLOCK]
                dvp_ref[j, :, hh * HEAD_DIM:(hh + 1) * HEAD_DIM] = dv[j * BLOCK:(j + 1) * BLOCK]
            t = w_sink * delta
            for g in range(GQA_GROUP):
                dsink_ref[hh, g:g + 1, :] -= jnp.sum(t[g * BLOCK:(g + 1) * BLOCK], axis=0, keepdims=True)

    part_spec = pl.BlockSpec((3, BLOCK, hps * HEAD_DIM), lambda h, n: (0, n, h))
    return pl.pallas_call(
        body, name="attn_bwd", grid=(nkv // hps, nb),
        in_specs=[q_spec, *k_specs, *v_specs, sink_spec, pl.BlockSpec((BLOCK, hps * gw), lambda h, n: (n, h))]
        + [pl.BlockSpec((BLOCK, HEAD_DIM), lambda h, n: (n, 0))] * 3,
        out_specs=[pl.BlockSpec((BLOCK, hps * gw), lambda h, n: (n, h)), part_spec, part_spec,
                   pl.BlockSpec((hps, 8, LANE), lambda h, n: (h, 0, 0))],
        out_shape=[jax.ShapeDtypeStruct((S, D), BF16), jax.ShapeDtypeStruct((3, S, Dkv), F32),
                   jax.ShapeDtypeStruct((3, S, Dkv), F32), jax.ShapeDtypeStruct((nkv, 8, LANE), F32)],
        compiler_params=_params("parallel", "arbitrary"),
    )(qkv, qkv, qkv, qkv, qkv, qkv, qkv, sinkcol, datt, *tables)


CONV_TC = 128
CONV_ROWS = 64


def _shift_rows(win, b):
    return win if b == 0 else pltpu.roll(win, win.shape[0] - b, 0)


def _conv_fill_u(ga_ref, gb_ref, upad_ref, S):
    tc = upad_ref.shape[1]
    zero = jnp.zeros((CONV_HALO, tc), F32)
    upad_ref[0:CONV_HALO, :] = zero
    upad_ref[S + CONV_HALO:S + 2 * CONV_HALO, :] = zero

    def fill(r, carry):
        r0 = pl.multiple_of(r * CONV_ROWS, CONV_ROWS)
        upad_ref[pl.ds(r0 + CONV_HALO, CONV_ROWS), :] = ga_ref[pl.ds(r0, CONV_ROWS), :] * _sig(gb_ref[pl.ds(r0, CONV_ROWS), :])
        return carry
    lax.fori_loop(0, S // CONV_ROWS, fill, 0)


def conv_fwd(pglu, wdw, S, D):
    tc = min(CONV_TC, D)
    nct = D // tc

    def body(ga_ref, gb_ref, w_ref, o_ref, upad_ref):
        _conv_fill_u(ga_ref, gb_ref, upad_ref, S)
        w = w_ref[...]

        def step(r, carry):
            r0 = pl.multiple_of(r * CONV_ROWS, CONV_ROWS)
            win = upad_ref[pl.ds(r0, CONV_ROWS + 2 * CONV_HALO), :]
            acc = jnp.zeros((CONV_ROWS, tc), F32)
            for b in range(8):
                shifted = _shift_rows(win, b)
                for k in range(CONV_WIDTH):
                    o = k + CONV_HALO - CONV_WIDTH // 2
                    if o % 8 == b:
                        acc = acc + w[k:k + 1, :] * shifted[o - b:o - b + CONV_ROWS, :]
            o_ref[pl.ds(r0, CONV_ROWS), :] = acc
            return carry
        lax.fori_loop(0, S // CONV_ROWS, step, 0)

    return pl.pallas_call(
        body, name="conv_fwd", grid=(nct,),
        in_specs=[pl.BlockSpec((S, tc), lambda j: (0, j)), pl.BlockSpec((S, tc), lambda j: (0, nct + j)),
                  pl.BlockSpec((CONV_WIDTH, tc), lambda j: (0, j))],
        out_specs=pl.BlockSpec((S, tc), lambda j: (0, j)),
        out_shape=jax.ShapeDtypeStruct((S, D), F32),
        scratch_shapes=[pltpu.VMEM((S + 2 * CONV_HALO, tc), F32)],
        compiler_params=_params("parallel"),
    )(pglu, pglu, wdw)


def conv_bwd(duc, pglu, wdw, S, D):
    tc = min(CONV_TC, D)
    nct = D // tc
    half = CONV_WIDTH // 2

    def body(d_ref, ga_ref, gb_ref, w_ref, dga_ref, dgb_ref, dw_ref, upad_ref, dpad_ref, dwacc_ref):
        _conv_fill_u(ga_ref, gb_ref, upad_ref, S)
        zero = jnp.zeros((CONV_HALO, tc), F32)
        dpad_ref[0:CONV_HALO, :] = zero
        dpad_ref[S + CONV_HALO:S + 2 * CONV_HALO, :] = zero

        def fill(r, carry):
            r0 = pl.multiple_of(r * CONV_ROWS, CONV_ROWS)
            dpad_ref[pl.ds(r0 + CONV_HALO, CONV_ROWS), :] = d_ref[pl.ds(r0, CONV_ROWS), :]
            return carry
        lax.fori_loop(0, S // CONV_ROWS, fill, 0)
        dwacc_ref[...] = jnp.zeros_like(dwacc_ref)
        w = w_ref[...]

        def step(r, carry):
            r0 = pl.multiple_of(r * CONV_ROWS, CONV_ROWS)
            uwin = upad_ref[pl.ds(r0, CONV_ROWS + 2 * CONV_HALO), :]
            dwin = dpad_ref[pl.ds(r0, CONV_ROWS + 2 * CONV_HALO), :]
            d = dwin[CONV_HALO:CONV_HALO + CONV_ROWS, :]
            du = jnp.zeros((CONV_ROWS, tc), F32)
            for b in range(8):
                d_shifted, u_shifted = _shift_rows(dwin, b), _shift_rows(uwin, b)
                for k in range(CONV_WIDTH):
                    o = CONV_HALO + half - k
                    if o % 8 == b:
                        du = du + w[k:k + 1, :] * d_shifted[o - b:o - b + CONV_ROWS, :]
                    o = CONV_HALO + k - half
                    if o % 8 == b:
                        prod = d * u_shifted[o - b:o - b + CONV_ROWS, :]
                        part = prod[0:8]
                        for q in range(1, CONV_ROWS // 8):
                            part = part + prod[8 * q:8 * q + 8]
                        dwacc_ref[k] += part
            ga = ga_ref[pl.ds(r0, CONV_ROWS), :]
            sg = _sig(gb_ref[pl.ds(r0, CONV_ROWS), :])
            dga_ref[pl.ds(r0, CONV_ROWS), :] = (du * sg).astype(BF16)
            dgb_ref[pl.ds(r0, CONV_ROWS), :] = (du * ga * sg * (1.0 - sg)).astype(BF16)
            return carry
        lax.fori_loop(0, S // CONV_ROWS, step, 0)
        dw_ref[...] = jnp.sum(dwacc_ref[...], axis=1)

    return pl.pallas_call(
        body, name="conv_bwd", grid=(nct,),
        in_specs=[pl.BlockSpec((S, tc), lambda j: (0, j)), pl.BlockSpec((S, tc), lambda j: (0, j)),
                  pl.BlockSpec((S, tc), lambda j: (0, nct + j)), pl.BlockSpec((CONV_WIDTH, tc), lambda j: (0, j))],
        out_specs=[pl.BlockSpec((S, tc), lambda j: (0, j)), pl.BlockSpec((S, tc), lambda j: (0, j)),
                   pl.BlockSpec((32, tc), lambda j: (0, j))],
        out_shape=[jax.ShapeDtypeStruct((S, D), BF16), jax.ShapeDtypeStruct((S, D), BF16),
                   jax.ShapeDtypeStruct((32, D), F32)],
        scratch_shapes=[pltpu.VMEM((S + 2 * CONV_HALO, tc), F32), pltpu.VMEM((S + 2 * CONV_HALO, tc), F32),
                        pltpu.VMEM((32, 8, tc), F32)],
        compiler_params=_params("parallel"),
    )(duc, pglu, pglu, wdw)


def exchange(name, ops, deps=()):
    n = len(ops)

    def body(*refs):
        xs, outs = refs[:n], refs[n + len(deps):2 * n + len(deps)]
        send, recv, lsem = refs[2 * n + len(deps):]
        mx, my, mc = lax.axis_index("x"), lax.axis_index("y"), lax.axis_index("c")
        me = 4 * mx + 2 * my + mc
        local = [pltpu.make_async_copy(op[3](xs[i], me), op[4](outs[i], me), lsem.at[i]) for i, op in enumerate(ops)]
        for cp in local:
            cp.start()
        copies = []
        for k in range(1, N_DEV):
            px = 1 - mx if k & 4 else mx
            py = 1 - my if k & 2 else my
            pc = 1 - mc if k & 1 else mc
            peer = 4 * px + 2 * py + pc
            for i, op in enumerate(ops):
                cp = pltpu.make_async_remote_copy(
                    src_ref=op[3](xs[i], peer), dst_ref=op[4](outs[i], me),
                    send_sem=send.at[i, k - 1], recv_sem=recv.at[i, k - 1],
                    device_id=(px, py, pc), device_id_type=pl.DeviceIdType.MESH)
                cp.start()
                copies.append(cp)
        for cp in copies:
            cp.wait()
        for cp in local:
            cp.wait()

    any_spec = pl.BlockSpec(memory_space=pl.ANY)
    return pl.pallas_call(
        body, name=name,
        in_specs=[any_spec] * (n + len(deps)), out_specs=[any_spec] * n,
        out_shape=[jax.ShapeDtypeStruct(op[1], op[2]) for op in ops],
        scratch_shapes=[pltpu.SemaphoreType.DMA((n, N_DEV - 1)), pltpu.SemaphoreType.DMA((n, N_DEV - 1)),
                        pltpu.SemaphoreType.DMA((n,))],
        compiler_params=pltpu.CompilerParams(has_side_effects=True),
    )(*[op[0] for op in ops], *deps)


_HBM = pl.BlockSpec(memory_space=pltpu.HBM)
_SEM = pl.BlockSpec(memory_space=pltpu.SEMAPHORE)
_EFFECT = pltpu.SideEffectType.DATAFLOW_SIDE_EFFECTING


def _me_and_peers():
    mx, my, mc = lax.axis_index("x"), lax.axis_index("y"), lax.axis_index("c")
    peers = []
    for k in range(1, N_DEV):
        px = 1 - mx if k & 4 else mx
        py = 1 - my if k & 2 else my
        pc = 1 - mc if k & 1 else mc
        peers.append((k, (px, py, pc), 4 * px + 2 * py + pc))
    return 4 * mx + 2 * my + mc, peers


PLACE_STEPS = 4


def exchange_local(name, ops, me_arr, deps=()):
    n = len(ops)
    in_specs, out_specs = [], []
    for op in ops:
        kind, axis = op[5]
        rows, cols = op[0].shape if kind == "gather" else op[1][1:]
        steps = PLACE_STEPS if rows % (PLACE_STEPS * ROW_CHUNK) == 0 else 1
        tr = rows // steps

        def row(i, steps=steps):
            return i if steps > 1 else 0

        def window(i, me, axis=axis, steps=steps):
            return (me[0] * steps + row(i, steps), 0) if axis == 0 else (row(i, steps), me[0])
        if kind == "gather":
            in_specs.append(pl.BlockSpec((tr, cols), lambda i, me, row=row: (row(i), 0)))
            out_specs.append(pl.BlockSpec((tr, cols), window))
        else:
            in_specs.append(pl.BlockSpec((tr, cols), window))
            out_specs.append(pl.BlockSpec((None, tr, cols), lambda i, me, row=row: (me[0], row(i), 0)))

    def body(me_ref, *refs):
        for i in range(n):
            refs[n + len(deps) + i][...] = refs[i][...]

    return pl.pallas_call(
        body, name=name,
        grid_spec=pltpu.PrefetchScalarGridSpec(
            num_scalar_prefetch=1, grid=(PLACE_STEPS,),
            in_specs=in_specs + [pl.BlockSpec(memory_space=pl.ANY)] * len(deps), out_specs=out_specs),
        out_shape=[jax.ShapeDtypeStruct(op[1], op[2]) for op in ops],
        compiler_params=_params("arbitrary"),
    )(me_arr, *[op[0] for op in ops], *deps)


ALL_PEERS = (1, 2, 3, 4, 5, 6, 7)
SIBLING = 1
SAME_CORE = (2, 4, 6)


def _flipped(pos, k):
    p = (1 - pos[0] if k & 4 else pos[0], 1 - pos[1] if k & 2 else pos[1], 1 - pos[2] if k & 1 else pos[2])
    return p, 4 * p[0] + 2 * p[1] + p[2]


def direct_copies(ops, ks):
    n = len(ops)

    def build(refs, send, recv):
        xs, lands = refs[:n], refs[n:2 * n]
        pos = (lax.axis_index("x"), lax.axis_index("y"), lax.axis_index("c"))
        _, me = _flipped(pos, 0)
        copies = []
        for j, k in enumerate(ks):
            peer_id, peer = _flipped(pos, k)
            for i, op in enumerate(ops):
                s = i * len(ks) + j
                copies.append(pltpu.make_async_remote_copy(
                    src_ref=op[3](xs[i], peer), dst_ref=op[4](lands[i], me), send_sem=send.at[s], recv_sem=recv.at[s],
                    device_id=peer_id, device_id_type=pl.DeviceIdType.MESH))
        return copies
    return n * len(ks), build


def forward_copies(ops, ks):
    n = len(ops)

    def build(refs, send, recv):
        lands = refs[:n]
        pos = (lax.axis_index("x"), lax.axis_index("y"), lax.axis_index("c"))
        sibling_id, _ = _flipped(pos, SIBLING)
        copies = []
        for j, k in enumerate(ks):
            _, origin = _flipped(pos, k)
            for i, op in enumerate(ops):
                s = i * len(ks) + j
                region = op[4](lands[i], origin)
                copies.append(pltpu.make_async_remote_copy(
                    src_ref=region, dst_ref=region, send_sem=send.at[s], recv_sem=recv.at[s],
                    device_id=sibling_id, device_id_type=pl.DeviceIdType.MESH))
        return copies
    return n * len(ks), build


def exchange_start(name, arrays, plan):
    n_sems, build = plan
    n = len(arrays)

    def body(*refs):
        for cp in build(refs[:n], refs[n], refs[n + 1]):
            cp.start()
        refs[-1][...] = jnp.zeros_like(refs[-1])

    args = [pltpu.with_memory_space_constraint(a, pltpu.HBM) for a in arrays]
    sems = pltpu.SemaphoreType.DMA((n_sems,))
    outs = pl.pallas_call(
        body, name=name,
        out_shape=(sems, sems, *[pltpu.HBM(a.shape, a.dtype) for a in args], jax.ShapeDtypeStruct((8, LANE), F32)),
        in_specs=[_HBM] * n,
        out_specs=(_SEM, _SEM, *[_HBM] * n, pl.BlockSpec(memory_space=pltpu.VMEM)),
        input_output_aliases={i: 2 + i for i in range(n)},
        compiler_params=pltpu.CompilerParams(has_side_effects=_EFFECT),
    )(*args)
    return (build, outs[0], outs[1], outs[2:2 + n]), outs[-1]


def exchange_wait(name, handle, after):
    build, send, recv, thru = handle
    n = len(thru)

    def body(*refs):
        for cp in build(refs[:n], refs[n], refs[n + 1]):
            cp.wait_send()
            cp.wait_recv()

    return pl.pallas_call(
        body, name=name,
        out_shape=[pltpu.HBM(a.shape, a.dtype) for a in thru],
        in_specs=[_HBM] * n + [_SEM, _SEM, pl.BlockSpec(memory_space=pl.ANY)],
        out_specs=[_HBM] * n,
        input_output_aliases={i: i for i in range(n)},
        compiler_params=pltpu.CompilerParams(has_side_effects=_EFFECT),
    )(*thru, send, recv, after)


def _whole(ref, q):
    return ref


def _slot(ref, q):
    return ref.at[q]


def op_gather_stack(x):
    return (x, (N_DEV,) + x.shape, x.dtype, _whole, _slot)


def op_gather_axis(x, axis):
    size = x.shape[axis]
    shape = x.shape[:axis] + (N_DEV * size,) + x.shape[axis + 1:]

    def dst(ref, q):
        idx = [slice(None)] * len(shape)
        idx[axis] = pl.ds(pl.multiple_of(q * size, size), size)
        return ref.at[tuple(idx)]
    return (x, shape, x.dtype, _whole, dst, ("gather", axis))


def op_scatter_axis(x, axis):
    size = x.shape[axis] // N_DEV
    shape = x.shape[:axis] + (size,) + x.shape[axis + 1:]

    def src(ref, q):
        idx = [slice(None)] * len(shape)
        idx[axis] = pl.ds(pl.multiple_of(q * size, size), size)
        return ref.at[tuple(idx)]
    return (x, (N_DEV,) + shape, x.dtype, src, _slot, ("scatter", axis))


def _adamw(w, g, m, v):
    m = ADAM_B1 * m + (1.0 - ADAM_B1) * g
    v = ADAM_B2 * v + (1.0 - ADAM_B2) * (g * g)
    m_hat = m / (1.0 - ADAM_B1 ** ADAM_STEP)
    v_hat = v / (1.0 - ADAM_B2 ** ADAM_STEP)
    delta = -ADAM_LR * (m_hat / (jnp.sqrt(v_hat) + ADAM_EPS) + ADAM_WD * w)
    return delta, m, v


def adamw_update(name, w, m, v, *, layer=None, landed=None, grad=None, into=None, deps=()):
    R, W = w.shape[-2:]
    n_g = N_DEV if landed is not None else 1

    def fn(rows, vecs, c, i):
        g = rows[0].astype(F32)
        for q in range(1, n_g):
            g = g + rows[q].astype(F32)
        wv, mv, vv = rows[n_g:]
        delta, m2, v2 = _adamw(wv, g, mv, vv)
        return [[g], [delta], [m2], [v2]], []

    g_ins = [_ri(landed, lead=q) for q in range(N_DEV)] if landed is not None else [_ri(grad)]
    cw = LANE if W % LANE == 0 else W
    return rowwise(name, fn, R, g_ins + [_ri(w, lead=layer), _ri(m, lead=layer), _ri(v, lead=layer)], [],
                   [(F32, 1)] * 4, W=W, tr=128, cw=cw, rc=32,
                   out_layer=None if layer is None else (layer, w.shape[0]), into=into, deps=deps)


def kernel(x, c, w_ada, b_ada, w_in, sink, w_dw, conv_ln_g, conv_ln_b, w_oa, w_ob, w_out, ln1_g, ln1_b, w_gu, w_down, ln2_g, ln2_b, loss_target, m_w_ada, m_b_ada, m_w_in, m_sink, m_w_dw, m_conv_ln_g, m_conv_ln_b, m_w_oa, m_w_ob, m_w_out, m_ln1_g, m_ln1_b, m_w_gu, m_w_down, m_ln2_g, m_ln2_b, v_w_ada, v_b_ada, v_w_in, v_sink, v_w_dw, v_conv_ln_g, v_conv_ln_b, v_w_oa, v_w_ob, v_w_out, v_ln1_g, v_ln1_b, v_w_gu, v_w_down, v_ln2_g, v_ln2_b):
    L = w_ada.shape[0]
    S, D = x.shape[1], x.shape[2]
    Dkv = D // GQA_GROUP
    Dqkv = D + 2 * Dkv
    DFF = w_down.shape[1] * N_DEV
    nq, nkv, nb = D // HEAD_DIM, Dkv // HEAD_DIM, S // BLOCK
    alpha = (2.0 * L) ** 0.25
    me = 4 * lax.axis_index("x") + 2 * lax.axis_index("y") + lax.axis_index("c")
    x0 = x.reshape(S, D)
    target = loss_target.reshape(S, D)
    t_cos, t_lo, t_hi = rope_tables(S)

    c_act = jax.nn.silu(c)
    c_all = exchange("gather_c", [op_gather_stack(c_act)])[0].reshape(N_DEV, D)
    c_pad = jnp.concatenate([c_all, jnp.zeros_like(c_all)], axis=0).astype(BF16)
    ada_cols = w_ada.shape[2]
    mod_part = jnp.stack([matmul("mod_mm", c_pad, w_ada[l], "nn", F32)[:N_DEV] for l in range(L)], axis=1)
    mod_land = exchange("scatter_mod", [op_scatter_axis(mod_part, 0)])[0]
    mod = jnp.transpose(mod_land.reshape(N_DEV, L, ada_cols), (1, 0, 2)).reshape(L, N_MOD * D) + b_ada
    mods = [[mod[l:l + 1, j * D:(j + 1) * D] for j in range(N_MOD)] for l in range(L)]

    me_arr = me.astype(jnp.int32).reshape(1)

    def begin(tag, ops, dep, ks=ALL_PEERS):
        lands = exchange_local(tag + "_local", ops, me_arr, [dep])
        return exchange_start(tag + "_start", [op[0] for op in ops] + list(lands), direct_copies(ops, ks))

    def gather_begin(tag, ops, dep):
        return begin(tag, ops, dep, (SIBLING,) + SAME_CORE)

    def gather_forward(tag, ops, handle, after):
        lands = exchange_wait(tag + "_wait", handle, after)[len(ops):]
        return exchange_start(tag + "_fwd_start", list(lands), forward_copies(ops, SAME_CORE))

    def gather_end(tag, handle, after):
        return exchange_wait(tag + "_fwd_wait", handle, after)

    def gather_first(l):
        return [op_gather_axis(w_in[l].astype(BF16), 1), op_gather_axis(w_dw[l], 1)]

    def gather_rest(l):
        return [op_gather_axis(w_gu[l].astype(BF16), 1), op_gather_axis(w_oa[l].astype(BF16), 0),
                op_gather_axis(w_ob[l].astype(BF16), 0), op_gather_axis(w_out[l].astype(BF16), 0),
                op_gather_axis(w_down[l].astype(BF16), 0)]

    W_in, W_gu, W_oa, W_ob, W_out, W_down, W_dw = ([None] * L for _ in range(7))
    first_ops = gather_first(0)
    ag_first, _ = gather_begin("ag0a", first_ops, mod_land)

    def vec(a, l):
        return a[l:l + 1]

    def f_mod(rows, vecs, c_, i_):
        (xv,), (sc, sh) = rows, vecs
        return [[xv * (1.0 + sc) + sh]], []

    cwl = min(LN_CHUNK, D)
    ln_chunks = range(0, D, cwl)

    def f_convln(load, vec_, store, add_sum, i_):
        mu, rstd = _ln_stats(lambda c0: load(0, c0, cwl), D, cwl)
        for c0 in ln_chunks:
            nrm = (load(0, c0, cwl) - mu) * rstd * vec_(0, c0, cwl) + vec_(1, c0, cwl)
            store(0, 0, c0, nrm * _sig(nrm))

    def f_merge(rows, vecs, c_, i_):
        g_a, g_b, y_a, y_b = [r.astype(F32) for r in rows]
        return [[_sig(g_a) * y_a + _sig(g_b) * y_b]], []

    def f_ln(load, vec_, store, add_sum, i_):
        def t_of(c0):
            return alpha * load(0, c0, cwl) + (1.0 + vec_(0, c0, cwl)) * load(1, c0, cwl)
        mu, rstd = _ln_stats(t_of, D, cwl)
        for c0 in ln_chunks:
            y = (t_of(c0) - mu) * rstd * vec_(1, c0, cwl) + vec_(2, c0, cwl)
            store(0, 0, c0, y)
            store(1, 0, c0, y * (1.0 + vec_(3, c0, cwl)) + vec_(4, c0, cwl))

    saved = []
    xl = x0
    h = rowwise("modulate", f_mod, S, [_ri(x0)], [mods[0][1], mods[0][0]], [(BF16, 1)], W=D)[0]
    ag_first, _ = gather_forward("ag0a", first_ops, ag_first, h)
    W_in[0], W_dw[0] = gather_end("ag0a", ag_first, h)
    ag_next, next_ops = None, None

    def after_token(token):
        return [token] if token is not None else []

    def rope_heads(tile, extras, j):
        tc_, tl, th = extras
        heads = []
        for hh in range(tile.shape[1] // HEAD_DIM):
            xh = tile[:, hh * HEAD_DIM:(hh + 1) * HEAD_DIM]
            is_qk = j * (tile.shape[1] // HEAD_DIM) + hh < nq + nkv
            heads.append(jnp.where(is_qk, _rope(xh, tc_, tl, th, 1.0), xh))
        return jnp.concatenate(heads, axis=1)

    for l in range(L):
        sh_a, sc_a, gt_a, sh_f, sc_f, gt_f = mods[l]
        if l == 0:
            rest_ops = gather_rest(0)
            ag_rest, token = gather_begin("ag0b", rest_ops, W_in[0])
        elif l + 1 < L:
            next_ops = gather_first(l + 1) + gather_rest(l + 1)
            ag_next, token = gather_begin("ag%d" % (l + 1), next_ops, W_in[l])
        else:
            token = None
        qkv = matmul("in_qkv", h, W_in[l], "nn", BF16, n=Dqkv, deps=after_token(token),
                     epilogue=rope_heads, row_extras=[t_cos, t_lo, t_hi])
        p_glu = matmul("in_glu", h, W_in[l], "nn", F32, b_off=Dqkv, n=2 * D)
        p_gate = matmul("in_gate", h, W_in[l], "nn", BF16, b_off=Dqkv + 2 * D, n=2 * D)
        sinkcol = jnp.repeat(sink[l].reshape(nkv, GQA_GROUP), BLOCK, axis=1).reshape(nkv, GQA_GROUP * BLOCK, 1)
        att = attn_fwd(qkv, sinkcol, S, D, Dkv)
        uc = conv_fwd(p_glu, W_dw[l], S, D)
        token = None
        if l == 0:
            ag_rest, _ = gather_forward("ag0b", rest_ops, ag_rest, uc)
            if L > 1:
                next_ops = gather_first(1) + gather_rest(1)
                ag_next, token = gather_begin("ag1", next_ops, ag_rest[3][0])
            W_gu[0], W_oa[0], W_ob[0], W_out[0], W_down[0] = gather_end("ag0b", ag_rest, uc)
        y_a = matmul("oa", att, W_oa[l], "nn", BF16, deps=after_token(token))
        z = rowwise("conv_ln", f_convln, S, [_ri(uc)], [vec(conv_ln_g, l), vec(conv_ln_b, l)], [(BF16, 1)], W=D,
                    lazy=True, rc=LN_ROWS)[0]
        y_b = matmul("ob", z, W_ob[l], "nn", BF16)
        merged = rowwise("merge", f_merge, S, [_ri(p_gate, col=0), _ri(p_gate, col=1), _ri(y_a), _ri(y_b)], [],
                         [(BF16, 1)], W=D, cw=_pick(D, 512))[0]
        r1 = matmul("out", merged, W_out[l], "nn", BF16)
        x1, h2 = rowwise("ln1", f_ln, S, [_ri(xl), _ri(r1)], [gt_a, vec(ln1_g, l), vec(ln1_b, l), sc_f, sh_f],
                         [(F32, 1), (BF16, 1)], W=D, lazy=True, rc=LN_ROWS)
        act, gate, up = matmul_swiglu(h2, W_gu[l], DFF)
        token = None
        if l + 1 < L:
            ag_next, token = gather_forward("ag%d" % (l + 1), next_ops, ag_next, act)
        f = matmul("down", act, W_down[l], "nn", BF16, deps=after_token(token))
        nsc, nsh = (mods[l + 1][1], mods[l + 1][0]) if l + 1 < L else (sc_a, sh_a)
        x2, h_next = rowwise("ln2", f_ln, S, [_ri(x1), _ri(f)], [gt_f, vec(ln2_g, l), vec(ln2_b, l), nsc, nsh],
                             [(F32, 1), (BF16, 1)], W=D, lazy=True, rc=LN_ROWS)
        saved.append(dict(x=xl, h=h, qkv=qkv, sinkcol=sinkcol, att=att, p_glu=p_glu, p_gate=p_gate, y_a=y_a, y_b=y_b,
                          uc=uc, z=z, merged=merged, r1=r1, x1=x1, h2=h2, gate=gate, up=up, act=act, f=f))
        xl, h = x2, h_next
        if l + 1 < L:
            (W_in[l + 1], W_dw[l + 1], W_gu[l + 1], W_oa[l + 1], W_ob[l + 1], W_out[l + 1],
             W_down[l + 1]) = gather_end("ag%d" % (l + 1), ag_next, x2)

    def f_loss(rows, vecs, c_, i_):
        y, t = rows
        e = y - t
        return [[e * (1.0 / D)]], [e * e]

    dy, err = rowwise("loss", f_loss, S, [_ri(xl), _ri(target)], [], [(F32, 1)], 1, W=D)
    loss = lax.psum(0.5 * jnp.sum(err) / D, AXES)

    def ln_bwd_passes(t_of, dout_of, g_of, r_of, gt_of, store, add_sum):
        mu, rstd = _ln_stats(t_of, D, cwl)
        m1 = _row_mean(lambda c0: dout_of(c0) * g_of(c0), D, cwl)
        m2 = _row_mean(lambda c0: dout_of(c0) * g_of(c0) * ((t_of(c0) - mu) * rstd), D, cwl)
        for c0 in ln_chunks:
            dout, xhat = dout_of(c0), (t_of(c0) - mu) * rstd
            dt = rstd * (dout * g_of(c0) - m1 - xhat * m2)
            store(0, 0, c0, (1.0 + gt_of(c0)) * dt)
            store(1, 0, c0, alpha * dt)
            add_sum(0, c0, dout * xhat)
            add_sum(1, c0, dout)
            add_sum(2, c0, dt * r_of(c0))

    def f_ln_bwd_last(load, vec_, store, add_sum, i_):
        ln_bwd_passes(lambda c0: alpha * load(1, c0, cwl) + (1.0 + vec_(0, c0, cwl)) * load(2, c0, cwl),
                      lambda c0: load(0, c0, cwl), lambda c0: vec_(1, c0, cwl), lambda c0: load(2, c0, cwl),
                      lambda c0: vec_(0, c0, cwl), store, add_sum)

    def f_ln_bwd(load, vec_, store, add_sum, i_):
        ln_bwd_passes(lambda c0: alpha * load(3, c0, cwl) + (1.0 + vec_(1, c0, cwl)) * load(4, c0, cwl),
                      lambda c0: load(0, c0, cwl) + load(1, c0, cwl).astype(F32) * (1.0 + vec_(0, c0, cwl)),
                      lambda c0: vec_(2, c0, cwl), lambda c0: load(4, c0, cwl), lambda c0: vec_(1, c0, cwl),
                      store, add_sum)
        for c0 in ln_chunks:
            dh = load(1, c0, cwl).astype(F32)
            add_sum(3, c0, dh * load(2, c0, cwl))
            add_sum(4, c0, dh)

    def f_swiglu_bwd(rows, vecs, c_, i_):
        da, gate, up = rows[0].astype(F32), rows[1].astype(F32), rows[2].astype(F32)
        sg = _sig(gate)
        return [[da * up * sg * (1.0 + gate * (1.0 - sg)), da * (gate * sg)]], []

    def f_gate_bwd(rows, vecs, c_, i_):
        dm, g_a, g_b, y_a, y_b = [r.astype(F32) for r in rows]
        sa, sb = _sig(g_a), _sig(g_b)
        return [[dm * sa], [dm * sb], [dm * y_a * sa * (1.0 - sa), dm * y_b * sb * (1.0 - sb)]], []

    def f_convln_bwd(load, vec_, store, add_sum, i_):
        mu, rstd = _ln_stats(lambda c0: load(1, c0, cwl), D, cwl)

        def parts(c0):
            xhat = (load(1, c0, cwl) - mu) * rstd
            nrm = xhat * vec_(0, c0, cwl) + vec_(1, c0, cwl)
            sg = _sig(nrm)
            return load(0, c0, cwl).astype(F32) * sg * (1.0 + nrm * (1.0 - sg)), xhat
        m1 = _row_mean(lambda c0: parts(c0)[0] * vec_(0, c0, cwl), D, cwl)

        def dyg_xhat(c0):
            dn, xhat = parts(c0)
            return dn * vec_(0, c0, cwl) * xhat
        m2 = _row_mean(dyg_xhat, D, cwl)
        for c0 in ln_chunks:
            dn, xhat = parts(c0)
            store(0, 0, c0, rstd * (dn * vec_(0, c0, cwl) - m1 - xhat * m2))
            add_sum(0, c0, dn * xhat)
            add_sum(1, c0, dn)

    def f_kv_combine(rows, vecs, c_, i_):
        k_lo, k_mid, k_hi, v_lo, v_mid, v_hi, tc_, tl, th = rows
        lo, hi = i_ > 0, i_ < nb - 1
        dk = jnp.where(lo, k_lo, 0.0) + k_mid + jnp.where(hi, k_hi, 0.0)
        dv = jnp.where(lo, v_lo, 0.0) + v_mid + jnp.where(hi, v_hi, 0.0)
        return [[_rope(dk, tc_, tl, th, -1.0)], [dv]], []

    def f_mod_bwd(rows, vecs, c_, i_):
        (dres, dh, xv), (sc,) = rows, vecs
        dh = dh.astype(F32)
        return [[dres + dh * (1.0 + sc)]], [dh * xv, dh]

    def flat(a):
        return a.reshape(-1, a.shape[-1])

    big_names = ("w_in", "w_gu", "w_oa", "w_ob", "w_out", "w_down")
    big_w = dict(w_in=w_in, w_gu=w_gu, w_oa=w_oa, w_ob=w_ob, w_out=w_out, w_down=w_down)
    big_m = dict(w_in=m_w_in, w_gu=m_w_gu, w_oa=m_w_oa, w_ob=m_w_ob, w_out=m_w_out, w_down=m_w_down)
    big_v = dict(w_in=v_w_in, w_gu=v_w_gu, w_oa=v_w_oa, w_ob=v_w_ob, w_out=v_w_out, w_down=v_w_down)
    big_res = {nm: None for nm in big_names}
    dmod = [None] * L
    small = dict(sink=[None] * L, conv_ln_g=[None] * L, conv_ln_b=[None] * L, ln1_g=[None] * L, ln1_b=[None] * L,
                 ln2_g=[None] * L, ln2_b=[None] * L)
    dwdw = [None] * L

    first, second, last = ("w_down", "w_gu"), ("w_out", "w_oa", "w_ob"), ("w_in",)

    deferred = []

    def run_adamw(nm, l, land, deps=()):
        big_res[nm] = adamw_update("adamw_" + nm, big_w[nm], big_m[nm], big_v[nm], layer=l, landed=land,
                                   into=big_res[nm], deps=deps)

    def finish(names, tag, handle, after, l, defer=True):
        landed = exchange_wait(tag + "_wait", handle, after)[len(names):]
        for nm, land in zip(names, landed):
            if defer:
                deferred.append((nm, l, land))
            else:
                run_adamw(nm, l, land)
        return landed[0]

    dres, dh_next = dy, None
    rs_last, token = None, None
    for l in reversed(range(L)):
        sv = saved[l]
        sh_a, sc_a, gt_a, sh_f, sc_f, gt_f = mods[l]
        if dh_next is None:
            df, dres, d_g2, d_b2, d_gtf = rowwise(
                "ln2_bwd_last", f_ln_bwd_last, S, [_ri(dres), _ri(sv["x1"]), _ri(sv["f"])], [gt_f, vec(ln2_g, l)],
                [(BF16, 1), (F32, 1)], 3, W=D, lazy=True, rc=LN_ROWS)
            d_sca_next = d_sha_next = None
        else:
            df, dres, d_g2, d_b2, d_gtf, d_sca_next, d_sha_next = rowwise(
                "ln2_bwd", f_ln_bwd, S, [_ri(dres), _ri(dh_next), _ri(saved[l + 1]["x"]), _ri(sv["x1"]), _ri(sv["f"])],
                [mods[l + 1][1], gt_f, vec(ln2_g, l)], [(BF16, 1), (F32, 1)], 5, W=D, lazy=True, rc=LN_ROWS)
            dmod[l + 1][1], dmod[l + 1][0] = d_sca_next, d_sha_next
        dmod[l] = [None] * N_MOD
        dmod[l][5] = d_gtf
        small["ln2_g"][l], small["ln2_b"][l] = d_g2, d_b2
        g_down = matmul("d_w_down", sv["act"], df, "tn", BF16, deps=after_token(token))
        dact = matmul("d_act", df, W_down[l], "nt", BF16)
        dgu = rowwise("swiglu_bwd", f_swiglu_bwd, S, [_ri(dact), _ri(sv["gate"]), _ri(sv["up"])], [],
                      [(BF16, 2)], W=DFF, tr=128, cw=_pick(DFF, 512))[0]
        g_gu = matmul("d_w_gu", sv["h2"], dgu, "tn", BF16)
        rs_first, token = begin("rsa%d" % l, [op_scatter_axis(g_down, 0), op_scatter_axis(g_gu, 1)], g_gu)
        dh2 = matmul("d_h2", dgu, W_gu[l], "nt", BF16, deps=[token])
        dr1, dres, d_g1, d_b1, d_gta, d_scf, d_shf = rowwise(
            "ln1_bwd", f_ln_bwd, S, [_ri(dres), _ri(dh2), _ri(sv["x1"]), _ri(sv["x"]), _ri(sv["r1"])],
            [sc_f, gt_a, vec(ln1_g, l)], [(BF16, 1), (F32, 1)], 5, W=D, lazy=True, rc=LN_ROWS)
        dmod[l][2], dmod[l][4], dmod[l][3] = d_gta, d_scf, d_shf
        small["ln1_g"][l], small["ln1_b"][l] = d_g1, d_b1
        g_out = matmul("d_w_out", sv["merged"], dr1, "tn", BF16)
        dmerged = matmul("d_merged", dr1, W_out[l], "nt", BF16)
        if rs_last is not None:
            finish(last, "rsc%d" % (l + 1), rs_last, dmerged, l + 1)
        dy_a, dy_b, dp_gate = rowwise(
            "gate_bwd", f_gate_bwd, S,
            [_ri(dmerged), _ri(sv["p_gate"], col=0), _ri(sv["p_gate"], col=1), _ri(sv["y_a"]), _ri(sv["y_b"])], [],
            [(BF16, 1), (BF16, 1), (BF16, 2)], W=D, cw=_pick(D, 512))
        g_oa = matmul("d_w_oa", sv["att"], dy_a, "tn", BF16)
        datt = matmul("d_att", dy_a, W_oa[l], "nt", BF16)
        g_ob = matmul("d_w_ob", sv["z"], dy_b, "tn", BF16)
        dz = matmul("d_z", dy_b, W_ob[l], "nt", BF16)
        duc, d_cg, d_cb = rowwise("conv_ln_bwd", f_convln_bwd, S, [_ri(dz), _ri(sv["uc"])],
                                  [vec(conv_ln_g, l), vec(conv_ln_b, l)], [(F32, 1)], 2, W=D, lazy=True, rc=LN_ROWS)
        small["conv_ln_g"][l], small["conv_ln_b"][l] = d_cg, d_cb
        dga, dgb, dwdw[l] = conv_bwd(duc, sv["p_glu"], W_dw[l], S, D)
        before = finish(first, "rsa%d" % l, rs_first, dga, l)
        rs_second, token = begin("rsb%d" % l, [op_scatter_axis(g_out, 0), op_scatter_axis(g_oa, 0),
                                               op_scatter_axis(g_ob, 0)], before)
        dq_r, dkp, dvp, dsink = attn_bwd(sv["qkv"], sv["sinkcol"], datt, (t_cos, t_lo, t_hi), S, D, Dkv)
        small["sink"][l] = dsink[:, :GQA_GROUP, 0].reshape(1, nq)
        dk_r, dv_r = rowwise(
            "kv_combine", f_kv_combine, S,
            [_ri(dkp, lead=2, shift=-1), _ri(dkp, lead=1), _ri(dkp, lead=0, shift=1),
             _ri(dvp, lead=2, shift=-1), _ri(dvp, lead=1), _ri(dvp, lead=0, shift=1),
             _ri(t_cos, whole=True), _ri(t_lo, whole=True), _ri(t_hi, whole=True)],
            [], [(BF16, 1), (BF16, 1)], W=Dkv, tr=BLOCK, cw=HEAD_DIM, rc=32)
        dp = jnp.concatenate([dq_r, dk_r, dv_r, dga, dgb, dp_gate], axis=1)
        g_in = matmul("d_w_in", sv["h"], dp, "tn", BF16, deps=[token])
        rs_last, token = begin("rsc%d" % l, [op_scatter_axis(g_in, 1)], g_in)
        dh_next = matmul("d_h", dp, W_in[l], "nt", BF16, deps=[token])
        finish(second, "rsb%d" % l, rs_second, dh_next, l)
        token = None

    grad_x, d_sca0, d_sha0 = rowwise("mod_bwd", f_mod_bwd, S, [_ri(dres), _ri(dh_next), _ri(x0)], [mods[0][1]],
                                     [(F32, 1)], 2, W=D)
    dmod[0][1], dmod[0][0] = d_sca0, d_sha0
    for nm, l, land in deferred:
        run_adamw(nm, l, land, deps=[grad_x])
    finish(last, "rsc0", rs_last, big_res["w_gu"][0], 0, defer=False)

    small_names = ("b_ada", "sink", "conv_ln_g", "conv_ln_b", "ln1_g", "ln1_b", "ln2_g", "ln2_b")
    small_w = dict(b_ada=b_ada, sink=sink, conv_ln_g=conv_ln_g, conv_ln_b=conv_ln_b, ln1_g=ln1_g, ln1_b=ln1_b,
                   ln2_g=ln2_g, ln2_b=ln2_b)
    small_m = dict(b_ada=m_b_ada, sink=m_sink, conv_ln_g=m_conv_ln_g, conv_ln_b=m_conv_ln_b, ln1_g=m_ln1_g,
                   ln1_b=m_ln1_b, ln2_g=m_ln2_g, ln2_b=m_ln2_b)
    small_v = dict(b_ada=v_b_ada, sink=v_sink, conv_ln_g=v_conv_ln_g, conv_ln_b=v_conv_ln_b, ln1_g=v_ln1_g,
                   ln1_b=v_ln1_b, ln2_g=v_ln2_g, ln2_b=v_ln2_b)
    small_g = dict(small)
    small_g["b_ada"] = [jnp.concatenate(dmod[l], axis=1) for l in range(L)]
    sizes = [small_w[nm].size for nm in small_names]
    total = sum(sizes)
    padded = -(-total // (SMALL_W * ROW_CHUNK)) * (SMALL_W * ROW_CHUNK)

    def pack(parts):
        flat_ = jnp.concatenate([p.reshape(-1) for p in parts] + [jnp.zeros((padded - total,), F32)])
        return flat_.reshape(padded // SMALL_W, SMALL_W)

    g_pack = pack([jnp.concatenate(small_g[nm], axis=0) for nm in small_names])
    small_land, dwdw_land = exchange("gather_small", [op_gather_stack(g_pack),
                                                      op_scatter_axis(jnp.stack(dwdw, axis=0), 2)],
                                     deps=[big_res["w_in"][0]])
    small_out = adamw_update("adamw_small", pack([small_w[nm] for nm in small_names]),
                             pack([small_m[nm] for nm in small_names]), pack([small_v[nm] for nm in small_names]),
                             landed=small_land)

    def unpack(buf):
        flat_, out, o = buf.reshape(-1), {}, 0
        for nm, sz in zip(small_names, sizes):
            out[nm] = flat_[o:o + sz].reshape(small_w[nm].shape)
            o += sz
        return out
    small_res = [unpack(b) for b in small_out]

    dw_cols = w_dw.shape[2]

    def pad_dw(a):
        return jnp.pad(a, ((0, 0), (0, 32 - CONV_WIDTH), (0, 0))).reshape(L * 32, dw_cols)
    dw_out = adamw_update("adamw_w_dw", pad_dw(w_dw), pad_dw(m_w_dw), pad_dw(v_w_dw),
                          landed=dwdw_land.reshape(N_DEV, L * 32, dw_cols))
    dw_res = [a.reshape(L, 32, dw_cols)[:, :CONV_WIDTH] for a in dw_out]

    dmod_all = small_land.reshape(N_DEV, -1)[:, :L * N_MOD * D].reshape(N_DEV, L, N_MOD * D)
    dmod_mine = lax.dynamic_slice_in_dim(dmod_all, me * ada_cols, ada_cols, axis=2)
    dmod_pad = jnp.concatenate([dmod_mine, jnp.zeros_like(dmod_mine)], axis=0).astype(BF16)
    g_ada = jnp.stack([matmul("d_w_ada", c_pad, dmod_pad[:, l], "tn", F32) for l in range(L)], axis=0)
    ada_out = adamw_update("adamw_w_ada", flat(w_ada), flat(m_w_ada), flat(v_w_ada), grad=flat(g_ada))
    ada_res = [a.reshape(w_ada.shape) for a in ada_out]

    order = ("w_ada", "b_ada", "w_in", "sink", "w_dw", "conv_ln_g", "conv_ln_b", "w_oa", "w_ob", "w_out",
             "ln1_g", "ln1_b", "w_gu", "w_down", "ln2_g", "ln2_b")

    def result(nm, j):
        if nm == "w_ada":
            return ada_res[j]
        if nm == "w_dw":
            return dw_res[j]
        if nm in big_res:
            return big_res[nm][j]
        return small_res[j][nm]

    outs = [loss, grad_x.reshape(x.shape)]
    for j in range(4):
        outs += [result(nm, j) for nm in order]
    return tuple(outs)
```
